```python
import math
import jax, jax.numpy as jnp
from jax import lax
import numpy as np

D_MODEL = 1024
BATCH = 4
SEQ = 4096
DEPTH = 2
DEC_BATCH = 8
DEC_SEQ = 2048
PAST_LEN = 128

HEAD_DIM = 64
DIFF_HEADS = 4
DIFF_WIDTH = DIFF_HEADS * 2 * HEAD_DIM
RWKV_HEADS = 8
RWKV_WIDTH = RWKV_HEADS * HEAD_DIM
DECAY_LORA = 64
AAA_LORA = 64
GATE_LORA = 160
RWKV_GN_EPS = 64e-5
DIL_PAIRS = ((128, 1), (512, 4), (2048, 16))
DIL_HEADS_PER_GROUP = 4
DIL_WIDTH = len(DIL_PAIRS) * DIL_HEADS_PER_GROUP * HEAD_DIM
DIL_BLOCK = 128
Q_BLOCK = 128
FFN_HIDDEN = -(-8 * D_MODEL // (3 * 256)) * 256
ROPE_THETA = 10000.0
NORM_EPS = 1e-6
SUBLN_EPS = 1e-5
NEG_INF = -1e30

kernel_name = "hybrid_diffattn_rwkv7_dilated_encoder"


def rms_norm(x, w, eps=NORM_EPS):
    xf = x.astype(jnp.float32)
    y = xf * lax.rsqrt(jnp.mean(xf * xf, axis=-1, keepdims=True) + eps)
    return (y * w.astype(jnp.float32)).astype(x.dtype)


def rope(x):
    S, d = x.shape[1], x.shape[-1]
    half = d // 2
    inv = ROPE_THETA ** (-jnp.arange(half, dtype=jnp.float32) / half)
    ang = jnp.arange(S, dtype=jnp.float32)[:, None] * inv[None, :]
    shape = (1, S) + (1,) * (x.ndim - 3) + (half,)
    cos = jnp.cos(ang).reshape(shape)
    sin = jnp.sin(ang).reshape(shape)
    xf = x.astype(jnp.float32)
    x1, x2 = xf[..., :half], xf[..., half:]
    return jnp.concatenate([x1 * cos - x2 * sin, x2 * cos + x1 * sin], axis=-1).astype(x.dtype)


def centred_shift(x):
    z = jnp.zeros_like(x[:, :1])
    prev = jnp.concatenate([z, x[:, :-1]], axis=1)
    nxt = jnp.concatenate([x[:, 1:], z], axis=1)
    return 0.5 * (prev + nxt)


def swiglu(x, wg, wu, wd):
    return (jax.nn.silu(x @ wg) * (x @ wu)) @ wd


def diff_attention(q, k, v, lam, lambda_init, subln_w):
    B, S, H, _, d = q.shape
    nq = S // Q_BLOCK
    qb = (q * (d ** -0.5)).reshape(B, nq, Q_BLOCK, H, 2, d).transpose(1, 0, 2, 3, 4, 5)
    vf = v.astype(jnp.float32)

    def block(qi):
        s = jnp.einsum('bqhcd,bkhcd->bhcqk', qi, k, preferred_element_type=jnp.float32)
        p = jax.nn.softmax(s, axis=-1)
        pd = p[:, :, 0] - lam * p[:, :, 1]
        return jnp.einsum('bhqk,bkhe->bqhe', pd, vf)

    o = lax.map(block, qb)
    o = o.transpose(1, 0, 2, 3, 4).reshape(B, S, H, 2 * d)
    o = rms_norm(o, subln_w, SUBLN_EPS) * (1.0 - lambda_init)
    return o.astype(q.dtype)


def rwkv_scan(r, w, k, v, kk, a, reverse):
    B, S, H, N = r.shape
    xs = tuple(t.transpose(1, 0, 2, 3) for t in (r, w, k, v, kk, a))

    def step(st, inp):
        r_t, w_t, k_t, v_t, kk_t, a_t = inp
        sa = jnp.einsum('bhvk,bhk->bhv', st, -kk_t)
        st = (st * w_t[:, :, None, :] + sa[..., None] * (kk_t * a_t)[:, :, None, :]
              + v_t[..., None] * k_t[:, :, None, :])
        y = jnp.einsum('bhvk,bhk->bhv', st, r_t)
        return st, y

    s0 = jnp.zeros((B, H, N, N), jnp.float32)
    _, ys = lax.scan(step, s0, xs, reverse=reverse)
    return ys.transpose(1, 0, 2, 3)


def banded_attention(q, k, v, radius):
    N, L, d = q.shape
    Q = min(DIL_BLOCK, L)
    nblk = -(-L // Q)
    Lp = nblk * Q
    qp = jnp.pad(q, ((0, 0), (0, Lp - L), (0, 0))).reshape(N, nblk, Q, d)
    kpad = ((0, 0), (radius, Lp - L + radius), (0, 0))
    kp = jnp.pad(k, kpad)
    vp = jnp.pad(v, kpad)
    W = Q + 2 * radius
    idx = jnp.arange(nblk)[:, None] * Q + jnp.arange(W)[None, :]
    kw = kp[:, idx]
    vw = vp[:, idx].astype(jnp.float32)
    s = jnp.einsum('nbqd,nbkd->nbqk', qp, kw, preferred_element_type=jnp.float32)
    qpos = jnp.arange(nblk)[:, None] * Q + jnp.arange(Q)[None, :]
    kpos = idx - radius
    rel = kpos[:, None, :] - qpos[:, :, None]
    valid = (jnp.abs(rel) <= radius) & (kpos[:, None, :] >= 0) & (kpos[:, None, :] < L)
    s = jnp.where(valid[None], s, NEG_INF)
    lse = jax.nn.logsumexp(s, axis=-1)
    p = jnp.exp(s - lse[..., None])
    o = jnp.einsum('nbqk,nbkd->nbqd', p, vw)
    return o.reshape(N, Lp, d)[:, :L], lse.reshape(N, Lp)[:, :L]


def dilated_group(q, k, v, window, dilation):
    B, S, Hg, d = q.shape
    L = S // dilation
    radius = window // (2 * dilation)

    def fold(t):
        return t.reshape(B, L, dilation, Hg, d).transpose(0, 2, 3, 1, 4).reshape(B * dilation * Hg, L, d)

    o, lse = banded_attention(fold(q), fold(k), fold(v), radius)
    o = o.reshape(B, dilation, Hg, L, d).transpose(0, 3, 1, 2, 4).reshape(B, S, Hg, d)
    lse = lse.reshape(B, dilation, Hg, L).transpose(0, 3, 1, 2).reshape(B, S, Hg)
    return o, lse


def setup_inputs(seed: int = 0) -> dict:
    key = jax.random.key(seed)
    keys = iter(jax.random.split(key, 80))
    f32 = jnp.float32

    def nrm(shape, scale):
        return jax.random.normal(next(keys), shape, f32) * scale

    def gain(n):
        return 1.0 + nrm((n,), 0.05)

    def unif(shape, lo, hi):
        return jax.random.uniform(next(keys), shape, f32, lo, hi)

    D = D_MODEL
    p = {}
    p['x_prompt'] = nrm((BATCH, SEQ, D), 1.0)
    p['x_sample'] = nrm((DEC_BATCH, DEC_SEQ, D), 1.0)
    p['mix_pre0'] = gain(D)
    p['mix_post0'] = gain(D)
    p['w_in0'] = nrm((D, 3 * DIFF_WIDTH + 3 * RWKV_WIDTH), D ** -0.5)
    p['lam_q1'] = nrm((HEAD_DIM,), 0.1)
    p['lam_k1'] = nrm((HEAD_DIM,), 0.1)
    p['lam_q2'] = nrm((HEAD_DIM,), 0.1)
    p['lam_k2'] = nrm((HEAD_DIM,), 0.1)
    p['subln_w'] = gain(2 * HEAD_DIM)
    p['mu_r'] = unif((RWKV_WIDTH,), 0.0, 1.0)
    p['mu_k'] = unif((RWKV_WIDTH,), 0.0, 1.0)
    p['mu_v'] = unif((RWKV_WIDTH,), 0.0, 1.0)
    p['mu_w'] = unif((D,), 0.0, 1.0)
    p['mu_a'] = unif((D,), 0.0, 1.0)
    p['mu_g'] = unif((D,), 0.0, 1.0)
    for dname in ('f', 'b'):
        p['w0_' + dname] = unif((RWKV_WIDTH,), -6.0, -1.0)
        p['w1_' + dname] = nrm((D, DECAY_LORA), D ** -0.5)
        p['w2_' + dname] = nrm((DECAY_LORA, RWKV_WIDTH), 0.5 * DECAY_LORA ** -0.5)
    for dname in ('f', 'b'):
        p['a0_' + dname] = nrm((RWKV_WIDTH,), 0.1)
        p['a1_' + dname] = nrm((D, AAA_LORA), D ** -0.5)
        p['a2_' + dname] = nrm((AAA_LORA, RWKV_WIDTH), AAA_LORA ** -0.5)
    p['g1'] = nrm((D, GATE_LORA), D ** -0.5)
    p['g2'] = nrm((GATE_LORA, RWKV_WIDTH), GATE_LORA ** -0.5)
    p['k_k'] = 0.85 + nrm((RWKV_WIDTH,), 0.05)
    p['k_a'] = 1.0 + nrm((RWKV_WIDTH,), 0.05)
    p['r_k'] = nrm((RWKV_HEADS, HEAD_DIM), 0.1)
    p['lnx_w'] = gain(RWKV_WIDTH)
    p['lnx_b'] = nrm((RWKV_WIDTH,), 0.02)
    p['w_out0'] = nrm((DIFF_WIDTH + RWKV_WIDTH, D), (DIFF_WIDTH + RWKV_WIDTH) ** -0.5)
    p['ffn_pre0'] = gain(D)
    p['ffn_post0'] = gain(D)
    p['ffn_gate0'] = nrm((D, FFN_HIDDEN), D ** -0.5)
    p['ffn_up0'] = nrm((D, FFN_HIDDEN), D ** -0.5)
    p['ffn_down0'] = nrm((FFN_HIDDEN, D), FFN_HIDDEN ** -0.5)
    p['mix_pre1'] = gain(D)
    p['mix_post1'] = gain(D)
    p['w_in1'] = nrm((D, 3 * DIL_WIDTH), D ** -0.5)
    p['w_out1'] = nrm((DIL_WIDTH, D), DIL_WIDTH ** -0.5)
    p['ffn_pre1'] = gain(D)
    p['ffn_post1'] = gain(D)
    p['ffn_gate1'] = nrm((D, FFN_HIDDEN), D ** -0.5)
    p['ffn_up1'] = nrm((D, FFN_HIDDEN), D ** -0.5)
    p['ffn_down1'] = nrm((FFN_HIDDEN, D), FFN_HIDDEN ** -0.5)
    return p


def reference(x_prompt, x_sample, mix_pre0, mix_post0, w_in0, lam_q1, lam_k1, lam_q2, lam_k2, subln_w,
              mu_r, mu_k, mu_v, mu_w, mu_a, mu_g, w0_f, w1_f, w2_f, w0_b, w1_b, w2_b,
              a0_f, a1_f, a2_f, a0_b, a1_b, a2_b, g1, g2, k_k, k_a, r_k, lnx_w, lnx_b, w_out0,
              ffn_pre0, ffn_post0, ffn_gate0, ffn_up0, ffn_down0,
              mix_pre1, mix_post1, w_in1, w_out1, ffn_pre1, ffn_post1, ffn_gate1, ffn_up1, ffn_down1):
    f32 = jnp.float32

    def rwkv_mixer(xn, r_p, k_p, v_p):
        B, S, _ = xn.shape
        H, N = RWKV_HEADS, HEAD_DIM
        xx = centred_shift(xn) - xn
        xw = xn + xx * mu_w
        xa = xn + xx * mu_a
        xg = xn + xx * mu_g

        def shift_mix(t, mu):
            return (t + (centred_shift(t) - t) * mu).astype(f32)

        r = shift_mix(r_p, mu_r)
        k = shift_mix(k_p, mu_k)
        v = shift_mix(v_p, mu_v)

        def decay(w0, w1, w2):
            wl = -jax.nn.softplus(-(w0 + jnp.tanh(xw @ w1) @ w2).astype(f32)) - 0.5
            return jnp.exp(-jnp.exp(wl))

        def icl_rate(a0, a1, a2):
            return jax.nn.sigmoid((a0 + (xa @ a1) @ a2).astype(f32))

        g = (jax.nn.sigmoid(xg @ g1) @ g2).astype(f32)

        def heads(t):
            return t.reshape(B, S, H, N)

        kk = heads(k * k_k)
        kk = kk / jnp.maximum(jnp.sqrt(jnp.sum(kk * kk, axis=-1, keepdims=True)), 1e-12)
        rh, vh = heads(r), heads(v)
        a_f = icl_rate(a0_f, a1_f, a2_f)
        a_b = icl_rate(a0_b, a1_b, a2_b)
        k_f = k * (1.0 + (a_f - 1.0) * k_a)
        k_b = k * (1.0 + (a_b - 1.0) * k_a)
        y = (rwkv_scan(rh, heads(decay(w0_f, w1_f, w2_f)), heads(k_f), vh, kk, heads(a_f), False)
             + rwkv_scan(rh, heads(decay(w0_b, w1_b, w2_b)), heads(k_b), vh, kk, heads(a_b), True))
        mean = jnp.mean(y, axis=-1, keepdims=True)
        var = jnp.mean(jnp.square(y - mean), axis=-1, keepdims=True)
        y = ((y - mean) * lax.rsqrt(var + RWKV_GN_EPS)).reshape(B, S, H * N) * lnx_w + lnx_b
        bonus = jnp.sum(rh * heads(0.5 * (k_f + k_b)) * r_k, axis=-1, keepdims=True) * vh
        return ((y + bonus.reshape(B, S, H * N)) * g).astype(xn.dtype)

    def even_mixer(xn, layer):
        B, S, _ = xn.shape
        proj = xn @ w_in0
        qa, ka, va, rb, kb, vb = jnp.split(proj, 6, axis=-1)
        qa = rope(qa.reshape(B, S, DIFF_HEADS, 2, HEAD_DIM))
        ka = rope(ka.reshape(B, S, DIFF_HEADS, 2, HEAD_DIM))
        va = va.reshape(B, S, DIFF_HEADS, 2 * HEAD_DIM)
        lambda_init = 0.8 - 0.6 * math.exp(-0.3 * layer)
        lam = (jnp.exp(jnp.sum(lam_q1.astype(f32) * lam_k1.astype(f32)))
               - jnp.exp(jnp.sum(lam_q2.astype(f32) * lam_k2.astype(f32))) + lambda_init)
        out_a = diff_attention(qa, ka, va, lam, lambda_init, subln_w).reshape(B, S, DIFF_WIDTH)
        out_b = rwkv_mixer(xn, rb, kb, vb)
        return jnp.concatenate([out_a.astype(xn.dtype), out_b], axis=-1) @ w_out0

    def odd_mixer(xn):
        B, S, _ = xn.shape
        G, Hg = len(DIL_PAIRS), DIL_HEADS_PER_GROUP
        q, k, v = jnp.split(xn @ w_in1, 3, axis=-1)
        q = rope(q.reshape(B, S, G * Hg, HEAD_DIM)) * (HEAD_DIM ** -0.5)
        k = rope(k.reshape(B, S, G * Hg, HEAD_DIM))
        v = v.reshape(B, S, G * Hg, HEAD_DIM)
        outs, lses = [], []
        for gi, (window, dilation) in enumerate(DIL_PAIRS):
            sl = slice(gi * Hg, (gi + 1) * Hg)
            o, lse = dilated_group(q[:, :, sl], k[:, :, sl], v[:, :, sl], window, dilation)
            outs.append(o)
            lses.append(lse)
        alpha = jax.nn.softmax(jnp.stack(lses, axis=0), axis=0)
        y = jnp.concatenate([outs[gi] * alpha[gi][..., None] for gi in range(G)], axis=2)
        return y.reshape(B, S, DIL_WIDTH).astype(xn.dtype) @ w_out1

    def run(x):
        for layer in range(DEPTH):
            if layer % 2 == 0:
                m = even_mixer(rms_norm(x, mix_pre0), layer)
                x = x + rms_norm(m, mix_post0)
                f = swiglu(rms_norm(x, ffn_pre0), ffn_gate0, ffn_up0, ffn_down0)
                x = x + rms_norm(f, ffn_post0)
            else:
                m = odd_mixer(rms_norm(x, mix_pre1))
                x = x + rms_norm(m, mix_post1)
                f = swiglu(rms_norm(x, ffn_pre1), ffn_gate1, ffn_up1, ffn_down1)
                x = x + rms_norm(f, ffn_post1)
        return x

    y_prompt = run(x_prompt)
    y_sample = run(x_sample)
    return (y_prompt, y_sample)
```

```python
import functools
import math

import jax
import jax.numpy as jnp
from jax import lax
from jax.experimental import pallas as pl
from jax.experimental.pallas import tpu as pltpu

F32 = jnp.float32
BF16 = jnp.bfloat16
HIGHEST = lax.Precision.HIGHEST

D_MODEL = 1024
HEAD_DIM = 64
LANES = 128
DIFF_WIDTH = 512
RWKV_WIDTH = 512
N_PAIRS = RWKV_WIDTH // LANES
DIL_PAIRS = ((128, 1), (512, 4), (2048, 16))
DIL_GROUP_WIDTH = 256
DIL_WIDTH = 768
DIL_RADIUS = 64
FFN_HIDDEN = 2816
ROPE_THETA = 10000.0
NORM_EPS = 1e-6
SUBLN_EPS = 1e-5
RWKV_GN_EPS = 64e-5
NEG_INF = -1e30
CHUNK = 64
VMEM_LIMIT = 56 * 1024 * 1024

NT_DIMS = (((1,), (1,)), ((), ()))
TN_DIMS = (((0,), (0,)), ((), ()))


def _params(*sem):
    return pltpu.CompilerParams(dimension_semantics=sem, vmem_limit_bytes=VMEM_LIMIT)


def _sigmoid(x):
    return 1.0 / (1.0 + jnp.exp(-x))


def _rms(x, gain, eps):
    return x * lax.rsqrt(jnp.mean(x * x, axis=-1, keepdims=True) + eps) * gain


def _rope_tile(x, cos, sin, upper):
    rot = jnp.where(upper, pltpu.roll(x, 32, 1), pltpu.roll(x, 96, 1))
    return x * cos + rot * sin


def _proj_kernel(x_ref, g_ref, w_ref, cos_ref, sin_ref, *out_refs, splits, n_rope, q_cols, emit_xn):
    x = x_ref[...]
    xn = _rms(x, g_ref[...], NORM_EPS)
    if emit_xn:
        out_refs[-1][...] = xn
    xb = xn.astype(BF16)
    cos = cos_ref[...]
    sin = sin_ref[...]
    lane = lax.broadcasted_iota(jnp.int32, (1, LANES), 1)
    upper = (lane % HEAD_DIM) >= (HEAD_DIM // 2)
    col = 0
    for ref, width in zip(out_refs, splits):
        for c in range(width // LANES):
            y = jnp.dot(xb, w_ref[:, col:col + LANES], preferred_element_type=F32)
            if col < n_rope:
                y = _rope_tile(y, cos, sin, upper)
            if col < q_cols:
                y = y * (HEAD_DIM ** -0.5)
            ref[:, c * LANES:(c + 1) * LANES] = y.astype(ref.dtype)
            col += LANES


def _norm_proj(x2d, gain, w_bf16, cos, sin, seq, splits, dtypes, n_rope, q_cols, emit_xn, tm=512):
    m = x2d.shape[0]
    n = w_bf16.shape[1]
    nseq = seq // tm
    out_shape = [jax.ShapeDtypeStruct((m, w), dt) for w, dt in zip(splits, dtypes)]
    out_specs = [pl.BlockSpec((tm, w), lambda i: (i, 0)) for w in splits]
    if emit_xn:
        out_shape.append(jax.ShapeDtypeStruct((m, D_MODEL), F32))
        out_specs.append(pl.BlockSpec((tm, D_MODEL), lambda i: (i, 0)))
    kern = functools.partial(_proj_kernel, splits=splits, n_rope=n_rope, q_cols=q_cols, emit_xn=emit_xn)
    return pl.pallas_call(
        kern,
        out_shape=out_shape,
        grid=(m // tm,),
        in_specs=[
            pl.BlockSpec((tm, D_MODEL), lambda i: (i, 0)),
            pl.BlockSpec((1, D_MODEL), lambda i: (0, 0)),
            pl.BlockSpec((D_MODEL, n), lambda i: (0, 0)),
            pl.BlockSpec((tm, LANES), lambda i: (i % nseq, 0)),
            pl.BlockSpec((tm, LANES), lambda i: (i % nseq, 0)),
        ],
        out_specs=out_specs,
        compiler_params=_params("parallel"),
        name="norm_proj",
    )(x2d, gain.reshape(1, -1), w_bf16, cos, sin)


def _rope_tables(seq):
    half = HEAD_DIM // 2
    inv = ROPE_THETA ** (-jnp.arange(half, dtype=F32) / half)
    ang = jnp.arange(seq, dtype=F32)[:, None] * inv[None, :]
    cos = jnp.cos(ang)
    sin = jnp.sin(ang)
    cos_t = jnp.tile(jnp.concatenate([cos, cos], axis=-1), (1, LANES // HEAD_DIM))
    sin_t = jnp.tile(jnp.concatenate([-sin, sin], axis=-1), (1, LANES // HEAD_DIM))
    return cos_t, sin_t


def _diff_attn_kernel(lamq_ref, lamk_ref, subln_ref, q_ref, k_ref, v_ref, o_ref,
                      m_ref, l_ref, acc_ref, *, seq, tk, lam_init):
    q = q_ref[...]
    tq = q.shape[0]
    lane = lax.broadcasted_iota(jnp.int32, (1, LANES), 1)
    zero = jnp.zeros_like(q)
    qs = (jnp.where(lane < HEAD_DIM, q, zero), jnp.where(lane >= HEAD_DIM, q, zero))
    m_ref[...] = jnp.full(m_ref.shape, -jnp.inf, F32)
    l_ref[...] = jnp.zeros(l_ref.shape, F32)
    acc_ref[...] = jnp.zeros(acc_ref.shape, F32)

    def body(j, carry):
        off = pl.multiple_of(j * tk, tk)
        kj = k_ref[pl.ds(off, tk), :]
        vj = v_ref[pl.ds(off, tk), :]
        for c in range(2):
            s = lax.dot_general(qs[c], kj, NT_DIMS, preferred_element_type=F32)
            m_old = m_ref[c]
            m_new = jnp.maximum(m_old, jnp.max(s, axis=-1, keepdims=True))
            alpha = jnp.exp(m_old - m_new)
            p = jnp.exp(s - m_new)
            l_ref[c] = alpha * l_ref[c] + jnp.sum(p, axis=-1, keepdims=True)
            acc_ref[c] = alpha * acc_ref[c] + jnp.dot(p.astype(BF16), vj, preferred_element_type=F32)
            m_ref[c] = m_new
        return carry

    lax.fori_loop(0, seq // tk, body, 0)

    e = jnp.exp(jnp.sum(lamq_ref[...] * lamk_ref[...], axis=-1, keepdims=True))
    lam = e[0:1] - e[1:2] + lam_init
    o = acc_ref[0] / l_ref[0] - lam * (acc_ref[1] / l_ref[1])
    o_ref[...] = _rms(o, subln_ref[...], SUBLN_EPS) * (1.0 - lam_init)
    del tq


def _diff_attention(q, k, v, lamq, lamk, subln_w, lam_init, tq=512, tk=512):
    b, s, _ = q.shape
    heads = DIFF_WIDTH // LANES
    kern = functools.partial(_diff_attn_kernel, seq=s, tk=tk, lam_init=lam_init)
    return pl.pallas_call(
        kern,
        out_shape=jax.ShapeDtypeStruct((b, s, DIFF_WIDTH), F32),
        grid=(b, heads, s // tq),
        in_specs=[
            pl.BlockSpec((2, HEAD_DIM), lambda bi, h, i: (0, 0)),
            pl.BlockSpec((2, HEAD_DIM), lambda bi, h, i: (0, 0)),
            pl.BlockSpec((1, LANES), lambda bi, h, i: (0, 0)),
            pl.BlockSpec((None, tq, LANES), lambda bi, h, i: (bi, i, h)),
            pl.BlockSpec((None, s, LANES), lambda bi, h, i: (bi, 0, h)),
            pl.BlockSpec((None, s, LANES), lambda bi, h, i: (bi, 0, h)),
        ],
        out_specs=pl.BlockSpec((None, tq, LANES), lambda bi, h, i: (bi, i, h)),
        scratch_shapes=[
            pltpu.VMEM((2, tq, 1), F32),
            pltpu.VMEM((2, tq, 1), F32),
            pltpu.VMEM((2, tq, LANES), F32),
        ],
        compiler_params=_params("parallel", "parallel", "parallel"),
        name="diff_attn",
    )(lamq, lamk, subln_w.reshape(1, -1), q, k, v)


def _cshift(x, prev_row, next_row):
    t = x.shape[0]
    row = lax.broadcasted_iota(jnp.int32, (t, 1), 0)
    p = jnp.where(row == 0, prev_row, pltpu.roll(x, 1, 0))
    n = jnp.where(row == t - 1, next_row, pltpu.roll(x, t - 1, 0))
    return 0.5 * (p + n)


def _head_sum(x, ones_bd):
    parts = [jnp.dot(x[:, p * LANES:(p + 1) * LANES], ones_bd, precision=HIGHEST,
                     preferred_element_type=F32) for p in range(x.shape[1] // LANES)]
    return jnp.concatenate(parts, axis=1)


def _rwkv_prep_kernel(xn_ref, xnp_ref, xnn_ref, t_ref, tp_ref, tn_ref,
                      mux_ref, mut_ref, w1_ref, w2_ref, a1_ref, a2_ref, g1_ref, g2_ref,
                      w0_ref, a0_ref, kk_ref, ka_ref, rk_ref, bd_ref,
                      r_out, v_out, kk_out, lwf_out, lwb_out, kf_out, kb_out, bf_out, bb_out,
                      bonus_out, g_out):
    i = pl.program_id(1)
    first = jnp.where(i > 0, 1.0, 0.0).astype(F32)
    last = jnp.where(i < pl.num_programs(1) - 1, 1.0, 0.0).astype(F32)
    xn = xn_ref[...]
    xx = _cshift(xn, xnp_ref[7:8, :] * first, xnn_ref[0:1, :] * last) - xn
    mux = mux_ref[...]
    xw = (xn + xx * mux[0:1]).astype(BF16)
    xa = (xn + xx * mux[1:2]).astype(BF16)
    xg = (xn + xx * mux[2:3]).astype(BF16)

    t = t_ref[...]
    ts = t + (_cshift(t, tp_ref[7:8, :] * first, tn_ref[0:1, :] * last) - t) * mut_ref[...]
    r = ts[:, 0:RWKV_WIDTH]
    k = ts[:, RWKV_WIDTH:2 * RWKV_WIDTH]
    v = ts[:, 2 * RWKV_WIDTH:3 * RWKV_WIDTH]

    hw = jnp.tanh(jnp.dot(xw, w1_ref[...], preferred_element_type=F32))
    dec = jnp.dot(hw.astype(BF16), w2_ref[...], preferred_element_type=F32) + w0_ref[...]
    ha = jnp.dot(xa, a1_ref[...], preferred_element_type=F32)
    rate = _sigmoid(jnp.dot(ha.astype(BF16), a2_ref[...], preferred_element_type=F32) + a0_ref[...])
    hg = _sigmoid(jnp.dot(xg, g1_ref[...], preferred_element_type=F32))
    g_out[...] = jnp.dot(hg.astype(BF16), g2_ref[...], preferred_element_type=F32)

    lw = -math.exp(-0.5) * _sigmoid(dec)
    lwf_out[...] = lw[:, 0:RWKV_WIDTH]
    lwb_out[...] = lw[:, RWKV_WIDTH:]

    bd = bd_ref[...]
    kk = k * kk_ref[...]
    kk = kk / jnp.maximum(jnp.sqrt(_head_sum(kk * kk, bd)), 1e-12)
    a_f = rate[:, 0:RWKV_WIDTH]
    a_b = rate[:, RWKV_WIDTH:]
    ka = ka_ref[...]
    k_f = k * (1.0 + (a_f - 1.0) * ka)
    k_b = k * (1.0 + (a_b - 1.0) * ka)
    r_out[...] = r
    v_out[...] = v
    kk_out[...] = kk
    kf_out[...] = k_f
    kb_out[...] = k_b
    bf_out[...] = kk * a_f
    bb_out[...] = kk * a_b
    bonus_out[...] = _head_sum(r * (0.5 * (k_f + k_b)) * rk_ref[...], bd) * v


def _halo_specs(ts, width, seq):
    nb8 = seq // 8
    r8 = ts // 8
    return [
        pl.BlockSpec((None, ts, width), lambda b, i: (b, i, 0)),
        pl.BlockSpec((None, 8, width), lambda b, i: (b, jnp.maximum(i * r8 - 1, 0), 0)),
        pl.BlockSpec((None, 8, width), lambda b, i: (b, jnp.minimum((i + 1) * r8, nb8 - 1), 0)),
    ]


def _rwkv_prep(xn, rkv, wts, ts=256):
    b, s, _ = xn.shape
    full = lambda a: pl.BlockSpec(a.shape, lambda bi, i: (0,) * a.ndim)
    in_specs = (_halo_specs(ts, D_MODEL, s) + _halo_specs(ts, 3 * RWKV_WIDTH, s)
                + [full(a) for a in wts])
    out_spec = pl.BlockSpec((None, ts, RWKV_WIDTH), lambda bi, i: (bi, i, 0))
    return pl.pallas_call(
        _rwkv_prep_kernel,
        out_shape=[jax.ShapeDtypeStruct((b, s, RWKV_WIDTH), F32)] * 11,
        grid=(b, s // ts),
        in_specs=in_specs,
        out_specs=[out_spec] * 11,
        compiler_params=_params("parallel", "parallel"),
        name="rwkv_prep",
    )(xn, xn, xn, rkv, rkv, rkv, *wts)


def _mm(a, b):
    return jnp.dot(a, b, precision=HIGHEST, preferred_element_type=F32)


def _stack(x, m0):
    zero = jnp.zeros_like(x)
    return jnp.concatenate([jnp.where(m0, x, zero), jnp.where(m0, zero, x)], axis=0)


def _chunk_pair(r, a, b, k, v, p_inc, p_inv, p_exc, e_hat, p_tot, t_in, strict, incl, eye, m0):
    a_st = _stack(-a * p_exc, m0)
    r_st = _stack(r * p_inc, m0)
    b_st = _stack(b * p_inv, m0)
    k_st = _stack(k * p_inv, m0)
    v_st = _stack(v, m0)
    bh_st = _stack(b * e_hat, m0)
    kh_st = _stack(k * e_hat, m0)
    lhs = jnp.concatenate([a_st, r_st], axis=0)
    rhs = jnp.concatenate([b_st, k_st], axis=0)
    g = lax.dot_general(lhs, rhs, NT_DIMS, precision=HIGHEST, preferred_element_type=F32)
    n2 = 2 * CHUNK
    zero = jnp.zeros((n2, n2), F32)
    n_ab = jnp.where(strict, g[:n2, :n2], zero)
    a_ak = jnp.where(strict, g[:n2, n2:], zero)
    a_rb = jnp.where(incl, g[n2:, :n2], zero)
    a_rk = jnp.where(incl, g[n2:, n2:], zero)
    minv = jnp.where(eye, 1.0, 0.0).astype(F32) + n_ab
    npow = n_ab
    for _ in range(int(math.log2(CHUNK)) - 1):
        npow = _mm(npow, npow)
        minv = minv + _mm(minv, npow)
    akv = _mm(a_ak, v_st)
    x = _mm(minv, jnp.concatenate([a_st, akv], axis=1))
    w1 = x[:, :LANES]
    u_loc = x[:, LANES:]
    y_loc = _mm(a_rb, u_loc) + _mm(a_rk, v_st)
    rw = r_st + _mm(a_rb, w1)
    phi = jnp.where(eye, p_tot, 0.0) + lax.dot_general(bh_st, w1, TN_DIMS, precision=HIGHEST,
                                                       preferred_element_type=F32)
    dm = (lax.dot_general(bh_st, u_loc, TN_DIMS, precision=HIGHEST, preferred_element_type=F32)
          + lax.dot_general(kh_st, v_st, TN_DIMS, precision=HIGHEST, preferred_element_type=F32))
    y_st = _mm(rw, t_in) + y_loc
    t_out = _mm(phi, t_in) + dm
    return y_st[:CHUNK] + y_st[CHUNK:], t_out


def _rwkv_scan_kernel(rf_ref, vf_ref, af_ref, lwf_ref, kf_ref, bf_ref,
                      rb_ref, vb_ref, ab_ref, lwb_ref, kb_ref, bb_ref,
                      yf_ref, yb_ref, state_ref):
    @pl.when(pl.program_id(1) == 0)
    def _():
        state_ref[...] = jnp.zeros(state_ref.shape, F32)

    n2 = 2 * CHUNK
    ri = lax.broadcasted_iota(jnp.int32, (n2, n2), 0)
    ci = lax.broadcasted_iota(jnp.int32, (n2, n2), 1)
    same = (ri // CHUNK) == (ci // CHUNK)
    eye = ri == ci
    ti = lax.broadcasted_iota(jnp.int32, (CHUNK, CHUNK), 0)
    si = lax.broadcasted_iota(jnp.int32, (CHUNK, CHUNK), 1)
    m0 = lax.broadcasted_iota(jnp.int32, (1, LANES), 1) < HEAD_DIM

    dirs = (
        (0, rf_ref, vf_ref, af_ref, lwf_ref, kf_ref, bf_ref, yf_ref),
        (1, rb_ref, vb_ref, ab_ref, lwb_ref, kb_ref, bb_ref, yb_ref),
    )
    for d, r_ref, v_ref, a_ref, lw_ref, k_ref, b_ref, y_ref in dirs:
        if d == 0:
            strict = same & (ci < ri)
            incl = same & (ci <= ri)
            tri = jnp.where(si <= ti, 1.0, 0.0).astype(F32)
            tot_row = CHUNK - 1
        else:
            strict = same & (ci > ri)
            incl = same & (ci >= ri)
            tri = jnp.where(si >= ti, 1.0, 0.0).astype(F32)
            tot_row = 0
        lw = lw_ref[...]
        cum = _mm(tri, lw)
        tot = cum[tot_row:tot_row + 1, :]
        p_inc = jnp.exp(cum)
        p_inv = jnp.exp(-cum)
        p_exc = jnp.exp(cum - lw)
        e_hat = jnp.exp(tot - cum)
        p_tot = jnp.exp(tot)
        r = r_ref[...]
        v = v_ref[...]
        a = a_ref[...]
        k = k_ref[...]
        b = b_ref[...]
        for p in range(N_PAIRS):
            sl = slice(p * LANES, (p + 1) * LANES)
            y, t_out = _chunk_pair(r[:, sl], a[:, sl], b[:, sl], k[:, sl], v[:, sl],
                                   p_inc[:, sl], p_inv[:, sl], p_exc[:, sl], e_hat[:, sl],
                                   p_tot[:, sl], state_ref[d, p], strict, incl, eye, m0)
            y_ref[:, sl] = y
            state_ref[d, p] = t_out


def _rwkv_scan(r, v, kk, lwf, kf, bf, lwb, kb, bb):
    b, s, _ = r.shape
    nc = s // CHUNK
    fwd = pl.BlockSpec((None, CHUNK, RWKV_WIDTH), lambda bi, c: (bi, c, 0))
    bwd = pl.BlockSpec((None, CHUNK, RWKV_WIDTH), lambda bi, c: (bi, nc - 1 - c, 0))
    return pl.pallas_call(
        _rwkv_scan_kernel,
        out_shape=[jax.ShapeDtypeStruct((b, s, RWKV_WIDTH), F32)] * 2,
        grid=(b, nc),
        in_specs=[fwd] * 6 + [bwd] * 6,
        out_specs=[fwd, bwd],
        scratch_shapes=[pltpu.VMEM((2, N_PAIRS, LANES, LANES), F32)],
        compiler_params=_params("parallel", "arbitrary"),
        name="rwkv_scan",
    )(r, v, kk, lwf, kf, bf, r, v, kk, lwb, kb, bb)


def _mix0_out_kernel(x_ref, oa_ref, yf_ref, yb_ref, bonus_ref, g_ref, lnw_ref, lnb_ref, bd_ref,
                     w_ref, gain_ref, o_ref):
    y = yf_ref[...] + yb_ref[...]
    bd = bd_ref[...]
    mean = _head_sum(y, bd) * (1.0 / HEAD_DIM)
    yc = y - mean
    var = _head_sum(yc * yc, bd) * (1.0 / HEAD_DIM)
    yn = yc * lax.rsqrt(var + RWKV_GN_EPS) * lnw_ref[...] + lnb_ref[...]
    ob = (yn + bonus_ref[...]) * g_ref[...]
    m = (jnp.dot(oa_ref[...].astype(BF16), w_ref[0:DIFF_WIDTH, :], preferred_element_type=F32)
         + jnp.dot(ob.astype(BF16), w_ref[DIFF_WIDTH:, :], preferred_element_type=F32))
    o_ref[...] = x_ref[...] + _rms(m, gain_ref[...], NORM_EPS)


def _mix0_out(x2d, oa, yf, yb, bonus, g, lnw, lnb, bd, w_bf16, gain, tm=256):
    m = x2d.shape[0]
    row = lambda w: pl.BlockSpec((tm, w), lambda i: (i, 0))
    full = lambda a: pl.BlockSpec(a.shape, lambda i: (0,) * a.ndim)
    small = (lnw.reshape(1, -1), lnb.reshape(1, -1), bd, w_bf16, gain.reshape(1, -1))
    return pl.pallas_call(
        _mix0_out_kernel,
        out_shape=jax.ShapeDtypeStruct((m, D_MODEL), F32),
        grid=(m // tm,),
        in_specs=[row(D_MODEL)] + [row(RWKV_WIDTH)] * 5 + [full(a) for a in small],
        out_specs=row(D_MODEL),
        compiler_params=_params("parallel"),
        name="mix0_out",
    )(x2d, oa, yf, yb, bonus, g, *small)


def _ffn_kernel(x_ref, pre_ref, post_ref, wg_ref, wu_ref, wd_ref, o_ref, xn_ref, acc_ref):
    j = pl.program_id(1)

    @pl.when(j == 0)
    def _():
        xn_ref[...] = _rms(x_ref[...], pre_ref[...], NORM_EPS).astype(BF16)
        acc_ref[...] = jnp.zeros(acc_ref.shape, F32)

    xn = xn_ref[...]
    gate = jnp.dot(xn, wg_ref[...], preferred_element_type=F32)
    up = jnp.dot(xn, wu_ref[...], preferred_element_type=F32)
    h = (gate * _sigmoid(gate) * up).astype(BF16)
    acc_ref[...] += jnp.dot(h, wd_ref[...], preferred_element_type=F32)

    @pl.when(j == pl.num_programs(1) - 1)
    def _():
        o_ref[...] = x_ref[...] + _rms(acc_ref[...], post_ref[...], NORM_EPS)


def _ffn(x2d, pre, post, wg, wu, wd, tm=512, th=1408):
    m = x2d.shape[0]
    return pl.pallas_call(
        _ffn_kernel,
        out_shape=jax.ShapeDtypeStruct((m, D_MODEL), F32),
        grid=(m // tm, FFN_HIDDEN // th),
        in_specs=[
            pl.BlockSpec((tm, D_MODEL), lambda i, j: (i, 0)),
            pl.BlockSpec((1, D_MODEL), lambda i, j: (0, 0)),
            pl.BlockSpec((1, D_MODEL), lambda i, j: (0, 0)),
            pl.BlockSpec((D_MODEL, th), lambda i, j: (0, j)),
            pl.BlockSpec((D_MODEL, th), lambda i, j: (0, j)),
            pl.BlockSpec((th, D_MODEL), lambda i, j: (j, 0)),
        ],
        out_specs=pl.BlockSpec((tm, D_MODEL), lambda i, j: (i, 0)),
        scratch_shapes=[pltpu.VMEM((tm, D_MODEL), BF16), pltpu.VMEM((tm, D_MODEL), F32)],
        compiler_params=_params("parallel", "arbitrary"),
        name="ffn",
    )(x2d, pre.reshape(1, -1), post.reshape(1, -1), wg, wu, wd)


def _band_attn_kernel(q_ref, kp_ref, kc_ref, kn_ref, vp_ref, vc_ref, vn_ref, o_ref, lse_ref, *, length):
    tq = q_ref.shape[0]
    halo = DIL_RADIUS
    l0 = pl.program_id(2) * tq
    q = q_ref[...]
    kw = jnp.concatenate([kp_ref[...], kc_ref[...], kn_ref[...]], axis=0)
    vw = jnp.concatenate([vp_ref[...], vc_ref[...], vn_ref[...]], axis=0)
    wlen = tq + 2 * halo
    qpos = l0 + lax.broadcasted_iota(jnp.int32, (tq, wlen), 0)
    kpos = l0 - halo + lax.broadcasted_iota(jnp.int32, (tq, wlen), 1)
    valid = (jnp.abs(kpos - qpos) <= halo) & (kpos >= 0) & (kpos < length)
    lane = lax.broadcasted_iota(jnp.int32, (1, DIL_GROUP_WIDTH), 1)
    zero = jnp.zeros_like(q)
    o = jnp.zeros((tq, DIL_GROUP_WIDTH), F32)
    lse_full = jnp.zeros((tq, DIL_GROUP_WIDTH), F32)
    for h in range(DIL_GROUP_WIDTH // HEAD_DIM):
        hm = (lane // HEAD_DIM) == h
        s = lax.dot_general(jnp.where(hm, q, zero), kw, NT_DIMS, preferred_element_type=F32)
        s = jnp.where(valid, s, NEG_INF)
        mx = jnp.max(s, axis=-1, keepdims=True)
        p = jnp.exp(s - mx)
        den = jnp.sum(p, axis=-1, keepdims=True)
        oh = jnp.dot(p.astype(BF16), vw, preferred_element_type=F32) / den
        o = jnp.where(hm, oh, o)
        lse_full = jnp.where(hm, mx + jnp.log(den), lse_full)
    o_ref[...] = o
    lse_ref[...] = lse_full


def _band_attention(q, k, v, group, dilation, tq=128):
    b, s, _ = q.shape
    length = s // dilation
    nblk = length // tq
    hb = tq // DIL_RADIUS
    nh = length // DIL_RADIUS
    cpr = DIL_WIDTH // DIL_GROUP_WIDTH
    fold = lambda a: a.reshape(b, length, dilation * DIL_WIDTH)
    cur = lambda bi, r, i: (bi, i, r * cpr + group)
    prev = lambda bi, r, i: (bi, jnp.maximum(i * hb - 1, 0), r * cpr + group)
    nxt = lambda bi, r, i: (bi, jnp.minimum((i + 1) * hb, nh - 1), r * cpr + group)
    main = pl.BlockSpec((None, tq, DIL_GROUP_WIDTH), cur)
    hp = pl.BlockSpec((None, DIL_RADIUS, DIL_GROUP_WIDTH), prev)
    hn = pl.BlockSpec((None, DIL_RADIUS, DIL_GROUP_WIDTH), nxt)
    out_spec = pl.BlockSpec((None, tq, DIL_GROUP_WIDTH), lambda bi, r, i: (bi, i, r))
    out_sds = jax.ShapeDtypeStruct((b, length, dilation * DIL_GROUP_WIDTH), F32)
    o, lse = pl.pallas_call(
        functools.partial(_band_attn_kernel, length=length),
        out_shape=[out_sds, out_sds],
        grid=(b, dilation, nblk),
        in_specs=[main, hp, main, hn, hp, main, hn],
        out_specs=[out_spec, out_spec],
        compiler_params=_params("parallel", "parallel", "parallel"),
        name="band_attn",
    )(fold(q), fold(k), fold(k), fold(k), fold(v), fold(v), fold(v))
    return o.reshape(b * s, DIL_GROUP_WIDTH), lse.reshape(b * s, DIL_GROUP_WIDTH)


def _mix1_out_kernel(x_ref, o0_ref, o1_ref, o2_ref, l0_ref, l1_ref, l2_ref, w_ref, gain_ref, out_ref):
    ls = (l0_ref[...], l1_ref[...], l2_ref[...])
    os_ = (o0_ref[...], o1_ref[...], o2_ref[...])
    mx = jnp.maximum(jnp.maximum(ls[0], ls[1]), ls[2])
    es = [jnp.exp(l - mx) for l in ls]
    den = es[0] + es[1] + es[2]
    m = jnp.zeros((x_ref.shape[0], D_MODEL), F32)
    for gi in range(3):
        y = (os_[gi] * (es[gi] / den)).astype(BF16)
        m = m + jnp.dot(y, w_ref[gi * DIL_GROUP_WIDTH:(gi + 1) * DIL_GROUP_WIDTH, :],
                        preferred_element_type=F32)
    out_ref[...] = x_ref[...] + _rms(m, gain_ref[...], NORM_EPS)


def _mix1_out(x2d, outs, lses, w_bf16, gain, tm=256):
    m = x2d.shape[0]
    row = lambda w: pl.BlockSpec((tm, w), lambda i: (i, 0))
    return pl.pallas_call(
        _mix1_out_kernel,
        out_shape=jax.ShapeDtypeStruct((m, D_MODEL), F32),
        grid=(m // tm,),
        in_specs=[row(D_MODEL)] + [row(DIL_GROUP_WIDTH)] * 6
        + [pl.BlockSpec(w_bf16.shape, lambda i: (0, 0)), pl.BlockSpec((1, D_MODEL), lambda i: (0, 0))],
        out_specs=row(D_MODEL),
        compiler_params=_params("parallel"),
        name="mix1_out",
    )(x2d, *outs, *lses, w_bf16, gain.reshape(1, -1))


def _block_diag2(top, bottom):
    z_tr = jnp.zeros((top.shape[0], bottom.shape[1]), top.dtype)
    z_bl = jnp.zeros((bottom.shape[0], top.shape[1]), top.dtype)
    return jnp.concatenate([jnp.concatenate([top, z_tr], axis=1),
                            jnp.concatenate([z_bl, bottom], axis=1)], axis=0)


def kernel(x_prompt, x_sample, mix_pre0, mix_post0, w_in0, lam_q1, lam_k1, lam_q2, lam_k2, subln_w,
           mu_r, mu_k, mu_v, mu_w, mu_a, mu_g, w0_f, w1_f, w2_f, w0_b, w1_b, w2_b,
           a0_f, a1_f, a2_f, a0_b, a1_b, a2_b, g1, g2, k_k, k_a, r_k, lnx_w, lnx_b, w_out0,
           ffn_pre0, ffn_post0, ffn_gate0, ffn_up0, ffn_down0,
           mix_pre1, mix_post1, w_in1, w_out1, ffn_pre1, ffn_post1, ffn_gate1, ffn_up1, ffn_down1):
    bf = lambda a: a.astype(BF16)
    row = lambda a: a.reshape(1, -1).astype(F32)
    lam_init = 0.8 - 0.6 * math.exp(-0.3 * 0)
    lamq = jnp.stack([lam_q1, lam_q2]).astype(F32)
    lamk = jnp.stack([lam_k1, lam_k2]).astype(F32)
    gate_pad = 2 * LANES - g1.shape[1]
    ones_bd = _block_diag2(jnp.ones((HEAD_DIM, HEAD_DIM), F32), jnp.ones((HEAD_DIM, HEAD_DIM), F32))
    prep_w = (
        jnp.stack([mu_w, mu_a, mu_g]).astype(F32),
        row(jnp.concatenate([mu_r, mu_k, mu_v])),
        bf(jnp.concatenate([w1_f, w1_b], axis=1)),
        bf(_block_diag2(w2_f, w2_b)),
        bf(jnp.concatenate([a1_f, a1_b], axis=1)),
        bf(_block_diag2(a2_f, a2_b)),
        bf(jnp.pad(g1, ((0, 0), (0, gate_pad)))),
        bf(jnp.pad(g2, ((0, gate_pad), (0, 0)))),
        row(jnp.concatenate([w0_f, w0_b])),
        row(jnp.concatenate([a0_f, a0_b])),
        row(k_k), row(k_a), row(r_k.reshape(-1)),
        ones_bd,
    )
    w_in0_b, w_out0_b, w_in1_b, w_out1_b = bf(w_in0), bf(w_out0), bf(w_in1), bf(w_out1)
    ffn0 = (ffn_pre0, ffn_post0, bf(ffn_gate0), bf(ffn_up0), bf(ffn_down0))
    ffn1 = (ffn_pre1, ffn_post1, bf(ffn_gate1), bf(ffn_up1), bf(ffn_down1))

    def run(x):
        b, s, _ = x.shape
        x2d = x.reshape(b * s, D_MODEL)
        cos, sin = _rope_tables(s)
        q, k, v, rkv, xn = _norm_proj(
            x2d, mix_pre0, w_in0_b, cos, sin, s,
            splits=(DIFF_WIDTH, DIFF_WIDTH, DIFF_WIDTH, 3 * RWKV_WIDTH), dtypes=(BF16, BF16, BF16, F32),
            n_rope=2 * DIFF_WIDTH, q_cols=DIFF_WIDTH, emit_xn=True)
        sh = lambda a: a.reshape(b, s, -1)
        out_a = _diff_attention(sh(q), sh(k), sh(v), lamq, lamk, subln_w, lam_init)
        r, vv, kk, lwf, lwb, kf, kb, bfw, bbw, bonus, gate = _rwkv_prep(sh(xn), sh(rkv), prep_w)
        yf, yb = _rwkv_scan(r, vv, kk, lwf, kf, bfw, lwb, kb, bbw)
        fl = lambda a: a.reshape(b * s, -1)
        x1 = _mix0_out(x2d, fl(out_a), fl(yf), fl(yb), fl(bonus), fl(gate), lnx_w, lnx_b, ones_bd,
                       w_out0_b, mix_post0)
        x2 = _ffn(x1, *ffn0)
        q1, k1, v1 = _norm_proj(
            x2, mix_pre1, w_in1_b, cos, sin, s,
            splits=(DIL_WIDTH, DIL_WIDTH, DIL_WIDTH), dtypes=(BF16, BF16, BF16),
            n_rope=2 * DIL_WIDTH, q_cols=DIL_WIDTH, emit_xn=False)
        outs, lses = [], []
        for gi, (_, dilation) in enumerate(DIL_PAIRS):
            o, lse = _band_attention(sh(q1), sh(k1), sh(v1), gi, dilation)
            outs.append(o)
            lses.append(lse)
        x3 = _mix1_out(x2, outs, lses, w_out1_b, mix_post1)
        x4 = _ffn(x3, *ffn1)
        return x4.reshape(b, s, D_MODEL)

    return (run(x_prompt), run(x_sample))
```

```python
import functools
import math

import jax
import jax.numpy as jnp
from jax import lax
from jax.experimental import pallas as pl
from jax.experimental.pallas import tpu as pltpu

F32 = jnp.float32
BF16 = jnp.bfloat16
HIGHEST = lax.Precision.HIGHEST

D_MODEL = 1024
HEAD_DIM = 64
LANES = 128
DIFF_WIDTH = 512
RWKV_WIDTH = 512
N_PAIRS = RWKV_WIDTH // LANES
DIL_PAIRS = ((128, 1), (512, 4), (2048, 16))
DIL_GROUP_WIDTH = 256
DIL_WIDTH = 768
DIL_RADIUS = 64
FFN_HIDDEN = 2816
ROPE_THETA = 10000.0
NORM_EPS = 1e-6
SUBLN_EPS = 1e-5
RWKV_GN_EPS = 64e-5
NEG_INF = -1e30
LOG2E = math.log2(math.e)
CHUNK = 64
VMEM_LIMIT = 56 * 1024 * 1024

NT_DIMS = (((1,), (1,)), ((), ()))
TN_DIMS = (((0,), (0,)), ((), ()))


def _params(*sem):
    return pltpu.CompilerParams(dimension_semantics=sem, vmem_limit_bytes=VMEM_LIMIT)


def _sigmoid(x):
    return 1.0 / (1.0 + jnp.exp(-x))


def _rms(x, gain, eps):
    return x * lax.rsqrt(jnp.mean(x * x, axis=-1, keepdims=True) + eps) * gain


def _rope_tile(x, cos, sin, upper):
    rot = jnp.where(upper, pltpu.roll(x, 32, 1), pltpu.roll(x, 96, 1))
    return x * cos + rot * sin


def _proj_kernel(x_ref, g_ref, w_ref, cos_ref, sin_ref, *out_refs, splits, n_rope, q_cols, q_scale, emit_xn):
    x = x_ref[...]
    xn = _rms(x, g_ref[...], NORM_EPS)
    if emit_xn:
        out_refs[-1][...] = xn
    xb = xn.astype(BF16)
    cos = cos_ref[...]
    sin = sin_ref[...]
    lane = lax.broadcasted_iota(jnp.int32, (1, LANES), 1)
    upper = (lane % HEAD_DIM) >= (HEAD_DIM // 2)
    col = 0
    for ref, width in zip(out_refs, splits):
        for c in range(width // LANES):
            y = jnp.dot(xb, w_ref[:, col:col + LANES], preferred_element_type=F32)
            if col < n_rope:
                y = _rope_tile(y, cos, sin, upper)
            if col < q_cols:
                y = y * q_scale
            ref[:, c * LANES:(c + 1) * LANES] = y.astype(ref.dtype)
            col += LANES


def _norm_proj(x2d, gain, w_bf16, cos, sin, seq, splits, dtypes, n_rope, q_cols, q_scale, emit_xn, tm=512):
    m = x2d.shape[0]
    n = w_bf16.shape[1]
    nseq = seq // tm
    out_shape = [jax.ShapeDtypeStruct((m, w), dt) for w, dt in zip(splits, dtypes)]
    out_specs = [pl.BlockSpec((tm, w), lambda i: (i, 0)) for w in splits]
    if emit_xn:
        out_shape.append(jax.ShapeDtypeStruct((m, D_MODEL), F32))
        out_specs.append(pl.BlockSpec((tm, D_MODEL), lambda i: (i, 0)))
    kern = functools.partial(_proj_kernel, splits=splits, n_rope=n_rope, q_cols=q_cols, q_scale=q_scale,
                             emit_xn=emit_xn)
    return pl.pallas_call(
        kern,
        out_shape=out_shape,
        grid=(m // tm,),
        in_specs=[
            pl.BlockSpec((tm, D_MODEL), lambda i: (i, 0)),
            pl.BlockSpec((1, D_MODEL), lambda i: (0, 0)),
            pl.BlockSpec((D_MODEL, n), lambda i: (0, 0)),
            pl.BlockSpec((tm, LANES), lambda i: (i % nseq, 0)),
            pl.BlockSpec((tm, LANES), lambda i: (i % nseq, 0)),
        ],
        out_specs=out_specs,
        compiler_params=_params("parallel"),
        name="norm_proj",
    )(x2d, gain.reshape(1, -1), w_bf16, cos, sin)


def _rope_tables(seq):
    half = HEAD_DIM // 2
    inv = ROPE_THETA ** (-jnp.arange(half, dtype=F32) / half)
    ang = jnp.arange(seq, dtype=F32)[:, None] * inv[None, :]
    cos = jnp.cos(ang)
    sin = jnp.sin(ang)
    cos_t = jnp.tile(jnp.concatenate([cos, cos], axis=-1), (1, LANES // HEAD_DIM))
    sin_t = jnp.tile(jnp.concatenate([-sin, sin], axis=-1), (1, LANES // HEAD_DIM))
    return cos_t, sin_t


def _diff_attn_kernel(lamq_ref, lamk_ref, subln_ref, q_ref, k_ref, v_ref, o_ref,
                      m_ref, l_ref, acc_ref, *, seq, tk, lam_init):
    q = q_ref[...]
    lane = lax.broadcasted_iota(jnp.int32, (1, LANES), 1)
    zero = jnp.zeros_like(q)
    qs = (jnp.where(lane < HEAD_DIM, q, zero), jnp.where(lane >= HEAD_DIM, q, zero))
    m_ref[...] = jnp.full(m_ref.shape, -jnp.inf, F32)
    l_ref[...] = jnp.zeros(l_ref.shape, F32)
    acc_ref[...] = jnp.zeros(acc_ref.shape, F32)
    nck = tk // LANES

    def body(j, carry):
        off = pl.multiple_of(j * tk, tk)
        kj = k_ref[pl.ds(off, tk), :]
        vj = v_ref[pl.ds(off, tk), :]
        for c in range(2):
            s = lax.dot_general(qs[c], kj, NT_DIMS, preferred_element_type=F32)
            cols = [s[:, i * LANES:(i + 1) * LANES] for i in range(nck)]
            mx = cols[0]
            for col in cols[1:]:
                mx = jnp.maximum(mx, col)
            m_old = m_ref[c]
            m_new = jnp.maximum(m_old, jnp.max(mx, axis=-1, keepdims=True))
            alpha = jnp.exp2(m_old - m_new)
            ps = [jnp.exp2(col - m_new) for col in cols]
            lsum = ps[0]
            for pc in ps[1:]:
                lsum = lsum + pc
            l_ref[c] = alpha * l_ref[c] + lsum
            p = jnp.concatenate([pc.astype(BF16) for pc in ps], axis=1)
            acc_ref[c] = alpha * acc_ref[c] + jnp.dot(p, vj, preferred_element_type=F32)
            m_ref[c] = m_new
        return carry

    lax.fori_loop(0, seq // tk, body, 0)

    e = jnp.exp(jnp.sum(lamq_ref[...] * lamk_ref[...], axis=-1, keepdims=True))
    lam = e[0:1] - e[1:2] + lam_init
    l0 = jnp.sum(l_ref[0], axis=-1, keepdims=True)
    l1 = jnp.sum(l_ref[1], axis=-1, keepdims=True)
    o = acc_ref[0] / l0 - lam * (acc_ref[1] / l1)
    o_ref[...] = _rms(o, subln_ref[...], SUBLN_EPS) * (1.0 - lam_init)


def _diff_attention(q, k, v, lamq, lamk, subln_w, lam_init, tq=512, tk=512):
    b, s, _ = q.shape
    heads = DIFF_WIDTH // LANES
    kern = functools.partial(_diff_attn_kernel, seq=s, tk=tk, lam_init=lam_init)
    return pl.pallas_call(
        kern,
        out_shape=jax.ShapeDtypeStruct((b, s, DIFF_WIDTH), F32),
        grid=(b, heads, s // tq),
        in_specs=[
            pl.BlockSpec((2, HEAD_DIM), lambda bi, h, i: (0, 0)),
            pl.BlockSpec((2, HEAD_DIM), lambda bi, h, i: (0, 0)),
            pl.BlockSpec((1, LANES), lambda bi, h, i: (0, 0)),
            pl.BlockSpec((None, tq, LANES), lambda bi, h, i: (bi, i, h)),
            pl.BlockSpec((None, s, LANES), lambda bi, h, i: (bi, 0, h)),
            pl.BlockSpec((None, s, LANES), lambda bi, h, i: (bi, 0, h)),
        ],
        out_specs=pl.BlockSpec((None, tq, LANES), lambda bi, h, i: (bi, i, h)),
        scratch_shapes=[
            pltpu.VMEM((2, tq, LANES), F32),
            pltpu.VMEM((2, tq, LANES), F32),
            pltpu.VMEM((2, tq, LANES), F32),
        ],
        compiler_params=_params("parallel", "parallel", "parallel"),
        name="diff_attn",
    )(lamq, lamk, subln_w.reshape(1, -1), q, k, v)


def _cshift(x, prev_row, next_row):
    t = x.shape[0]
    row = lax.broadcasted_iota(jnp.int32, (t, 1), 0)
    p = jnp.where(row == 0, prev_row, pltpu.roll(x, 1, 0))
    n = jnp.where(row == t - 1, next_row, pltpu.roll(x, t - 1, 0))
    return 0.5 * (p + n)


def _head_sum(x, ones_bd):
    parts = [jnp.dot(x[:, p * LANES:(p + 1) * LANES], ones_bd, precision=HIGHEST,
                     preferred_element_type=F32) for p in range(x.shape[1] // LANES)]
    return jnp.concatenate(parts, axis=1)


def _rwkv_prep_kernel(xn_ref, xnp_ref, xnn_ref, t_ref, tp_ref, tn_ref,
                      mux_ref, mut_ref, w1_ref, w2_ref, a1_ref, a2_ref, g1_ref, g2_ref,
                      w0_ref, a0_ref, kk_ref, ka_ref, rk_ref, bd_ref,
                      r_out, v_out, kk_out, lwf_out, lwb_out, kf_out, kb_out, bf_out, bb_out,
                      bonus_out, g_out):
    i = pl.program_id(1)
    first = jnp.where(i > 0, 1.0, 0.0).astype(F32)
    last = jnp.where(i < pl.num_programs(1) - 1, 1.0, 0.0).astype(F32)
    xn = xn_ref[...]
    xx = _cshift(xn, xnp_ref[7:8, :] * first, xnn_ref[0:1, :] * last) - xn
    mux = mux_ref[...]
    xw = (xn + xx * mux[0:1]).astype(BF16)
    xa = (xn + xx * mux[1:2]).astype(BF16)
    xg = (xn + xx * mux[2:3]).astype(BF16)

    t = t_ref[...]
    ts = t + (_cshift(t, tp_ref[7:8, :] * first, tn_ref[0:1, :] * last) - t) * mut_ref[...]
    r = ts[:, 0:RWKV_WIDTH]
    k = ts[:, RWKV_WIDTH:2 * RWKV_WIDTH]
    v = ts[:, 2 * RWKV_WIDTH:3 * RWKV_WIDTH]

    hw = jnp.tanh(jnp.dot(xw, w1_ref[...], preferred_element_type=F32))
    dec = jnp.dot(hw.astype(BF16), w2_ref[...], preferred_element_type=F32) + w0_ref[...]
    ha = jnp.dot(xa, a1_ref[...], preferred_element_type=F32)
    rate = _sigmoid(jnp.dot(ha.astype(BF16), a2_ref[...], preferred_element_type=F32) + a0_ref[...])
    hg = _sigmoid(jnp.dot(xg, g1_ref[...], preferred_element_type=F32))
    g_out[...] = jnp.dot(hg.astype(BF16), g2_ref[...], preferred_element_type=F32)

    lw = -math.exp(-0.5) * _sigmoid(dec)
    lwf_out[...] = lw[:, 0:RWKV_WIDTH]
    lwb_out[...] = lw[:, RWKV_WIDTH:]

    bd = bd_ref[...]
    kk = k * kk_ref[...]
    kk = kk / jnp.maximum(jnp.sqrt(_head_sum(kk * kk, bd)), 1e-12)
    a_f = rate[:, 0:RWKV_WIDTH]
    a_b = rate[:, RWKV_WIDTH:]
    ka = ka_ref[...]
    k_f = k * (1.0 + (a_f - 1.0) * ka)
    k_b = k * (1.0 + (a_b - 1.0) * ka)
    r_out[...] = r
    v_out[...] = v
    kk_out[...] = kk
    kf_out[...] = k_f
    kb_out[...] = k_b
    bf_out[...] = kk * a_f
    bb_out[...] = kk * a_b
    bonus_out[...] = _head_sum(r * (0.5 * (k_f + k_b)) * rk_ref[...], bd) * v


def _halo_specs(ts, width, seq):
    nb8 = seq // 8
    r8 = ts // 8
    return [
        pl.BlockSpec((None, ts, width), lambda b, i: (b, i, 0)),
        pl.BlockSpec((None, 8, width), lambda b, i: (b, jnp.maximum(i * r8 - 1, 0), 0)),
        pl.BlockSpec((None, 8, width), lambda b, i: (b, jnp.minimum((i + 1) * r8, nb8 - 1), 0)),
    ]


def _rwkv_prep(xn, rkv, wts, ts=256):
    b, s, _ = xn.shape
    full = lambda a: pl.BlockSpec(a.shape, lambda bi, i: (0,) * a.ndim)
    in_specs = (_halo_specs(ts, D_MODEL, s) + _halo_specs(ts, 3 * RWKV_WIDTH, s)
                + [full(a) for a in wts])
    out_spec = pl.BlockSpec((None, ts, RWKV_WIDTH), lambda bi, i: (bi, i, 0))
    return pl.pallas_call(
        _rwkv_prep_kernel,
        out_shape=[jax.ShapeDtypeStruct((b, s, RWKV_WIDTH), F32)] * 11,
        grid=(b, s // ts),
        in_specs=in_specs,
        out_specs=[out_spec] * 11,
        compiler_params=_params("parallel", "parallel"),
        name="rwkv_prep",
    )(xn, xn, xn, rkv, rkv, rkv, *wts)


def _mm(a, b):
    return jnp.dot(a, b, preferred_element_type=F32)


def _mm_nt(a, b):
    return lax.dot_general(a, b, NT_DIMS, preferred_element_type=F32)


def _mm_tn(a, b):
    return lax.dot_general(a, b, TN_DIMS, preferred_element_type=F32)


def _stack(x, m0):
    zero = jnp.zeros_like(x)
    return jnp.concatenate([jnp.where(m0, x, zero), jnp.where(m0, zero, x)], axis=0)


def _chunk_local(jobs, masks):
    eye, m0 = masks["eye"], masks["m0"]
    n2 = 2 * CHUNK
    st = []
    for jb in jobs:
        r_st = _stack(jb["r"] * jb["p_inc"], m0)
        st.append(dict(
            a=_stack(-jb["a"] * jb["p_exc"], m0).astype(BF16),
            r=r_st, rb=r_st.astype(BF16),
            b=_stack(jb["b"] * jb["p_inv"], m0).astype(BF16),
            k=_stack(jb["k"] * jb["p_inv"], m0).astype(BF16),
            v=_stack(jb["v"], m0).astype(BF16),
            bh=_stack(jb["b"] * jb["e_hat"], m0).astype(BF16),
            kh=_stack(jb["k"] * jb["e_hat"], m0).astype(BF16),
        ))
    gs = [_mm_nt(jnp.concatenate([s["a"], s["rb"]], axis=0), jnp.concatenate([s["b"], s["k"]], axis=0))
          for s in st]
    zero = jnp.zeros((n2, n2), F32)
    ident = jnp.where(eye, 1.0, 0.0).astype(F32)
    n_ab, a_ak, a_rb, a_rk = [], [], [], []
    for jb, g in zip(jobs, gs):
        strict, incl = masks["strict"][jb["dir"]], masks["incl"][jb["dir"]]
        n_ab.append(jnp.where(strict, g[:n2, :n2], zero))
        a_ak.append(jnp.where(strict, g[:n2, n2:], zero).astype(BF16))
        a_rb.append(jnp.where(incl, g[n2:, :n2], zero).astype(BF16))
        a_rk.append(jnp.where(incl, g[n2:, n2:], zero).astype(BF16))
    akv = [_mm(m, s["v"]) for m, s in zip(a_ak, st)]
    minv = [ident + n for n in n_ab]
    npow = [n.astype(BF16) for n in n_ab]
    for _ in range(int(math.log2(CHUNK)) - 1):
        npow = [_mm(n, n).astype(BF16) for n in npow]
        minv = [m + _mm(m.astype(BF16), n) for m, n in zip(minv, npow)]
    xs = [_mm(m.astype(BF16), jnp.concatenate([s["a"], u.astype(BF16)], axis=1))
          for m, s, u in zip(minv, st, akv)]
    out = []
    for jb, s, x, rb_, rk_ in zip(jobs, st, xs, a_rb, a_rk):
        w1 = x[:, :LANES].astype(BF16)
        u_loc = x[:, LANES:].astype(BF16)
        y_loc = _mm(rb_, u_loc) + _mm(rk_, s["v"])
        rw = s["r"] + _mm(rb_, w1)
        phi = jnp.where(eye, jb["p_tot"], 0.0) + _mm_tn(s["bh"], w1)
        dm = _mm_tn(s["bh"], u_loc) + _mm_tn(s["kh"], s["v"])
        out.append((rw.astype(BF16), y_loc, phi.astype(BF16), dm))
    return out


def _rwkv_scan_kernel(rf_ref, vf_ref, af_ref, lwf_ref, kf_ref, bf_ref,
                      rb_ref, vb_ref, ab_ref, lwb_ref, kb_ref, bb_ref,
                      yf_ref, yb_ref, state_ref, *, nsub):
    @pl.when(pl.program_id(1) == 0)
    def _():
        state_ref[...] = jnp.zeros(state_ref.shape, F32)

    n2 = 2 * CHUNK
    ri = lax.broadcasted_iota(jnp.int32, (n2, n2), 0)
    ci = lax.broadcasted_iota(jnp.int32, (n2, n2), 1)
    same = (ri // CHUNK) == (ci // CHUNK)
    ti = lax.broadcasted_iota(jnp.int32, (CHUNK, CHUNK), 0)
    si = lax.broadcasted_iota(jnp.int32, (CHUNK, CHUNK), 1)
    masks = dict(
        eye=ri == ci,
        m0=lax.broadcasted_iota(jnp.int32, (1, LANES), 1) < HEAD_DIM,
        strict=(same & (ci < ri), same & (ci > ri)),
        incl=(same & (ci <= ri), same & (ci >= ri)),
    )
    tris = (jnp.where(si <= ti, 1.0, 0.0).astype(F32), jnp.where(si >= ti, 1.0, 0.0).astype(F32))
    dirs = (
        (rf_ref, vf_ref, af_ref, lwf_ref, kf_ref, bf_ref, yf_ref),
        (rb_ref, vb_ref, ab_ref, lwb_ref, kb_ref, bb_ref, yb_ref),
    )
    jobs = []
    for d, (r_ref, v_ref, a_ref, lw_ref, k_ref, b_ref, _) in enumerate(dirs):
        tot_row = CHUNK - 1 if d == 0 else 0
        for sub in range(nsub):
            rows = slice(sub * CHUNK, (sub + 1) * CHUNK)
            lw = lw_ref[rows, :]
            cum = jnp.dot(tris[d], lw, precision=HIGHEST, preferred_element_type=F32)
            tot = cum[tot_row:tot_row + 1, :]
            rowops = dict(r=r_ref[rows, :], v=v_ref[rows, :], a=a_ref[rows, :], k=k_ref[rows, :],
                          b=b_ref[rows, :], p_inc=jnp.exp(cum), p_inv=jnp.exp(-cum),
                          p_exc=jnp.exp(cum - lw), e_hat=jnp.exp(tot - cum), p_tot=jnp.exp(tot))
            for p in range(N_PAIRS):
                sl = slice(p * LANES, (p + 1) * LANES)
                job = {name: val[:, sl] for name, val in rowops.items()}
                job.update(dir=d, sub=sub, pair=p)
                jobs.append(job)
    local = _chunk_local(jobs, masks)
    by_key = {(jb["dir"], jb["sub"], jb["pair"]): loc for jb, loc in zip(jobs, local)}
    states = {(d, p): state_ref[d, p] for d in range(2) for p in range(N_PAIRS)}
    for step in range(nsub):
        for d in range(2):
            sub = step if d == 0 else nsub - 1 - step
            y_ref = dirs[d][-1]
            for p in range(N_PAIRS):
                rw, y_loc, phi, dm = by_key[(d, sub, p)]
                t_in = states[(d, p)].astype(BF16)
                y_st = _mm(rw, t_in) + y_loc
                states[(d, p)] = _mm(phi, t_in) + dm
                y_ref[sub * CHUNK:(sub + 1) * CHUNK, p * LANES:(p + 1) * LANES] = y_st[:CHUNK] + y_st[CHUNK:]
    for (d, p), t in states.items():
        state_ref[d, p] = t


def _rwkv_scan(r, v, kk, lwf, kf, bf, lwb, kb, bb, nsub=1):
    b, s, _ = r.shape
    tb = nsub * CHUNK
    nb = s // tb
    fwd = pl.BlockSpec((None, tb, RWKV_WIDTH), lambda bi, c: (bi, c, 0))
    bwd = pl.BlockSpec((None, tb, RWKV_WIDTH), lambda bi, c: (bi, nb - 1 - c, 0))
    return pl.pallas_call(
        functools.partial(_rwkv_scan_kernel, nsub=nsub),
        out_shape=[jax.ShapeDtypeStruct((b, s, RWKV_WIDTH), F32)] * 2,
        grid=(b, nb),
        in_specs=[fwd] * 6 + [bwd] * 6,
        out_specs=[fwd, bwd],
        scratch_shapes=[pltpu.VMEM((2, N_PAIRS, LANES, LANES), F32)],
        compiler_params=_params("parallel", "arbitrary"),
        name="rwkv_scan",
    )(r, v, kk, lwf, kf, bf, r, v, kk, lwb, kb, bb)


def _mix0_out_kernel(x_ref, oa_ref, yf_ref, yb_ref, bonus_ref, g_ref, lnw_ref, lnb_ref, bd_ref,
                     w_ref, gain_ref, o_ref):
    y = yf_ref[...] + yb_ref[...]
    bd = bd_ref[...]
    mean = _head_sum(y, bd) * (1.0 / HEAD_DIM)
    yc = y - mean
    var = _head_sum(yc * yc, bd) * (1.0 / HEAD_DIM)
    yn = yc * lax.rsqrt(var + RWKV_GN_EPS) * lnw_ref[...] + lnb_ref[...]
    ob = (yn + bonus_ref[...]) * g_ref[...]
    m = (jnp.dot(oa_ref[...].astype(BF16), w_ref[0:DIFF_WIDTH, :], preferred_element_type=F32)
         + jnp.dot(ob.astype(BF16), w_ref[DIFF_WIDTH:, :], preferred_element_type=F32))
    o_ref[...] = x_ref[...] + _rms(m, gain_ref[...], NORM_EPS)


def _mix0_out(x2d, oa, yf, yb, bonus, g, lnw, lnb, bd, w_bf16, gain, tm=256):
    m = x2d.shape[0]
    row = lambda w: pl.BlockSpec((tm, w), lambda i: (i, 0))
    full = lambda a: pl.BlockSpec(a.shape, lambda i: (0,) * a.ndim)
    small = (lnw.reshape(1, -1), lnb.reshape(1, -1), bd, w_bf16, gain.reshape(1, -1))
    return pl.pallas_call(
        _mix0_out_kernel,
        out_shape=jax.ShapeDtypeStruct((m, D_MODEL), F32),
        grid=(m // tm,),
        in_specs=[row(D_MODEL)] + [row(RWKV_WIDTH)] * 5 + [full(a) for a in small],
        out_specs=row(D_MODEL),
        compiler_params=_params("parallel"),
        name="mix0_out",
    )(x2d, oa, yf, yb, bonus, g, *small)


def _ffn_kernel(x_ref, pre_ref, post_ref, wg_ref, wu_ref, wd_ref, o_ref, xn_ref, acc_ref):
    j = pl.program_id(1)

    @pl.when(j == 0)
    def _():
        xn_ref[...] = _rms(x_ref[...], pre_ref[...], NORM_EPS).astype(BF16)
        acc_ref[...] = jnp.zeros(acc_ref.shape, F32)

    xn = xn_ref[...]
    gate = jnp.dot(xn, wg_ref[...], preferred_element_type=F32)
    up = jnp.dot(xn, wu_ref[...], preferred_element_type=F32)
    h = (gate * _sigmoid(gate) * up).astype(BF16)
    acc_ref[...] += jnp.dot(h, wd_ref[...], preferred_element_type=F32)

    @pl.when(j == pl.num_programs(1) - 1)
    def _():
        o_ref[...] = x_ref[...] + _rms(acc_ref[...], post_ref[...], NORM_EPS)


def _ffn(x2d, pre, post, wg, wu, wd, tm=512, th=1408):
    m = x2d.shape[0]
    return pl.pallas_call(
        _ffn_kernel,
        out_shape=jax.ShapeDtypeStruct((m, D_MODEL), F32),
        grid=(m // tm, FFN_HIDDEN // th),
        in_specs=[
            pl.BlockSpec((tm, D_MODEL), lambda i, j: (i, 0)),
            pl.BlockSpec((1, D_MODEL), lambda i, j: (0, 0)),
            pl.BlockSpec((1, D_MODEL), lambda i, j: (0, 0)),
            pl.BlockSpec((D_MODEL, th), lambda i, j: (0, j)),
            pl.BlockSpec((D_MODEL, th), lambda i, j: (0, j)),
            pl.BlockSpec((th, D_MODEL), lambda i, j: (j, 0)),
        ],
        out_specs=pl.BlockSpec((tm, D_MODEL), lambda i, j: (i, 0)),
        scratch_shapes=[pltpu.VMEM((tm, D_MODEL), BF16), pltpu.VMEM((tm, D_MODEL), F32)],
        compiler_params=_params("parallel", "arbitrary"),
        name="ffn",
    )(x2d, pre.reshape(1, -1), post.reshape(1, -1), wg, wu, wd)


def _band_attn_kernel(q_ref, kp_ref, kc_ref, kn_ref, vp_ref, vc_ref, vn_ref, o_ref, lse_ref, *, length):
    tq = q_ref.shape[0]
    halo = DIL_RADIUS
    l0 = pl.program_id(2) * tq
    q = q_ref[...]
    kw = jnp.concatenate([kp_ref[...], kc_ref[...], kn_ref[...]], axis=0)
    vw = jnp.concatenate([vp_ref[...], vc_ref[...], vn_ref[...]], axis=0)
    wlen = tq + 2 * halo
    qpos = l0 + lax.broadcasted_iota(jnp.int32, (tq, wlen), 0)
    kpos = l0 - halo + lax.broadcasted_iota(jnp.int32, (tq, wlen), 1)
    valid = (jnp.abs(kpos - qpos) <= halo) & (kpos >= 0) & (kpos < length)
    lane = lax.broadcasted_iota(jnp.int32, (1, DIL_GROUP_WIDTH), 1)
    zero = jnp.zeros_like(q)
    o = jnp.zeros((tq, DIL_GROUP_WIDTH), F32)
    lse_full = jnp.zeros((tq, DIL_GROUP_WIDTH), F32)
    for h in range(DIL_GROUP_WIDTH // HEAD_DIM):
        hm = (lane // HEAD_DIM) == h
        s = lax.dot_general(jnp.where(hm, q, zero), kw, NT_DIMS, preferred_element_type=F32)
        s = jnp.where(valid, s, NEG_INF)
        mx = jnp.max(s, axis=-1, keepdims=True)
        p = jnp.exp(s - mx)
        den = jnp.sum(p, axis=-1, keepdims=True)
        oh = jnp.dot(p.astype(BF16), vw, preferred_element_type=F32) / den
        o = jnp.where(hm, oh, o)
        lse_full = jnp.where(hm, mx + jnp.log(den), lse_full)
    o_ref[...] = o
    lse_ref[...] = lse_full


def _band_attention(q, k, v, group, dilation, tq=128):
    b, s, _ = q.shape
    length = s // dilation
    nblk = length // tq
    hb = tq // DIL_RADIUS
    nh = length // DIL_RADIUS
    cpr = DIL_WIDTH // DIL_GROUP_WIDTH
    fold = lambda a: a.reshape(b, length, dilation * DIL_WIDTH)
    cur = lambda bi, r, i: (bi, i, r * cpr + group)
    prev = lambda bi, r, i: (bi, jnp.maximum(i * hb - 1, 0), r * cpr + group)
    nxt = lambda bi, r, i: (bi, jnp.minimum((i + 1) * hb, nh - 1), r * cpr + group)
    main = pl.BlockSpec((None, tq, DIL_GROUP_WIDTH), cur)
    hp = pl.BlockSpec((None, DIL_RADIUS, DIL_GROUP_WIDTH), prev)
    hn = pl.BlockSpec((None, DIL_RADIUS, DIL_GROUP_WIDTH), nxt)
    out_spec = pl.BlockSpec((None, tq, DIL_GROUP_WIDTH), lambda bi, r, i: (bi, i, r))
    out_sds = jax.ShapeDtypeStruct((b, length, dilation * DIL_GROUP_WIDTH), F32)
    o, lse = pl.pallas_call(
        functools.partial(_band_attn_kernel, length=length),
        out_shape=[out_sds, out_sds],
        grid=(b, dilation, nblk),
        in_specs=[main, hp, main, hn, hp, main, hn],
        out_specs=[out_spec, out_spec],
        compiler_params=_params("parallel", "parallel", "parallel"),
        name="band_attn",
    )(fold(q), fold(k), fold(k), fold(k), fold(v), fold(v), fold(v))
    return o.reshape(b * s, DIL_GROUP_WIDTH), lse.reshape(b * s, DIL_GROUP_WIDTH)


def _mix1_out_kernel(x_ref, o0_ref, o1_ref, o2_ref, l0_ref, l1_ref, l2_ref, w_ref, gain_ref, out_ref):
    ls = (l0_ref[...], l1_ref[...], l2_ref[...])
    os_ = (o0_ref[...], o1_ref[...], o2_ref[...])
    mx = jnp.maximum(jnp.maximum(ls[0], ls[1]), ls[2])
    es = [jnp.exp(l - mx) for l in ls]
    den = es[0] + es[1] + es[2]
    m = jnp.zeros((x_ref.shape[0], D_MODEL), F32)
    for gi in range(3):
        y = (os_[gi] * (es[gi] / den)).astype(BF16)
        m = m + jnp.dot(y, w_ref[gi * DIL_GROUP_WIDTH:(gi + 1) * DIL_GROUP_WIDTH, :],
                        preferred_element_type=F32)
    out_ref[...] = x_ref[...] + _rms(m, gain_ref[...], NORM_EPS)


def _mix1_out(x2d, outs, lses, w_bf16, gain, tm=256):
    m = x2d.shape[0]
    row = lambda w: pl.BlockSpec((tm, w), lambda i: (i, 0))
    return pl.pallas_call(
        _mix1_out_kernel,
        out_shape=jax.ShapeDtypeStruct((m, D_MODEL), F32),
        grid=(m // tm,),
        in_specs=[row(D_MODEL)] + [row(DIL_GROUP_WIDTH)] * 6
        + [pl.BlockSpec(w_bf16.shape, lambda i: (0, 0)), pl.BlockSpec((1, D_MODEL), lambda i: (0, 0))],
        out_specs=row(D_MODEL),
        compiler_params=_params("parallel"),
        name="mix1_out",
    )(x2d, *outs, *lses, w_bf16, gain.reshape(1, -1))


def _block_diag2(top, bottom):
    z_tr = jnp.zeros((top.shape[0], bottom.shape[1]), top.dtype)
    z_bl = jnp.zeros((bottom.shape[0], top.shape[1]), top.dtype)
    return jnp.concatenate([jnp.concatenate([top, z_tr], axis=1),
                            jnp.concatenate([z_bl, bottom], axis=1)], axis=0)


def kernel(x_prompt, x_sample, mix_pre0, mix_post0, w_in0, lam_q1, lam_k1, lam_q2, lam_k2, subln_w,
           mu_r, mu_k, mu_v, mu_w, mu_a, mu_g, w0_f, w1_f, w2_f, w0_b, w1_b, w2_b,
           a0_f, a1_f, a2_f, a0_b, a1_b, a2_b, g1, g2, k_k, k_a, r_k, lnx_w, lnx_b, w_out0,
           ffn_pre0, ffn_post0, ffn_gate0, ffn_up0, ffn_down0,
           mix_pre1, mix_post1, w_in1, w_out1, ffn_pre1, ffn_post1, ffn_gate1, ffn_up1, ffn_down1):
    bf = lambda a: a.astype(BF16)
    row = lambda a: a.reshape(1, -1).astype(F32)
    lam_init = 0.8 - 0.6 * math.exp(-0.3 * 0)
    lamq = jnp.stack([lam_q1, lam_q2]).astype(F32)
    lamk = jnp.stack([lam_k1, lam_k2]).astype(F32)
    gate_pad = 2 * LANES - g1.shape[1]
    ones_bd = _block_diag2(jnp.ones((HEAD_DIM, HEAD_DIM), F32), jnp.ones((HEAD_DIM, HEAD_DIM), F32))
    prep_w = (
        jnp.stack([mu_w, mu_a, mu_g]).astype(F32),
        row(jnp.concatenate([mu_r, mu_k, mu_v])),
        bf(jnp.concatenate([w1_f, w1_b], axis=1)),
        bf(_block_diag2(w2_f, w2_b)),
        bf(jnp.concatenate([a1_f, a1_b], axis=1)),
        bf(_block_diag2(a2_f, a2_b)),
        bf(jnp.pad(g1, ((0, 0), (0, gate_pad)))),
        bf(jnp.pad(g2, ((0, gate_pad), (0, 0)))),
        row(jnp.concatenate([w0_f, w0_b])),
        row(jnp.concatenate([a0_f, a0_b])),
        row(k_k), row(k_a), row(r_k.reshape(-1)),
        ones_bd,
    )
    w_in0_b, w_out0_b, w_in1_b, w_out1_b = bf(w_in0), bf(w_out0), bf(w_in1), bf(w_out1)
    ffn0 = (ffn_pre0, ffn_post0, bf(ffn_gate0), bf(ffn_up0), bf(ffn_down0))
    ffn1 = (ffn_pre1, ffn_post1, bf(ffn_gate1), bf(ffn_up1), bf(ffn_down1))

    def run(x):
        b, s, _ = x.shape
        x2d = x.reshape(b * s, D_MODEL)
        cos, sin = _rope_tables(s)
        q, k, v, rkv, xn = _norm_proj(
            x2d, mix_pre0, w_in0_b, cos, sin, s,
            splits=(DIFF_WIDTH, DIFF_WIDTH, DIFF_WIDTH, 3 * RWKV_WIDTH), dtypes=(BF16, BF16, BF16, F32),
            n_rope=2 * DIFF_WIDTH, q_cols=DIFF_WIDTH, q_scale=HEAD_DIM ** -0.5 * LOG2E, emit_xn=True)
        sh = lambda a: a.reshape(b, s, -1)
        out_a = _diff_attention(sh(q), sh(k), sh(v), lamq, lamk, subln_w, lam_init)
        r, vv, kk, lwf, lwb, kf, kb, bfw, bbw, bonus, gate = _rwkv_prep(sh(xn), sh(rkv), prep_w)
        yf, yb = _rwkv_scan(r, vv, kk, lwf, kf, bfw, lwb, kb, bbw)
        fl = lambda a: a.reshape(b * s, -1)
        x1 = _mix0_out(x2d, fl(out_a), fl(yf), fl(yb), fl(bonus), fl(gate), lnx_w, lnx_b, ones_bd,
                       w_out0_b, mix_post0)
        x2 = _ffn(x1, *ffn0)
        q1, k1, v1 = _norm_proj(
            x2, mix_pre1, w_in1_b, cos, sin, s,
            splits=(DIL_WIDTH, DIL_WIDTH, DIL_WIDTH), dtypes=(BF16, BF16, BF16),
            n_rope=2 * DIL_WIDTH, q_cols=DIL_WIDTH, q_scale=HEAD_DIM ** -0.5, emit_xn=False)
        outs, lses = [], []
        for gi, (_, dilation) in enumerate(DIL_PAIRS):
            o, lse = _band_attention(sh(q1), sh(k1), sh(v1), gi, dilation)
            outs.append(o)
            lses.append(lse)
        x3 = _mix1_out(x2, outs, lses, w_out1_b, mix_post1)
        x4 = _ffn(x3, *ffn1)
        return x4.reshape(b, s, D_MODEL)

    return (run(x_prompt), run(x_sample))
```

```python
import functools
import math

import jax
import jax.numpy as jnp
from jax import lax
from jax.experimental import pallas as pl
from jax.experimental.pallas import tpu as pltpu

F32 = jnp.float32
BF16 = jnp.bfloat16
HIGHEST = lax.Precision.HIGHEST

D_MODEL = 1024
HEAD_DIM = 64
LANES = 128
DIFF_WIDTH = 512
RWKV_WIDTH = 512
N_PAIRS = RWKV_WIDTH // LANES
DIL_PAIRS = ((128, 1), (512, 4), (2048, 16))
DIL_GROUP_WIDTH = 256
DIL_WIDTH = 768
DIL_RADIUS = 64
FFN_HIDDEN = 2816
ROPE_THETA = 10000.0
NORM_EPS = 1e-6
SUBLN_EPS = 1e-5
RWKV_GN_EPS = 64e-5
NEG_INF = -1e30
LOG2E = math.log2(math.e)
CHUNK = 64
VMEM_LIMIT = 56 * 1024 * 1024

NT_DIMS = (((1,), (1,)), ((), ()))
TN_DIMS = (((0,), (0,)), ((), ()))


def _params(*sem):
    return pltpu.CompilerParams(dimension_semantics=sem, vmem_limit_bytes=VMEM_LIMIT)


def _sigmoid(x):
    return 1.0 / (1.0 + jnp.exp(-x))


def _rms(x, gain, eps):
    return x * lax.rsqrt(jnp.mean(x * x, axis=-1, keepdims=True) + eps) * gain


def _rope_tile(x, cos, sin, upper):
    rot = jnp.where(upper, pltpu.roll(x, 32, 1), pltpu.roll(x, 96, 1))
    return x * cos + rot * sin


def _proj_kernel(x_ref, g_ref, w_ref, cos_ref, sin_ref, *out_refs, splits, n_rope, q_cols, q_scale, emit_xn):
    x = x_ref[...]
    xn = _rms(x, g_ref[...], NORM_EPS)
    if emit_xn:
        out_refs[-1][...] = xn
    xb = xn.astype(BF16)
    cos = cos_ref[...]
    sin = sin_ref[...]
    lane = lax.broadcasted_iota(jnp.int32, (1, LANES), 1)
    upper = (lane % HEAD_DIM) >= (HEAD_DIM // 2)
    col = 0
    for ref, width in zip(out_refs, splits):
        for c in range(width // LANES):
            y = jnp.dot(xb, w_ref[:, col:col + LANES], preferred_element_type=F32)
            if col < n_rope:
                y = _rope_tile(y, cos, sin, upper)
            if col < q_cols:
                y = y * q_scale
            ref[:, c * LANES:(c + 1) * LANES] = y.astype(ref.dtype)
            col += LANES


def _norm_proj(x2d, gain, w_bf16, cos, sin, seq, splits, dtypes, n_rope, q_cols, q_scale, emit_xn, tm=512):
    m = x2d.shape[0]
    n = w_bf16.shape[1]
    nseq = seq // tm
    out_shape = [jax.ShapeDtypeStruct((m, w), dt) for w, dt in zip(splits, dtypes)]
    out_specs = [pl.BlockSpec((tm, w), lambda i: (i, 0)) for w in splits]
    if emit_xn:
        out_shape.append(jax.ShapeDtypeStruct((m, D_MODEL), F32))
        out_specs.append(pl.BlockSpec((tm, D_MODEL), lambda i: (i, 0)))
    kern = functools.partial(_proj_kernel, splits=splits, n_rope=n_rope, q_cols=q_cols, q_scale=q_scale,
                             emit_xn=emit_xn)
    return pl.pallas_call(
        kern,
        out_shape=out_shape,
        grid=(m // tm,),
        in_specs=[
            pl.BlockSpec((tm, D_MODEL), lambda i: (i, 0)),
            pl.BlockSpec((1, D_MODEL), lambda i: (0, 0)),
            pl.BlockSpec((D_MODEL, n), lambda i: (0, 0)),
            pl.BlockSpec((tm, LANES), lambda i: (i % nseq, 0)),
            pl.BlockSpec((tm, LANES), lambda i: (i % nseq, 0)),
        ],
        out_specs=out_specs,
        compiler_params=_params("parallel"),
        name="norm_proj",
    )(x2d, gain.reshape(1, -1), w_bf16, cos, sin)


def _rope_tables(seq):
    half = HEAD_DIM // 2
    inv = ROPE_THETA ** (-jnp.arange(half, dtype=F32) / half)
    ang = jnp.arange(seq, dtype=F32)[:, None] * inv[None, :]
    cos = jnp.cos(ang)
    sin = jnp.sin(ang)
    cos_t = jnp.tile(jnp.concatenate([cos, cos], axis=-1), (1, LANES // HEAD_DIM))
    sin_t = jnp.tile(jnp.concatenate([-sin, sin], axis=-1), (1, LANES // HEAD_DIM))
    return cos_t, sin_t


def _diff_attn_kernel(lamq_ref, lamk_ref, subln_ref, q_ref, k_ref, v_ref, o_ref,
                      m_ref, l_ref, acc_ref, *, seq, tk, lam_init):
    q = q_ref[...]
    lane = lax.broadcasted_iota(jnp.int32, (1, LANES), 1)
    zero = jnp.zeros_like(q)
    qs = (jnp.where(lane < HEAD_DIM, q, zero), jnp.where(lane >= HEAD_DIM, q, zero))
    m_ref[...] = jnp.full(m_ref.shape, -jnp.inf, F32)
    l_ref[...] = jnp.zeros(l_ref.shape, F32)
    acc_ref[...] = jnp.zeros(acc_ref.shape, F32)
    nck = tk // LANES

    def body(j, carry):
        off = pl.multiple_of(j * tk, tk)
        kj = k_ref[pl.ds(off, tk), :]
        vj = v_ref[pl.ds(off, tk), :]
        for c in range(2):
            s = lax.dot_general(qs[c], kj, NT_DIMS, preferred_element_type=F32)
            cols = [s[:, i * LANES:(i + 1) * LANES] for i in range(nck)]
            mx = cols[0]
            for col in cols[1:]:
                mx = jnp.maximum(mx, col)
            m_old = m_ref[c]
            m_new = jnp.maximum(m_old, jnp.max(mx, axis=-1, keepdims=True))
            alpha = jnp.exp2(m_old - m_new)
            ps = [jnp.exp2(col - m_new) for col in cols]
            lsum = ps[0]
            for pc in ps[1:]:
                lsum = lsum + pc
            l_ref[c] = alpha * l_ref[c] + lsum
            p = jnp.concatenate([pc.astype(BF16) for pc in ps], axis=1)
            acc_ref[c] = alpha * acc_ref[c] + jnp.dot(p, vj, preferred_element_type=F32)
            m_ref[c] = m_new
        return carry

    lax.fori_loop(0, seq // tk, body, 0)

    e = jnp.exp(jnp.sum(lamq_ref[...] * lamk_ref[...], axis=-1, keepdims=True))
    lam = e[0:1] - e[1:2] + lam_init
    l0 = jnp.sum(l_ref[0], axis=-1, keepdims=True)
    l1 = jnp.sum(l_ref[1], axis=-1, keepdims=True)
    o = acc_ref[0] / l0 - lam * (acc_ref[1] / l1)
    o_ref[...] = _rms(o, subln_ref[...], SUBLN_EPS) * (1.0 - lam_init)


def _diff_attention(q, k, v, lamq, lamk, subln_w, lam_init, tq=512, tk=512):
    b, s, _ = q.shape
    heads = DIFF_WIDTH // LANES
    kern = functools.partial(_diff_attn_kernel, seq=s, tk=tk, lam_init=lam_init)
    return pl.pallas_call(
        kern,
        out_shape=jax.ShapeDtypeStruct((b, s, DIFF_WIDTH), F32),
        grid=(b, heads, s // tq),
        in_specs=[
            pl.BlockSpec((2, HEAD_DIM), lambda bi, h, i: (0, 0)),
            pl.BlockSpec((2, HEAD_DIM), lambda bi, h, i: (0, 0)),
            pl.BlockSpec((1, LANES), lambda bi, h, i: (0, 0)),
            pl.BlockSpec((None, tq, LANES), lambda bi, h, i: (bi, i, h)),
            pl.BlockSpec((None, s, LANES), lambda bi, h, i: (bi, 0, h)),
            pl.BlockSpec((None, s, LANES), lambda bi, h, i: (bi, 0, h)),
        ],
        out_specs=pl.BlockSpec((None, tq, LANES), lambda bi, h, i: (bi, i, h)),
        scratch_shapes=[
            pltpu.VMEM((2, tq, LANES), F32),
            pltpu.VMEM((2, tq, LANES), F32),
            pltpu.VMEM((2, tq, LANES), F32),
        ],
        compiler_params=_params("parallel", "parallel", "parallel"),
        name="diff_attn",
    )(lamq, lamk, subln_w.reshape(1, -1), q, k, v)


def _cshift(x, prev_row, next_row):
    t = x.shape[0]
    row = lax.broadcasted_iota(jnp.int32, (t, 1), 0)
    p = jnp.where(row == 0, prev_row, pltpu.roll(x, 1, 0))
    n = jnp.where(row == t - 1, next_row, pltpu.roll(x, t - 1, 0))
    return 0.5 * (p + n)


def _head_sum(x, ones_bd):
    parts = [jnp.dot(x[:, p * LANES:(p + 1) * LANES], ones_bd, precision=HIGHEST,
                     preferred_element_type=F32) for p in range(x.shape[1] // LANES)]
    return jnp.concatenate(parts, axis=1)


def _rwkv_prep_kernel(xn_ref, xnp_ref, xnn_ref, t_ref, tp_ref, tn_ref,
                      mux_ref, mut_ref, w1_ref, w2_ref, a1_ref, a2_ref, g1_ref, g2_ref,
                      w0_ref, a0_ref, kk_ref, ka_ref, rk_ref, bd_ref,
                      r_out, v_out, kk_out, lwf_out, lwb_out, kf_out, kb_out, bf_out, bb_out,
                      bonus_out, g_out):
    i = pl.program_id(1)
    first = jnp.where(i > 0, 1.0, 0.0).astype(F32)
    last = jnp.where(i < pl.num_programs(1) - 1, 1.0, 0.0).astype(F32)
    xn = xn_ref[...]
    xx = _cshift(xn, xnp_ref[7:8, :] * first, xnn_ref[0:1, :] * last) - xn
    mux = mux_ref[...]
    xw = (xn + xx * mux[0:1]).astype(BF16)
    xa = (xn + xx * mux[1:2]).astype(BF16)
    xg = (xn + xx * mux[2:3]).astype(BF16)

    t = t_ref[...]
    ts = t + (_cshift(t, tp_ref[7:8, :] * first, tn_ref[0:1, :] * last) - t) * mut_ref[...]
    r = ts[:, 0:RWKV_WIDTH]
    k = ts[:, RWKV_WIDTH:2 * RWKV_WIDTH]
    v = ts[:, 2 * RWKV_WIDTH:3 * RWKV_WIDTH]

    hw = jnp.tanh(jnp.dot(xw, w1_ref[...], preferred_element_type=F32))
    dec = jnp.dot(hw.astype(BF16), w2_ref[...], preferred_element_type=F32) + w0_ref[...]
    ha = jnp.dot(xa, a1_ref[...], preferred_element_type=F32)
    rate = _sigmoid(jnp.dot(ha.astype(BF16), a2_ref[...], preferred_element_type=F32) + a0_ref[...])
    hg = _sigmoid(jnp.dot(xg, g1_ref[...], preferred_element_type=F32))
    g_out[...] = jnp.dot(hg.astype(BF16), g2_ref[...], preferred_element_type=F32)

    lw = -math.exp(-0.5) * _sigmoid(dec)
    lwf_out[...] = lw[:, 0:RWKV_WIDTH]
    lwb_out[...] = lw[:, RWKV_WIDTH:]

    bd = bd_ref[...]
    kk = k * kk_ref[...]
    kk = kk / jnp.maximum(jnp.sqrt(_head_sum(kk * kk, bd)), 1e-12)
    a_f = rate[:, 0:RWKV_WIDTH]
    a_b = rate[:, RWKV_WIDTH:]
    ka = ka_ref[...]
    k_f = k * (1.0 + (a_f - 1.0) * ka)
    k_b = k * (1.0 + (a_b - 1.0) * ka)
    r_out[...] = r
    v_out[...] = v
    kk_out[...] = kk
    kf_out[...] = k_f
    kb_out[...] = k_b
    bf_out[...] = kk * a_f
    bb_out[...] = kk * a_b
    bonus_out[...] = _head_sum(r * (0.5 * (k_f + k_b)) * rk_ref[...], bd) * v


def _halo_specs(ts, width, seq):
    nb8 = seq // 8
    r8 = ts // 8
    return [
        pl.BlockSpec((None, ts, width), lambda b, i: (b, i, 0)),
        pl.BlockSpec((None, 8, width), lambda b, i: (b, jnp.maximum(i * r8 - 1, 0), 0)),
        pl.BlockSpec((None, 8, width), lambda b, i: (b, jnp.minimum((i + 1) * r8, nb8 - 1), 0)),
    ]


def _rwkv_prep(xn, rkv, wts, ts=256):
    b, s, _ = xn.shape
    full = lambda a: pl.BlockSpec(a.shape, lambda bi, i: (0,) * a.ndim)
    in_specs = (_halo_specs(ts, D_MODEL, s) + _halo_specs(ts, 3 * RWKV_WIDTH, s)
                + [full(a) for a in wts])
    out_spec = pl.BlockSpec((None, ts, RWKV_WIDTH), lambda bi, i: (bi, i, 0))
    return pl.pallas_call(
        _rwkv_prep_kernel,
        out_shape=[jax.ShapeDtypeStruct((b, s, RWKV_WIDTH), F32)] * 11,
        grid=(b, s // ts),
        in_specs=in_specs,
        out_specs=[out_spec] * 11,
        compiler_params=_params("parallel", "parallel"),
        name="rwkv_prep",
    )(xn, xn, xn, rkv, rkv, rkv, *wts)


def _mm(a, b):
    return jnp.dot(a, b, preferred_element_type=F32)


def _mm_nt(a, b):
    return lax.dot_general(a, b, NT_DIMS, preferred_element_type=F32)


def _mm_tn(a, b):
    return lax.dot_general(a, b, TN_DIMS, preferred_element_type=F32)


def _stack(x, m0):
    zero = jnp.zeros_like(x)
    return jnp.concatenate([jnp.where(m0, x, zero), jnp.where(m0, zero, x)], axis=0)


def _chunk_local(jobs, masks):
    eye, m0 = masks["eye"], masks["m0"]
    n2 = 2 * CHUNK
    st = []
    for jb in jobs:
        r_st = _stack(jb["r"] * jb["p_inc"], m0)
        st.append(dict(
            a=_stack(-jb["a"] * jb["p_exc"], m0).astype(BF16),
            r=r_st, rb=r_st.astype(BF16),
            b=_stack(jb["b"] * jb["p_inv"], m0).astype(BF16),
            k=_stack(jb["k"] * jb["p_inv"], m0).astype(BF16),
            v=_stack(jb["v"], m0).astype(BF16),
            bh=_stack(jb["b"] * jb["e_hat"], m0).astype(BF16),
            kh=_stack(jb["k"] * jb["e_hat"], m0).astype(BF16),
        ))
    gs = [_mm_nt(jnp.concatenate([s["a"], s["rb"]], axis=0), jnp.concatenate([s["b"], s["k"]], axis=0))
          for s in st]
    zero = jnp.zeros((n2, n2), F32)
    ident = jnp.where(eye, 1.0, 0.0).astype(F32)
    n_ab, a_ak, a_rb, a_rk = [], [], [], []
    for jb, g in zip(jobs, gs):
        strict, incl = masks["strict"][jb["dir"]], masks["incl"][jb["dir"]]
        n_ab.append(jnp.where(strict, g[:n2, :n2], zero))
        a_ak.append(jnp.where(strict, g[:n2, n2:], zero).astype(BF16))
        a_rb.append(jnp.where(incl, g[n2:, :n2], zero).astype(BF16))
        a_rk.append(jnp.where(incl, g[n2:, n2:], zero).astype(BF16))
    akv = [_mm(m, s["v"]) for m, s in zip(a_ak, st)]
    minv = [ident + n for n in n_ab]
    npow = [n.astype(BF16) for n in n_ab]
    for _ in range(int(math.log2(CHUNK)) - 1):
        npow = [_mm(n, n).astype(BF16) for n in npow]
        minv = [m + _mm(m.astype(BF16), n) for m, n in zip(minv, npow)]
    xs = [_mm(m.astype(BF16), jnp.concatenate([s["a"], u.astype(BF16)], axis=1))
          for m, s, u in zip(minv, st, akv)]
    out = []
    for jb, s, x, rb_, rk_ in zip(jobs, st, xs, a_rb, a_rk):
        w1 = x[:, :LANES].astype(BF16)
        u_loc = x[:, LANES:].astype(BF16)
        y_loc = _mm(rb_, u_loc) + _mm(rk_, s["v"])
        rw = s["r"] + _mm(rb_, w1)
        phi = jnp.where(eye, jb["p_tot"], 0.0) + _mm_tn(s["bh"], w1)
        dm = _mm_tn(s["bh"], u_loc) + _mm_tn(s["kh"], s["v"])
        out.append((rw.astype(BF16), y_loc, phi.astype(BF16), dm))
    return out


def _rwkv_scan_kernel(rf_ref, vf_ref, af_ref, lwf_ref, kf_ref, bf_ref,
                      rb_ref, vb_ref, ab_ref, lwb_ref, kb_ref, bb_ref,
                      yf_ref, yb_ref, state_ref, *, nsub):
    @pl.when(pl.program_id(1) == 0)
    def _():
        state_ref[...] = jnp.zeros(state_ref.shape, F32)

    n2 = 2 * CHUNK
    ri = lax.broadcasted_iota(jnp.int32, (n2, n2), 0)
    ci = lax.broadcasted_iota(jnp.int32, (n2, n2), 1)
    same = (ri // CHUNK) == (ci // CHUNK)
    ti = lax.broadcasted_iota(jnp.int32, (CHUNK, CHUNK), 0)
    si = lax.broadcasted_iota(jnp.int32, (CHUNK, CHUNK), 1)
    masks = dict(
        eye=ri == ci,
        m0=lax.broadcasted_iota(jnp.int32, (1, LANES), 1) < HEAD_DIM,
        strict=(same & (ci < ri), same & (ci > ri)),
        incl=(same & (ci <= ri), same & (ci >= ri)),
    )
    tris = (jnp.where(si <= ti, 1.0, 0.0).astype(F32), jnp.where(si >= ti, 1.0, 0.0).astype(F32))
    dirs = (
        (rf_ref, vf_ref, af_ref, lwf_ref, kf_ref, bf_ref, yf_ref),
        (rb_ref, vb_ref, ab_ref, lwb_ref, kb_ref, bb_ref, yb_ref),
    )
    jobs = []
    for d, (r_ref, v_ref, a_ref, lw_ref, k_ref, b_ref, _) in enumerate(dirs):
        tot_row = CHUNK - 1 if d == 0 else 0
        for sub in range(nsub):
            rows = slice(sub * CHUNK, (sub + 1) * CHUNK)
            lw = lw_ref[rows, :]
            cum = jnp.dot(tris[d], lw, precision=HIGHEST, preferred_element_type=F32)
            tot = cum[tot_row:tot_row + 1, :]
            rowops = dict(r=r_ref[rows, :], v=v_ref[rows, :], a=a_ref[rows, :], k=k_ref[rows, :],
                          b=b_ref[rows, :], p_inc=jnp.exp(cum), p_inv=jnp.exp(-cum),
                          p_exc=jnp.exp(cum - lw), e_hat=jnp.exp(tot - cum), p_tot=jnp.exp(tot))
            for p in range(N_PAIRS):
                sl = slice(p * LANES, (p + 1) * LANES)
                job = {name: val[:, sl] for name, val in rowops.items()}
                job.update(dir=d, sub=sub, pair=p)
                jobs.append(job)
    local = _chunk_local(jobs, masks)
    by_key = {(jb["dir"], jb["sub"], jb["pair"]): loc for jb, loc in zip(jobs, local)}
    states = {(d, p): state_ref[d, p] for d in range(2) for p in range(N_PAIRS)}
    for step in range(nsub):
        for d in range(2):
            sub = step if d == 0 else nsub - 1 - step
            y_ref = dirs[d][-1]
            for p in range(N_PAIRS):
                rw, y_loc, phi, dm = by_key[(d, sub, p)]
                t_in = states[(d, p)].astype(BF16)
                y_st = _mm(rw, t_in) + y_loc
                states[(d, p)] = _mm(phi, t_in) + dm
                y_ref[sub * CHUNK:(sub + 1) * CHUNK, p * LANES:(p + 1) * LANES] = y_st[:CHUNK] + y_st[CHUNK:]
    for (d, p), t in states.items():
        state_ref[d, p] = t


def _rwkv_scan(r, v, kk, lwf, kf, bf, lwb, kb, bb, nsub=2):
    b, s, _ = r.shape
    tb = nsub * CHUNK
    nb = s // tb
    fwd = pl.BlockSpec((None, tb, RWKV_WIDTH), lambda bi, c: (bi, c, 0))
    bwd = pl.BlockSpec((None, tb, RWKV_WIDTH), lambda bi, c: (bi, nb - 1 - c, 0))
    return pl.pallas_call(
        functools.partial(_rwkv_scan_kernel, nsub=nsub),
        out_shape=[jax.ShapeDtypeStruct((b, s, RWKV_WIDTH), F32)] * 2,
        grid=(b, nb),
        in_specs=[fwd] * 6 + [bwd] * 6,
        out_specs=[fwd, bwd],
        scratch_shapes=[pltpu.VMEM((2, N_PAIRS, LANES, LANES), F32)],
        compiler_params=_params("parallel", "arbitrary"),
        name="rwkv_scan",
    )(r, v, kk, lwf, kf, bf, r, v, kk, lwb, kb, bb)


def _mix0_out_kernel(x_ref, oa_ref, yf_ref, yb_ref, bonus_ref, g_ref, lnw_ref, lnb_ref, bd_ref,
                     w_ref, gain_ref, o_ref):
    y = yf_ref[...] + yb_ref[...]
    bd = bd_ref[...]
    mean = _head_sum(y, bd) * (1.0 / HEAD_DIM)
    yc = y - mean
    var = _head_sum(yc * yc, bd) * (1.0 / HEAD_DIM)
    yn = yc * lax.rsqrt(var + RWKV_GN_EPS) * lnw_ref[...] + lnb_ref[...]
    ob = (yn + bonus_ref[...]) * g_ref[...]
    m = (jnp.dot(oa_ref[...].astype(BF16), w_ref[0:DIFF_WIDTH, :], preferred_element_type=F32)
         + jnp.dot(ob.astype(BF16), w_ref[DIFF_WIDTH:, :], preferred_element_type=F32))
    o_ref[...] = x_ref[...] + _rms(m, gain_ref[...], NORM_EPS)


def _mix0_out(x2d, oa, yf, yb, bonus, g, lnw, lnb, bd, w_bf16, gain, tm=256):
    m = x2d.shape[0]
    row = lambda w: pl.BlockSpec((tm, w), lambda i: (i, 0))
    full = lambda a: pl.BlockSpec(a.shape, lambda i: (0,) * a.ndim)
    small = (lnw.reshape(1, -1), lnb.reshape(1, -1), bd, w_bf16, gain.reshape(1, -1))
    return pl.pallas_call(
        _mix0_out_kernel,
        out_shape=jax.ShapeDtypeStruct((m, D_MODEL), F32),
        grid=(m // tm,),
        in_specs=[row(D_MODEL)] + [row(RWKV_WIDTH)] * 5 + [full(a) for a in small],
        out_specs=row(D_MODEL),
        compiler_params=_params("parallel"),
        name="mix0_out",
    )(x2d, oa, yf, yb, bonus, g, *small)


def _ffn_kernel(x_ref, pre_ref, post_ref, wg_ref, wu_ref, wd_ref, o_ref, xn_ref, acc_ref):
    j = pl.program_id(1)

    @pl.when(j == 0)
    def _():
        xn_ref[...] = _rms(x_ref[...], pre_ref[...], NORM_EPS).astype(BF16)
        acc_ref[...] = jnp.zeros(acc_ref.shape, F32)

    xn = xn_ref[...]
    gate = jnp.dot(xn, wg_ref[...], preferred_element_type=F32)
    up = jnp.dot(xn, wu_ref[...], preferred_element_type=F32)
    h = (gate * _sigmoid(gate) * up).astype(BF16)
    acc_ref[...] += jnp.dot(h, wd_ref[...], preferred_element_type=F32)

    @pl.when(j == pl.num_programs(1) - 1)
    def _():
        o_ref[...] = x_ref[...] + _rms(acc_ref[...], post_ref[...], NORM_EPS)


def _ffn(x2d, pre, post, wg, wu, wd, tm=512, th=1408):
    m = x2d.shape[0]
    return pl.pallas_call(
        _ffn_kernel,
        out_shape=jax.ShapeDtypeStruct((m, D_MODEL), F32),
        grid=(m // tm, FFN_HIDDEN // th),
        in_specs=[
            pl.BlockSpec((tm, D_MODEL), lambda i, j: (i, 0)),
            pl.BlockSpec((1, D_MODEL), lambda i, j: (0, 0)),
            pl.BlockSpec((1, D_MODEL), lambda i, j: (0, 0)),
            pl.BlockSpec((D_MODEL, th), lambda i, j: (0, j)),
            pl.BlockSpec((D_MODEL, th), lambda i, j: (0, j)),
            pl.BlockSpec((th, D_MODEL), lambda i, j: (j, 0)),
        ],
        out_specs=pl.BlockSpec((tm, D_MODEL), lambda i, j: (i, 0)),
        scratch_shapes=[pltpu.VMEM((tm, D_MODEL), BF16), pltpu.VMEM((tm, D_MODEL), F32)],
        compiler_params=_params("parallel", "arbitrary"),
        name="ffn",
    )(x2d, pre.reshape(1, -1), post.reshape(1, -1), wg, wu, wd)


def _proj1_kernel(x_ref, g_ref, w_ref, cos_ref, sin_ref, *refs):
    out_refs, scr = refs[:-1], refs[-1]
    tm = x_ref.shape[0]
    xb = _rms(x_ref[...], g_ref[...], NORM_EPS).astype(BF16)
    cos = cos_ref[...]
    sin = sin_ref[...]
    lane = lax.broadcasted_iota(jnp.int32, (1, LANES), 1)
    upper = (lane % HEAD_DIM) >= (HEAD_DIM // 2)
    slot = 0
    for kind in range(3):
        for gi, (_, dil) in enumerate(DIL_PAIRS):
            ref = out_refs[kind * len(DIL_PAIRS) + gi]
            for c in range(DIL_GROUP_WIDTH // LANES):
                col = kind * DIL_WIDTH + gi * DIL_GROUP_WIDTH + c * LANES
                y = jnp.dot(xb, w_ref[:, col:col + LANES], preferred_element_type=F32)
                if kind < 2:
                    y = _rope_tile(y, cos, sin, upper)
                if kind == 0:
                    y = y * (HEAD_DIM ** -0.5)
                if dil == 1:
                    ref[:, c * LANES:(c + 1) * LANES] = y.astype(BF16)
                    continue
                scr[slot] = y
                for rho in range(dil):
                    rows = scr[slot, pl.ds(rho, tm // dil, stride=dil), :]
                    lo = rho * DIL_GROUP_WIDTH + c * LANES
                    ref[:, lo:lo + LANES] = rows.astype(BF16)
                slot += 1


def _norm_proj1(x3d, gain, w_bf16, cos, sin, tm=512):
    b, s, _ = x3d.shape
    n_fold = sum(1 for _, d in DIL_PAIRS if d > 1) * 3 * (DIL_GROUP_WIDTH // LANES)
    out_shape, out_specs = [], []
    for _ in range(3):
        for _, d in DIL_PAIRS:
            out_shape.append(jax.ShapeDtypeStruct((b, s // d, d * DIL_GROUP_WIDTH), BF16))
            out_specs.append(pl.BlockSpec((None, tm // d, d * DIL_GROUP_WIDTH), lambda bi, i: (bi, i, 0)))
    return pl.pallas_call(
        _proj1_kernel,
        out_shape=out_shape,
        grid=(b, s // tm),
        in_specs=[
            pl.BlockSpec((None, tm, D_MODEL), lambda bi, i: (bi, i, 0)),
            pl.BlockSpec((1, D_MODEL), lambda bi, i: (0, 0)),
            pl.BlockSpec(w_bf16.shape, lambda bi, i: (0, 0)),
            pl.BlockSpec((tm, LANES), lambda bi, i: (i, 0)),
            pl.BlockSpec((tm, LANES), lambda bi, i: (i, 0)),
        ],
        out_specs=out_specs,
        scratch_shapes=[pltpu.VMEM((n_fold, tm, LANES), F32)],
        compiler_params=_params("parallel", "parallel"),
        name="norm_proj1",
    )(x3d, gain.reshape(1, -1), w_bf16, cos, sin)


def _band_attn_kernel(q_ref, kp_ref, kc_ref, kn_ref, vp_ref, vc_ref, vn_ref, o_ref, lse_ref, *,
                      length, nsub, nres):
    qb = LANES
    halo = DIL_RADIUS
    wlen = qb + 2 * halo
    l0 = pl.program_id(2) * (nsub * qb)
    m0 = lax.broadcasted_iota(jnp.int32, (1, LANES), 1) < HEAD_DIM
    ti = lax.broadcasted_iota(jnp.int32, (2 * qb, wlen), 0) % qb
    ji = lax.broadcasted_iota(jnp.int32, (2 * qb, wlen), 1)
    band = jnp.abs(ji - halo - ti) <= halo
    jcol = lax.broadcasted_iota(jnp.int32, (1, wlen), 1)

    def window(p_ref, c_ref, n_ref, j, cols):
        lo = j * qb - halo
        parts = []
        if lo < 0:
            parts.append(p_ref[:, cols])
            lo = 0
        hi = min((j + 1) * qb + halo, nsub * qb)
        parts.append(c_ref[lo:hi, cols])
        if (j + 1) * qb + halo > nsub * qb:
            parts.append(n_ref[:, cols])
        return jnp.concatenate(parts, axis=0) if len(parts) > 1 else parts[0]

    jobs = [(r, j, p) for r in range(nres) for j in range(nsub) for p in range(DIL_GROUP_WIDTH // LANES)]
    scores = []
    for r, j, p in jobs:
        cols = slice(r * DIL_GROUP_WIDTH + p * LANES, r * DIL_GROUP_WIDTH + (p + 1) * LANES)
        q_st = _stack(q_ref[j * qb:(j + 1) * qb, cols], m0)
        s = _mm_nt(q_st, window(kp_ref, kc_ref, kn_ref, j, cols))
        kpos0 = l0 + j * qb - halo
        colbias = jnp.where((jcol + kpos0 >= 0) & (jcol + kpos0 < length), 0.0, NEG_INF).astype(F32)
        scores.append(jnp.where(band, s + colbias, NEG_INF))
    stats = []
    for s in scores:
        mx = jnp.max(s, axis=-1, keepdims=True)
        pr = jnp.exp(s - mx)
        stats.append((mx, jnp.sum(pr, axis=-1, keepdims=True), pr.astype(BF16)))
    for (r, j, p), (mx, den, pr) in zip(jobs, stats):
        cols = slice(r * DIL_GROUP_WIDTH + p * LANES, r * DIL_GROUP_WIDTH + (p + 1) * LANES)
        o_st = _mm(pr, window(vp_ref, vc_ref, vn_ref, j, cols)) / den
        lse = mx + jnp.log(den)
        o_ref[j * qb:(j + 1) * qb, cols] = jnp.where(m0, o_st[:qb], o_st[qb:])
        lse_ref[j * qb:(j + 1) * qb, cols] = jnp.where(m0, lse[:qb], lse[qb:])


def _band_attention(q, k, v, dilation):
    b, length, width = q.shape
    nsub = min(4, length // LANES)
    nres = min(dilation, 4 // nsub)
    tq = nsub * LANES
    hb = tq // DIL_RADIUS
    nh = length // DIL_RADIUS
    bw = nres * DIL_GROUP_WIDTH
    cur = lambda bi, r, i: (bi, i, r)
    prev = lambda bi, r, i: (bi, jnp.maximum(i * hb - 1, 0), r)
    nxt = lambda bi, r, i: (bi, jnp.minimum((i + 1) * hb, nh - 1), r)
    main = pl.BlockSpec((None, tq, bw), cur)
    hp = pl.BlockSpec((None, DIL_RADIUS, bw), prev)
    hn = pl.BlockSpec((None, DIL_RADIUS, bw), nxt)
    out_sds = jax.ShapeDtypeStruct((b, length, width), F32)
    return pl.pallas_call(
        functools.partial(_band_attn_kernel, length=length, nsub=nsub, nres=nres),
        out_shape=[out_sds, out_sds],
        grid=(b, dilation // nres, length // tq),
        in_specs=[main, hp, main, hn, hp, main, hn],
        out_specs=[main, main],
        compiler_params=_params("parallel", "parallel", "parallel"),
        name="band_attn",
    )(q, k, k, k, v, v, v)


def _mix1_out_kernel(x_ref, o0_ref, o1_ref, o2_ref, l0_ref, l1_ref, l2_ref, w_ref, gain_ref, out_ref,
                     scr):
    tm = x_ref.shape[0]

    def unfold(ref, dil, slot):
        if dil == 1:
            return ref[...]
        halves = DIL_GROUP_WIDTH // LANES
        for rho in range(dil):
            for c in range(halves):
                lo = rho * DIL_GROUP_WIDTH + c * LANES
                scr[slot * halves + c, pl.ds(rho, tm // dil, stride=dil), :] = ref[:, lo:lo + LANES]
        return jnp.concatenate([scr[slot * halves + c] for c in range(halves)], axis=1)

    dils = [d for _, d in DIL_PAIRS]
    os_, ls, slot = [], [], 0
    for o_ref, l_ref, d in zip((o0_ref, o1_ref, o2_ref), (l0_ref, l1_ref, l2_ref), dils):
        os_.append(unfold(o_ref, d, slot))
        ls.append(unfold(l_ref, d, slot + 1))
        slot += 2 if d > 1 else 0
    mx = jnp.maximum(jnp.maximum(ls[0], ls[1]), ls[2])
    es = [jnp.exp(l - mx) for l in ls]
    den = es[0] + es[1] + es[2]
    m = jnp.zeros((tm, D_MODEL), F32)
    for gi in range(3):
        y = (os_[gi] * (es[gi] / den)).astype(BF16)
        m = m + jnp.dot(y, w_ref[gi * DIL_GROUP_WIDTH:(gi + 1) * DIL_GROUP_WIDTH, :],
                        preferred_element_type=F32)
    out_ref[...] = x_ref[...] + _rms(m, gain_ref[...], NORM_EPS)


def _mix1_out(x3d, outs, lses, w_bf16, gain, tm=256):
    b, s, _ = x3d.shape
    row = pl.BlockSpec((None, tm, D_MODEL), lambda bi, i: (bi, i, 0))
    folded = [pl.BlockSpec((None, tm // d, d * DIL_GROUP_WIDTH), lambda bi, i: (bi, i, 0)) for _, d in DIL_PAIRS]
    n_slots = 2 * sum(1 for _, d in DIL_PAIRS if d > 1)
    return pl.pallas_call(
        _mix1_out_kernel,
        out_shape=jax.ShapeDtypeStruct((b, s, D_MODEL), F32),
        grid=(b, s // tm),
        in_specs=[row] + folded + folded
        + [pl.BlockSpec(w_bf16.shape, lambda bi, i: (0, 0)), pl.BlockSpec((1, D_MODEL), lambda bi, i: (0, 0))],
        out_specs=row,
        scratch_shapes=[pltpu.VMEM((n_slots * DIL_GROUP_WIDTH // LANES, tm, LANES), F32)],
        compiler_params=_params("parallel", "parallel"),
        name="mix1_out",
    )(x3d, *outs, *lses, w_bf16, gain.reshape(1, -1))


def _block_diag2(top, bottom):
    z_tr = jnp.zeros((top.shape[0], bottom.shape[1]), top.dtype)
    z_bl = jnp.zeros((bottom.shape[0], top.shape[1]), top.dtype)
    return jnp.concatenate([jnp.concatenate([top, z_tr], axis=1),
                            jnp.concatenate([z_bl, bottom], axis=1)], axis=0)


def kernel(x_prompt, x_sample, mix_pre0, mix_post0, w_in0, lam_q1, lam_k1, lam_q2, lam_k2, subln_w,
           mu_r, mu_k, mu_v, mu_w, mu_a, mu_g, w0_f, w1_f, w2_f, w0_b, w1_b, w2_b,
           a0_f, a1_f, a2_f, a0_b, a1_b, a2_b, g1, g2, k_k, k_a, r_k, lnx_w, lnx_b, w_out0,
           ffn_pre0, ffn_post0, ffn_gate0, ffn_up0, ffn_down0,
           mix_pre1, mix_post1, w_in1, w_out1, ffn_pre1, ffn_post1, ffn_gate1, ffn_up1, ffn_down1):
    bf = lambda a: a.astype(BF16)
    row = lambda a: a.reshape(1, -1).astype(F32)
    lam_init = 0.8 - 0.6 * math.exp(-0.3 * 0)
    lamq = jnp.stack([lam_q1, lam_q2]).astype(F32)
    lamk = jnp.stack([lam_k1, lam_k2]).astype(F32)
    gate_pad = 2 * LANES - g1.shape[1]
    ones_bd = _block_diag2(jnp.ones((HEAD_DIM, HEAD_DIM), F32), jnp.ones((HEAD_DIM, HEAD_DIM), F32))
    prep_w = (
        jnp.stack([mu_w, mu_a, mu_g]).astype(F32),
        row(jnp.concatenate([mu_r, mu_k, mu_v])),
        bf(jnp.concatenate([w1_f, w1_b], axis=1)),
        bf(_block_diag2(w2_f, w2_b)),
        bf(jnp.concatenate([a1_f, a1_b], axis=1)),
        bf(_block_diag2(a2_f, a2_b)),
        bf(jnp.pad(g1, ((0, 0), (0, gate_pad)))),
        bf(jnp.pad(g2, ((0, gate_pad), (0, 0)))),
        row(jnp.concatenate([w0_f, w0_b])),
        row(jnp.concatenate([a0_f, a0_b])),
        row(k_k), row(k_a), row(r_k.reshape(-1)),
        ones_bd,
    )
    w_in0_b, w_out0_b, w_in1_b, w_out1_b = bf(w_in0), bf(w_out0), bf(w_in1), bf(w_out1)
    ffn0 = (ffn_pre0, ffn_post0, bf(ffn_gate0), bf(ffn_up0), bf(ffn_down0))
    ffn1 = (ffn_pre1, ffn_post1, bf(ffn_gate1), bf(ffn_up1), bf(ffn_down1))

    def run(x):
        b, s, _ = x.shape
        x2d = x.reshape(b * s, D_MODEL)
        cos, sin = _rope_tables(s)
        q, k, v, rkv, xn = _norm_proj(
            x2d, mix_pre0, w_in0_b, cos, sin, s,
            splits=(DIFF_WIDTH, DIFF_WIDTH, DIFF_WIDTH, 3 * RWKV_WIDTH), dtypes=(BF16, BF16, BF16, F32),
            n_rope=2 * DIFF_WIDTH, q_cols=DIFF_WIDTH, q_scale=HEAD_DIM ** -0.5 * LOG2E, emit_xn=True)
        sh = lambda a: a.reshape(b, s, -1)
        out_a = _diff_attention(sh(q), sh(k), sh(v), lamq, lamk, subln_w, lam_init)
        r, vv, kk, lwf, lwb, kf, kb, bfw, bbw, bonus, gate = _rwkv_prep(sh(xn), sh(rkv), prep_w)
        yf, yb = _rwkv_scan(r, vv, kk, lwf, kf, bfw, lwb, kb, bbw)
        fl = lambda a: a.reshape(b * s, -1)
        x1 = _mix0_out(x2d, fl(out_a), fl(yf), fl(yb), fl(bonus), fl(gate), lnx_w, lnx_b, ones_bd,
                       w_out0_b, mix_post0)
        x2 = _ffn(x1, *ffn0)
        x2 = x2.reshape(b, s, D_MODEL)
        qkv1 = _norm_proj1(x2, mix_pre1, w_in1_b, cos, sin)
        outs, lses = [], []
        for gi, (_, dilation) in enumerate(DIL_PAIRS):
            o, lse = _band_attention(qkv1[gi], qkv1[3 + gi], qkv1[6 + gi], dilation)
            outs.append(o)
            lses.append(lse)
        x3 = _mix1_out(x2, outs, lses, w_out1_b, mix_post1)
        x4 = _ffn(x3.reshape(b * s, D_MODEL), *ffn1)
        return x4.reshape(b, s, D_MODEL)

    return (run(x_prompt), run(x_sample))
```

```python
import functools
import math

import jax
import jax.numpy as jnp
from jax import lax
from jax.experimental import pallas as pl
from jax.experimental.pallas import tpu as pltpu

F32 = jnp.float32
BF16 = jnp.bfloat16
HIGHEST = lax.Precision.HIGHEST

D_MODEL = 1024
HEAD_DIM = 64
LANES = 128
DIFF_WIDTH = 512
RWKV_WIDTH = 512
N_PAIRS = RWKV_WIDTH // LANES
DIL_PAIRS = ((128, 1), (512, 4), (2048, 16))
DIL_GROUP_WIDTH = 256
DIL_WIDTH = 768
DIL_RADIUS = 64
FFN_HIDDEN = 2816
ROPE_THETA = 10000.0
NORM_EPS = 1e-6
SUBLN_EPS = 1e-5
RWKV_GN_EPS = 64e-5
NEG_INF = -1e30
LOG2E = math.log2(math.e)
CHUNK = 64
VMEM_LIMIT = 56 * 1024 * 1024

NT_DIMS = (((1,), (1,)), ((), ()))
TN_DIMS = (((0,), (0,)), ((), ()))


def _params(*sem):
    return pltpu.CompilerParams(dimension_semantics=sem, vmem_limit_bytes=VMEM_LIMIT)


def _sigmoid(x):
    return 1.0 / (1.0 + jnp.exp(-x))


def _rms(x, gain, eps):
    return x * lax.rsqrt(jnp.mean(x * x, axis=-1, keepdims=True) + eps) * gain


def _rope_tile(x, cos, sin, upper):
    rot = jnp.where(upper, pltpu.roll(x, 32, 1), pltpu.roll(x, 96, 1))
    return x * cos + rot * sin


def _proj_kernel(x_ref, g_ref, w_ref, cos_ref, sin_ref, *out_refs, splits, n_rope, q_cols, q_scale, emit_xn):
    x = x_ref[...]
    xn = _rms(x, g_ref[...], NORM_EPS)
    if emit_xn:
        out_refs[-1][...] = xn
    xb = xn.astype(BF16)
    cos = cos_ref[...]
    sin = sin_ref[...]
    lane = lax.broadcasted_iota(jnp.int32, (1, LANES), 1)
    upper = (lane % HEAD_DIM) >= (HEAD_DIM // 2)
    col = 0
    for ref, width in zip(out_refs, splits):
        for c in range(width // LANES):
            y = jnp.dot(xb, w_ref[:, col:col + LANES], preferred_element_type=F32)
            if col < n_rope:
                y = _rope_tile(y, cos, sin, upper)
            if col < q_cols:
                y = y * q_scale
            ref[:, c * LANES:(c + 1) * LANES] = y.astype(ref.dtype)
            col += LANES


def _norm_proj(x2d, gain, w_bf16, cos, sin, seq, splits, dtypes, n_rope, q_cols, q_scale, emit_xn, tm=512):
    m = x2d.shape[0]
    n = w_bf16.shape[1]
    nseq = seq // tm
    out_shape = [jax.ShapeDtypeStruct((m, w), dt) for w, dt in zip(splits, dtypes)]
    out_specs = [pl.BlockSpec((tm, w), lambda i: (i, 0)) for w in splits]
    if emit_xn:
        out_shape.append(jax.ShapeDtypeStruct((m, D_MODEL), F32))
        out_specs.append(pl.BlockSpec((tm, D_MODEL), lambda i: (i, 0)))
    kern = functools.partial(_proj_kernel, splits=splits, n_rope=n_rope, q_cols=q_cols, q_scale=q_scale,
                             emit_xn=emit_xn)
    return pl.pallas_call(
        kern,
        out_shape=out_shape,
        grid=(m // tm,),
        in_specs=[
            pl.BlockSpec((tm, D_MODEL), lambda i: (i, 0)),
            pl.BlockSpec((1, D_MODEL), lambda i: (0, 0)),
            pl.BlockSpec((D_MODEL, n), lambda i: (0, 0)),
            pl.BlockSpec((tm, LANES), lambda i: (i % nseq, 0)),
            pl.BlockSpec((tm, LANES), lambda i: (i % nseq, 0)),
        ],
        out_specs=out_specs,
        compiler_params=_params("parallel"),
        name="norm_proj",
    )(x2d, gain.reshape(1, -1), w_bf16, cos, sin)


def _rope_tables(seq):
    half = HEAD_DIM // 2
    inv = ROPE_THETA ** (-jnp.arange(half, dtype=F32) / half)
    ang = jnp.arange(seq, dtype=F32)[:, None] * inv[None, :]
    cos = jnp.cos(ang)
    sin = jnp.sin(ang)
    cos_t = jnp.tile(jnp.concatenate([cos, cos], axis=-1), (1, LANES // HEAD_DIM))
    sin_t = jnp.tile(jnp.concatenate([-sin, sin], axis=-1), (1, LANES // HEAD_DIM))
    return cos_t, sin_t


def _diff_attn_kernel(lamq_ref, lamk_ref, subln_ref, q_ref, k_ref, v_ref, o_ref,
                      m_ref, l_ref, acc_ref, s_ref, *, seq, tk, lam_init):
    q = q_ref[...]
    tq = q.shape[0]
    lane = lax.broadcasted_iota(jnp.int32, (1, LANES), 1)
    zero = jnp.zeros_like(q)
    qs = (jnp.where(lane < HEAD_DIM, q, zero), jnp.where(lane >= HEAD_DIM, q, zero))
    m_ref[...] = jnp.full(m_ref.shape, -jnp.inf, F32)
    l_ref[...] = jnp.zeros(l_ref.shape, F32)
    acc_ref[...] = jnp.zeros(acc_ref.shape, F32)
    nck = tk // LANES
    nblk = seq // tk
    row_parts = 2

    def scores(j, slot):
        off = pl.multiple_of(j * tk, tk)
        kj = k_ref[pl.ds(off, tk), :]
        for c in range(2):
            s_ref[slot, c] = lax.dot_general(qs[c], kj, NT_DIMS, preferred_element_type=F32)

    def consume(j, slot):
        off = pl.multiple_of(j * tk, tk)
        vj = v_ref[pl.ds(off, tk), :]
        for c in range(2):
            for h in range(row_parts):
                rows = slice(h * tq // row_parts, (h + 1) * tq // row_parts)
                cols = [s_ref[slot, c, rows, i * LANES:(i + 1) * LANES] for i in range(nck)]
                mx = cols[0]
                for col in cols[1:]:
                    mx = jnp.maximum(mx, col)
                m_old = m_ref[c, rows, :]
                m_new = jnp.maximum(m_old, jnp.max(mx, axis=-1, keepdims=True))
                alpha = jnp.exp2(m_old - m_new)
                ps = [jnp.exp2(col - m_new) for col in cols]
                lsum = ps[0]
                for pc in ps[1:]:
                    lsum = lsum + pc
                l_ref[c, rows, :] = alpha * l_ref[c, rows, :] + lsum
                p = jnp.concatenate([pc.astype(BF16) for pc in ps], axis=1)
                acc_ref[c, rows, :] = (alpha * acc_ref[c, rows, :]
                                       + jnp.dot(p, vj, preferred_element_type=F32))
                m_ref[c, rows, :] = m_new

    scores(0, 0)

    def body(i, carry):
        scores(2 * i + 1, 1)
        consume(2 * i, 0)
        scores(2 * i + 2, 0)
        consume(2 * i + 1, 1)
        return carry

    lax.fori_loop(0, nblk // 2 - 1, body, 0)
    scores(nblk - 1, 1)
    consume(nblk - 2, 0)
    consume(nblk - 1, 1)

    e = jnp.exp(jnp.sum(lamq_ref[...] * lamk_ref[...], axis=-1, keepdims=True))
    lam = e[0:1] - e[1:2] + lam_init
    l0 = jnp.sum(l_ref[0], axis=-1, keepdims=True)
    l1 = jnp.sum(l_ref[1], axis=-1, keepdims=True)
    o = acc_ref[0] / l0 - lam * (acc_ref[1] / l1)
    o_ref[...] = _rms(o, subln_ref[...], SUBLN_EPS) * (1.0 - lam_init)


def _diff_attention(q, k, v, lamq, lamk, subln_w, lam_init, tq=512, tk=512):
    b, s, _ = q.shape
    heads = DIFF_WIDTH // LANES
    kern = functools.partial(_diff_attn_kernel, seq=s, tk=tk, lam_init=lam_init)
    return pl.pallas_call(
        kern,
        out_shape=jax.ShapeDtypeStruct((b, s, DIFF_WIDTH), F32),
        grid=(b, heads, s // tq),
        in_specs=[
            pl.BlockSpec((2, HEAD_DIM), lambda bi, h, i: (0, 0)),
            pl.BlockSpec((2, HEAD_DIM), lambda bi, h, i: (0, 0)),
            pl.BlockSpec((1, LANES), lambda bi, h, i: (0, 0)),
            pl.BlockSpec((None, tq, LANES), lambda bi, h, i: (bi, i, h)),
            pl.BlockSpec((None, s, LANES), lambda bi, h, i: (bi, 0, h)),
            pl.BlockSpec((None, s, LANES), lambda bi, h, i: (bi, 0, h)),
        ],
        out_specs=pl.BlockSpec((None, tq, LANES), lambda bi, h, i: (bi, i, h)),
        scratch_shapes=[
            pltpu.VMEM((2, tq, LANES), F32),
            pltpu.VMEM((2, tq, LANES), F32),
            pltpu.VMEM((2, tq, LANES), F32),
            pltpu.VMEM((2, 2, tq, tk), F32),
        ],
        compiler_params=_params("parallel", "parallel", "parallel"),
        name="diff_attn",
    )(lamq, lamk, subln_w.reshape(1, -1), q, k, v)


def _cshift(x, prev_row, next_row):
    t = x.shape[0]
    row = lax.broadcasted_iota(jnp.int32, (t, 1), 0)
    p = jnp.where(row == 0, prev_row, pltpu.roll(x, 1, 0))
    n = jnp.where(row == t - 1, next_row, pltpu.roll(x, t - 1, 0))
    return 0.5 * (p + n)


def _head_sum(x, ones_bd):
    parts = [jnp.dot(x[:, p * LANES:(p + 1) * LANES], ones_bd, precision=HIGHEST,
                     preferred_element_type=F32) for p in range(x.shape[1] // LANES)]
    return jnp.concatenate(parts, axis=1)


def _rwkv_prep_kernel(xn_ref, xnp_ref, xnn_ref, t_ref, tp_ref, tn_ref,
                      mux_ref, mut_ref, w1_ref, w2_ref, a1_ref, a2_ref, g1_ref, g2_ref,
                      w0_ref, a0_ref, kk_ref, ka_ref, rk_ref, bd_ref,
                      r_out, v_out, kk_out, lwf_out, lwb_out, kf_out, kb_out, bf_out, bb_out,
                      bonus_out, g_out):
    i = pl.program_id(1)
    first = jnp.where(i > 0, 1.0, 0.0).astype(F32)
    last = jnp.where(i < pl.num_programs(1) - 1, 1.0, 0.0).astype(F32)
    xn = xn_ref[...]
    xx = _cshift(xn, xnp_ref[7:8, :] * first, xnn_ref[0:1, :] * last) - xn
    mux = mux_ref[...]
    xw = (xn + xx * mux[0:1]).astype(BF16)
    xa = (xn + xx * mux[1:2]).astype(BF16)
    xg = (xn + xx * mux[2:3]).astype(BF16)

    t = t_ref[...]
    ts = t + (_cshift(t, tp_ref[7:8, :] * first, tn_ref[0:1, :] * last) - t) * mut_ref[...]
    r = ts[:, 0:RWKV_WIDTH]
    k = ts[:, RWKV_WIDTH:2 * RWKV_WIDTH]
    v = ts[:, 2 * RWKV_WIDTH:3 * RWKV_WIDTH]

    hw = jnp.tanh(jnp.dot(xw, w1_ref[...], preferred_element_type=F32))
    dec = jnp.dot(hw.astype(BF16), w2_ref[...], preferred_element_type=F32) + w0_ref[...]
    ha = jnp.dot(xa, a1_ref[...], preferred_element_type=F32)
    rate = _sigmoid(jnp.dot(ha.astype(BF16), a2_ref[...], preferred_element_type=F32) + a0_ref[...])
    hg = _sigmoid(jnp.dot(xg, g1_ref[...], preferred_element_type=F32))
    g_out[...] = jnp.dot(hg.astype(BF16), g2_ref[...], preferred_element_type=F32)

    lw = -math.exp(-0.5) * _sigmoid(dec)
    lwf_out[...] = lw[:, 0:RWKV_WIDTH]
    lwb_out[...] = lw[:, RWKV_WIDTH:]

    bd = bd_ref[...]
    kk = k * kk_ref[...]
    kk = kk / jnp.maximum(jnp.sqrt(_head_sum(kk * kk, bd)), 1e-12)
    a_f = rate[:, 0:RWKV_WIDTH]
    a_b = rate[:, RWKV_WIDTH:]
    ka = ka_ref[...]
    k_f = k * (1.0 + (a_f - 1.0) * ka)
    k_b = k * (1.0 + (a_b - 1.0) * ka)
    r_out[...] = r
    v_out[...] = v
    kk_out[...] = kk
    kf_out[...] = k_f
    kb_out[...] = k_b
    bf_out[...] = kk * a_f
    bb_out[...] = kk * a_b
    bonus_out[...] = _head_sum(r * (0.5 * (k_f + k_b)) * rk_ref[...], bd) * v


def _halo_specs(ts, width, seq):
    nb8 = seq // 8
    r8 = ts // 8
    return [
        pl.BlockSpec((None, ts, width), lambda b, i: (b, i, 0)),
        pl.BlockSpec((None, 8, width), lambda b, i: (b, jnp.maximum(i * r8 - 1, 0), 0)),
        pl.BlockSpec((None, 8, width), lambda b, i: (b, jnp.minimum((i + 1) * r8, nb8 - 1), 0)),
    ]


def _rwkv_prep(xn, rkv, wts, ts=256):
    b, s, _ = xn.shape
    full = lambda a: pl.BlockSpec(a.shape, lambda bi, i: (0,) * a.ndim)
    in_specs = (_halo_specs(ts, D_MODEL, s) + _halo_specs(ts, 3 * RWKV_WIDTH, s)
                + [full(a) for a in wts])
    out_spec = pl.BlockSpec((None, ts, RWKV_WIDTH), lambda bi, i: (bi, i, 0))
    return pl.pallas_call(
        _rwkv_prep_kernel,
        out_shape=[jax.ShapeDtypeStruct((b, s, RWKV_WIDTH), F32)] * 11,
        grid=(b, s // ts),
        in_specs=in_specs,
        out_specs=[out_spec] * 11,
        compiler_params=_params("parallel", "parallel"),
        name="rwkv_prep",
    )(xn, xn, xn, rkv, rkv, rkv, *wts)


def _mm(a, b):
    return jnp.dot(a, b, preferred_element_type=F32)


def _mm_nt(a, b):
    return lax.dot_general(a, b, NT_DIMS, preferred_element_type=F32)


def _mm_tn(a, b):
    return lax.dot_general(a, b, TN_DIMS, preferred_element_type=F32)


def _stack(x, m0):
    zero = jnp.zeros_like(x)
    return jnp.concatenate([jnp.where(m0, x, zero), jnp.where(m0, zero, x)], axis=0)


def _chunk_local(jobs, masks):
    eye, m0 = masks["eye"], masks["m0"]
    n2 = 2 * CHUNK
    st = []
    for jb in jobs:
        r_st = _stack(jb["r"] * jb["p_inc"], m0)
        st.append(dict(
            a=_stack(-jb["a"] * jb["p_exc"], m0).astype(BF16),
            r=r_st, rb=r_st.astype(BF16),
            b=_stack(jb["b"] * jb["p_inv"], m0).astype(BF16),
            k=_stack(jb["k"] * jb["p_inv"], m0).astype(BF16),
            v=_stack(jb["v"], m0).astype(BF16),
            bh=_stack(jb["b"] * jb["e_hat"], m0).astype(BF16),
            kh=_stack(jb["k"] * jb["e_hat"], m0).astype(BF16),
        ))
    gs = [_mm_nt(jnp.concatenate([s["a"], s["rb"]], axis=0), jnp.concatenate([s["b"], s["k"]], axis=0))
          for s in st]
    zero = jnp.zeros((n2, n2), F32)
    ident = jnp.where(eye, 1.0, 0.0).astype(F32)
    n_ab, a_ak, a_rb, a_rk = [], [], [], []
    for jb, g in zip(jobs, gs):
        strict, incl = masks["strict"][jb["dir"]], masks["incl"][jb["dir"]]
        n_ab.append(jnp.where(strict, g[:n2, :n2], zero))
        a_ak.append(jnp.where(strict, g[:n2, n2:], zero).astype(BF16))
        a_rb.append(jnp.where(incl, g[n2:, :n2], zero).astype(BF16))
        a_rk.append(jnp.where(incl, g[n2:, n2:], zero).astype(BF16))
    akv = [_mm(m, s["v"]) for m, s in zip(a_ak, st)]
    minv = [ident + n for n in n_ab]
    npow = [n.astype(BF16) for n in n_ab]
    for _ in range(int(math.log2(CHUNK)) - 1):
        npow = [_mm(n, n).astype(BF16) for n in npow]
        minv = [m + _mm(m.astype(BF16), n) for m, n in zip(minv, npow)]
    xs = [_mm(m.astype(BF16), jnp.concatenate([s["a"], u.astype(BF16)], axis=1))
          for m, s, u in zip(minv, st, akv)]
    out = []
    for jb, s, x, rb_, rk_ in zip(jobs, st, xs, a_rb, a_rk):
        w1 = x[:, :LANES].astype(BF16)
        u_loc = x[:, LANES:].astype(BF16)
        y_loc = _mm(rb_, u_loc) + _mm(rk_, s["v"])
        rw = s["r"] + _mm(rb_, w1)
        phi = jnp.where(eye, jb["p_tot"], 0.0) + _mm_tn(s["bh"], w1)
        dm = _mm_tn(s["bh"], u_loc) + _mm_tn(s["kh"], s["v"])
        out.append((rw.astype(BF16), y_loc, phi.astype(BF16), dm))
    return out


def _rwkv_scan_kernel(rf_ref, vf_ref, af_ref, lwf_ref, kf_ref, bf_ref,
                      rb_ref, vb_ref, ab_ref, lwb_ref, kb_ref, bb_ref,
                      yf_ref, yb_ref, state_ref, *, nsub):
    @pl.when(pl.program_id(1) == 0)
    def _():
        state_ref[...] = jnp.zeros(state_ref.shape, F32)

    n2 = 2 * CHUNK
    ri = lax.broadcasted_iota(jnp.int32, (n2, n2), 0)
    ci = lax.broadcasted_iota(jnp.int32, (n2, n2), 1)
    same = (ri // CHUNK) == (ci // CHUNK)
    ti = lax.broadcasted_iota(jnp.int32, (CHUNK, CHUNK), 0)
    si = lax.broadcasted_iota(jnp.int32, (CHUNK, CHUNK), 1)
    masks = dict(
        eye=ri == ci,
        m0=lax.broadcasted_iota(jnp.int32, (1, LANES), 1) < HEAD_DIM,
        strict=(same & (ci < ri), same & (ci > ri)),
        incl=(same & (ci <= ri), same & (ci >= ri)),
    )
    tris = (jnp.where(si <= ti, 1.0, 0.0).astype(F32), jnp.where(si >= ti, 1.0, 0.0).astype(F32))
    dirs = (
        (rf_ref, vf_ref, af_ref, lwf_ref, kf_ref, bf_ref, yf_ref),
        (rb_ref, vb_ref, ab_ref, lwb_ref, kb_ref, bb_ref, yb_ref),
    )
    jobs = []
    for d, (r_ref, v_ref, a_ref, lw_ref, k_ref, b_ref, _) in enumerate(dirs):
        tot_row = CHUNK - 1 if d == 0 else 0
        for sub in range(nsub):
            rows = slice(sub * CHUNK, (sub + 1) * CHUNK)
            lw = lw_ref[rows, :]
            cum = jnp.dot(tris[d], lw, precision=HIGHEST, preferred_element_type=F32)
            tot = cum[tot_row:tot_row + 1, :]
            rowops = dict(r=r_ref[rows, :], v=v_ref[rows, :], a=a_ref[rows, :], k=k_ref[rows, :],
                          b=b_ref[rows, :], p_inc=jnp.exp(cum), p_inv=jnp.exp(-cum),
                          p_exc=jnp.exp(cum - lw), e_hat=jnp.exp(tot - cum), p_tot=jnp.exp(tot))
            for p in range(N_PAIRS):
                sl = slice(p * LANES, (p + 1) * LANES)
                job = {name: val[:, sl] for name, val in rowops.items()}
                job.update(dir=d, sub=sub, pair=p)
                jobs.append(job)
    local = _chunk_local(jobs, masks)
    by_key = {(jb["dir"], jb["sub"], jb["pair"]): loc for jb, loc in zip(jobs, local)}
    states = {(d, p): state_ref[d, p] for d in range(2) for p in range(N_PAIRS)}
    for step in range(nsub):
        for d in range(2):
            sub = step if d == 0 else nsub - 1 - step
            y_ref = dirs[d][-1]
            for p in range(N_PAIRS):
                rw, y_loc, phi, dm = by_key[(d, sub, p)]
                t_in = states[(d, p)].astype(BF16)
                y_st = _mm(rw, t_in) + y_loc
                states[(d, p)] = _mm(phi, t_in) + dm
                y_ref[sub * CHUNK:(sub + 1) * CHUNK, p * LANES:(p + 1) * LANES] = y_st[:CHUNK] + y_st[CHUNK:]
    for (d, p), t in states.items():
        state_ref[d, p] = t


def _rwkv_scan(r, v, kk, lwf, kf, bf, lwb, kb, bb, nsub=2):
    b, s, _ = r.shape
    tb = nsub * CHUNK
    nb = s // tb
    fwd = pl.BlockSpec((None, tb, RWKV_WIDTH), lambda bi, c: (bi, c, 0))
    bwd = pl.BlockSpec((None, tb, RWKV_WIDTH), lambda bi, c: (bi, nb - 1 - c, 0))
    return pl.pallas_call(
        functools.partial(_rwkv_scan_kernel, nsub=nsub),
        out_shape=[jax.ShapeDtypeStruct((b, s, RWKV_WIDTH), F32)] * 2,
        grid=(b, nb),
        in_specs=[fwd] * 6 + [bwd] * 6,
        out_specs=[fwd, bwd],
        scratch_shapes=[pltpu.VMEM((2, N_PAIRS, LANES, LANES), F32)],
        compiler_params=_params("parallel", "arbitrary"),
        name="rwkv_scan",
    )(r, v, kk, lwf, kf, bf, r, v, kk, lwb, kb, bb)


def _mix0_out_kernel(x_ref, oa_ref, yf_ref, yb_ref, bonus_ref, g_ref, lnw_ref, lnb_ref, bd_ref,
                     w_ref, gain_ref, o_ref):
    y = yf_ref[...] + yb_ref[...]
    bd = bd_ref[...]
    mean = _head_sum(y, bd) * (1.0 / HEAD_DIM)
    yc = y - mean
    var = _head_sum(yc * yc, bd) * (1.0 / HEAD_DIM)
    yn = yc * lax.rsqrt(var + RWKV_GN_EPS) * lnw_ref[...] + lnb_ref[...]
    ob = (yn + bonus_ref[...]) * g_ref[...]
    m = (jnp.dot(oa_ref[...].astype(BF16), w_ref[0:DIFF_WIDTH, :], preferred_element_type=F32)
         + jnp.dot(ob.astype(BF16), w_ref[DIFF_WIDTH:, :], preferred_element_type=F32))
    o_ref[...] = x_ref[...] + _rms(m, gain_ref[...], NORM_EPS)


def _mix0_out(x2d, oa, yf, yb, bonus, g, lnw, lnb, bd, w_bf16, gain, tm=256):
    m = x2d.shape[0]
    row = lambda w: pl.BlockSpec((tm, w), lambda i: (i, 0))
    full = lambda a: pl.BlockSpec(a.shape, lambda i: (0,) * a.ndim)
    small = (lnw.reshape(1, -1), lnb.reshape(1, -1), bd, w_bf16, gain.reshape(1, -1))
    return pl.pallas_call(
        _mix0_out_kernel,
        out_shape=jax.ShapeDtypeStruct((m, D_MODEL), F32),
        grid=(m // tm,),
        in_specs=[row(D_MODEL)] + [row(RWKV_WIDTH)] * 5 + [full(a) for a in small],
        out_specs=row(D_MODEL),
        compiler_params=_params("parallel"),
        name="mix0_out",
    )(x2d, oa, yf, yb, bonus, g, *small)


FFN_CHUNK = 256


def _ffn_kernel(x_ref, pre_ref, post_ref, wg_ref, wu_ref, wd_ref, o_ref):
    x = x_ref[...]
    xn = _rms(x, pre_ref[...], NORM_EPS).astype(BF16)
    acc = jnp.zeros(x.shape, F32)
    for c in range(FFN_HIDDEN // FFN_CHUNK):
        cols = slice(c * FFN_CHUNK, (c + 1) * FFN_CHUNK)
        gate = jnp.dot(xn, wg_ref[:, cols], preferred_element_type=F32)
        up = jnp.dot(xn, wu_ref[:, cols], preferred_element_type=F32)
        h = (gate * _sigmoid(gate) * up).astype(BF16)
        acc = acc + jnp.dot(h, wd_ref[cols, :], preferred_element_type=F32)
    o_ref[...] = x + _rms(acc, post_ref[...], NORM_EPS)


def _ffn(x2d, pre, post, wg, wu, wd, tm=512):
    m = x2d.shape[0]
    resident = lambda shape: pl.BlockSpec(shape, lambda i: (0, 0), pipeline_mode=pl.Buffered(1))
    return pl.pallas_call(
        _ffn_kernel,
        out_shape=jax.ShapeDtypeStruct((m, D_MODEL), F32),
        grid=(m // tm,),
        in_specs=[
            pl.BlockSpec((tm, D_MODEL), lambda i: (i, 0)),
            pl.BlockSpec((1, D_MODEL), lambda i: (0, 0)),
            pl.BlockSpec((1, D_MODEL), lambda i: (0, 0)),
            resident((D_MODEL, FFN_HIDDEN)),
            resident((D_MODEL, FFN_HIDDEN)),
            resident((FFN_HIDDEN, D_MODEL)),
        ],
        out_specs=pl.BlockSpec((tm, D_MODEL), lambda i: (i, 0)),
        compiler_params=_params("parallel"),
        name="ffn",
    )(x2d, pre.reshape(1, -1), post.reshape(1, -1), wg, wu, wd)


def _proj1_kernel(x_ref, g_ref, w_ref, cos_ref, sin_ref, *refs):
    out_refs, scr = refs[:-1], refs[-1]
    tm = x_ref.shape[0]
    xb = _rms(x_ref[...], g_ref[...], NORM_EPS).astype(BF16)
    cos = cos_ref[...]
    sin = sin_ref[...]
    lane = lax.broadcasted_iota(jnp.int32, (1, LANES), 1)
    upper = (lane % HEAD_DIM) >= (HEAD_DIM // 2)
    slot = 0
    for kind in range(3):
        for gi, (_, dil) in enumerate(DIL_PAIRS):
            ref = out_refs[kind * len(DIL_PAIRS) + gi]
            for c in range(DIL_GROUP_WIDTH // LANES):
                col = kind * DIL_WIDTH + gi * DIL_GROUP_WIDTH + c * LANES
                y = jnp.dot(xb, w_ref[:, col:col + LANES], preferred_element_type=F32)
                if kind < 2:
                    y = _rope_tile(y, cos, sin, upper)
                if kind == 0:
                    y = y * (HEAD_DIM ** -0.5)
                if dil == 1:
                    ref[:, c * LANES:(c + 1) * LANES] = y.astype(BF16)
                    continue
                scr[slot] = y
                for rho in range(dil):
                    rows = scr[slot, pl.ds(rho, tm // dil, stride=dil), :]
                    lo = rho * DIL_GROUP_WIDTH + c * LANES
                    ref[:, lo:lo + LANES] = rows.astype(BF16)
                slot += 1


def _norm_proj1(x3d, gain, w_bf16, cos, sin, tm=512):
    b, s, _ = x3d.shape
    n_fold = sum(1 for _, d in DIL_PAIRS if d > 1) * 3 * (DIL_GROUP_WIDTH // LANES)
    out_shape, out_specs = [], []
    for _ in range(3):
        for _, d in DIL_PAIRS:
            out_shape.append(jax.ShapeDtypeStruct((b, s // d, d * DIL_GROUP_WIDTH), BF16))
            out_specs.append(pl.BlockSpec((None, tm // d, d * DIL_GROUP_WIDTH), lambda bi, i: (bi, i, 0)))
    return pl.pallas_call(
        _proj1_kernel,
        out_shape=out_shape,
        grid=(b, s // tm),
        in_specs=[
            pl.BlockSpec((None, tm, D_MODEL), lambda bi, i: (bi, i, 0)),
            pl.BlockSpec((1, D_MODEL), lambda bi, i: (0, 0)),
            pl.BlockSpec(w_bf16.shape, lambda bi, i: (0, 0)),
            pl.BlockSpec((tm, LANES), lambda bi, i: (i, 0)),
            pl.BlockSpec((tm, LANES), lambda bi, i: (i, 0)),
        ],
        out_specs=out_specs,
        scratch_shapes=[pltpu.VMEM((n_fold, tm, LANES), F32)],
        compiler_params=_params("parallel", "parallel"),
        name="norm_proj1",
    )(x3d, gain.reshape(1, -1), w_bf16, cos, sin)


def _band_attn_kernel(q_ref, kp_ref, kc_ref, kn_ref, vp_ref, vc_ref, vn_ref, o_ref, lse_ref, *,
                      length, nsub, nres):
    qb = LANES
    halo = DIL_RADIUS
    wlen = qb + 2 * halo
    l0 = pl.program_id(2) * (nsub * qb)
    m0 = lax.broadcasted_iota(jnp.int32, (1, LANES), 1) < HEAD_DIM
    ti = lax.broadcasted_iota(jnp.int32, (2 * qb, wlen), 0) % qb
    ji = lax.broadcasted_iota(jnp.int32, (2 * qb, wlen), 1)
    band = jnp.abs(ji - halo - ti) <= halo
    jcol = lax.broadcasted_iota(jnp.int32, (1, wlen), 1)

    def window(p_ref, c_ref, n_ref, j, cols):
        lo = j * qb - halo
        parts = []
        if lo < 0:
            parts.append(p_ref[:, cols])
            lo = 0
        hi = min((j + 1) * qb + halo, nsub * qb)
        parts.append(c_ref[lo:hi, cols])
        if (j + 1) * qb + halo > nsub * qb:
            parts.append(n_ref[:, cols])
        return jnp.concatenate(parts, axis=0) if len(parts) > 1 else parts[0]

    jobs = [(r, j, p) for r in range(nres) for j in range(nsub) for p in range(DIL_GROUP_WIDTH // LANES)]
    scores = []
    for r, j, p in jobs:
        cols = slice(r * DIL_GROUP_WIDTH + p * LANES, r * DIL_GROUP_WIDTH + (p + 1) * LANES)
        q_st = _stack(q_ref[j * qb:(j + 1) * qb, cols], m0)
        s = _mm_nt(q_st, window(kp_ref, kc_ref, kn_ref, j, cols))
        kpos0 = l0 + j * qb - halo
        colbias = jnp.where((jcol + kpos0 >= 0) & (jcol + kpos0 < length), 0.0, NEG_INF).astype(F32)
        scores.append(jnp.where(band, s + colbias, NEG_INF))
    stats = []
    for s in scores:
        mx = jnp.max(s, axis=-1, keepdims=True)
        pr = jnp.exp(s - mx)
        stats.append((mx, jnp.sum(pr, axis=-1, keepdims=True), pr.astype(BF16)))
    for (r, j, p), (mx, den, pr) in zip(jobs, stats):
        cols = slice(r * DIL_GROUP_WIDTH + p * LANES, r * DIL_GROUP_WIDTH + (p + 1) * LANES)
        o_st = _mm(pr, window(vp_ref, vc_ref, vn_ref, j, cols)) / den
        lse = mx + jnp.log(den)
        o_ref[j * qb:(j + 1) * qb, cols] = jnp.where(m0, o_st[:qb], o_st[qb:])
        lse_ref[j * qb:(j + 1) * qb, cols] = jnp.where(m0, lse[:qb], lse[qb:])


def _band_attention(q, k, v, dilation):
    b, length, width = q.shape
    nsub = min(4, length // LANES)
    nres = min(dilation, 4 // nsub)
    tq = nsub * LANES
    hb = tq // DIL_RADIUS
    nh = length // DIL_RADIUS
    bw = nres * DIL_GROUP_WIDTH
    cur = lambda bi, r, i: (bi, i, r)
    prev = lambda bi, r, i: (bi, jnp.maximum(i * hb - 1, 0), r)
    nxt = lambda bi, r, i: (bi, jnp.minimum((i + 1) * hb, nh - 1), r)
    main = pl.BlockSpec((None, tq, bw), cur)
    hp = pl.BlockSpec((None, DIL_RADIUS, bw), prev)
    hn = pl.BlockSpec((None, DIL_RADIUS, bw), nxt)
    out_sds = jax.ShapeDtypeStruct((b, length, width), F32)
    return pl.pallas_call(
        functools.partial(_band_attn_kernel, length=length, nsub=nsub, nres=nres),
        out_shape=[out_sds, out_sds],
        grid=(b, dilation // nres, length // tq),
        in_specs=[main, hp, main, hn, hp, main, hn],
        out_specs=[main, main],
        compiler_params=_params("parallel", "parallel", "parallel"),
        name="band_attn",
    )(q, k, k, k, v, v, v)


def _mix1_out_kernel(x_ref, o0_ref, o1_ref, o2_ref, l0_ref, l1_ref, l2_ref, w_ref, gain_ref, out_ref,
                     scr):
    tm = x_ref.shape[0]

    def unfold(ref, dil, slot):
        if dil == 1:
            return ref[...]
        halves = DIL_GROUP_WIDTH // LANES
        for rho in range(dil):
            for c in range(halves):
                lo = rho * DIL_GROUP_WIDTH + c * LANES
                scr[slot * halves + c, pl.ds(rho, tm // dil, stride=dil), :] = ref[:, lo:lo + LANES]
        return jnp.concatenate([scr[slot * halves + c] for c in range(halves)], axis=1)

    dils = [d for _, d in DIL_PAIRS]
    os_, ls, slot = [], [], 0
    for o_ref, l_ref, d in zip((o0_ref, o1_ref, o2_ref), (l0_ref, l1_ref, l2_ref), dils):
        os_.append(unfold(o_ref, d, slot))
        ls.append(unfold(l_ref, d, slot + 1))
        slot += 2 if d > 1 else 0
    mx = jnp.maximum(jnp.maximum(ls[0], ls[1]), ls[2])
    es = [jnp.exp(l - mx) for l in ls]
    den = es[0] + es[1] + es[2]
    m = jnp.zeros((tm, D_MODEL), F32)
    for gi in range(3):
        y = (os_[gi] * (es[gi] / den)).astype(BF16)
        m = m + jnp.dot(y, w_ref[gi * DIL_GROUP_WIDTH:(gi + 1) * DIL_GROUP_WIDTH, :],
                        preferred_element_type=F32)
    out_ref[...] = x_ref[...] + _rms(m, gain_ref[...], NORM_EPS)


def _mix1_out(x3d, outs, lses, w_bf16, gain, tm=256):
    b, s, _ = x3d.shape
    row = pl.BlockSpec((None, tm, D_MODEL), lambda bi, i: (bi, i, 0))
    folded = [pl.BlockSpec((None, tm // d, d * DIL_GROUP_WIDTH), lambda bi, i: (bi, i, 0)) for _, d in DIL_PAIRS]
    n_slots = 2 * sum(1 for _, d in DIL_PAIRS if d > 1)
    return pl.pallas_call(
        _mix1_out_kernel,
        out_shape=jax.ShapeDtypeStruct((b, s, D_MODEL), F32),
        grid=(b, s // tm),
        in_specs=[row] + folded + folded
        + [pl.BlockSpec(w_bf16.shape, lambda bi, i: (0, 0)), pl.BlockSpec((1, D_MODEL), lambda bi, i: (0, 0))],
        out_specs=row,
        scratch_shapes=[pltpu.VMEM((n_slots * DIL_GROUP_WIDTH // LANES, tm, LANES), F32)],
        compiler_params=_params("parallel", "parallel"),
        name="mix1_out",
    )(x3d, *outs, *lses, w_bf16, gain.reshape(1, -1))


def _block_diag2(top, bottom):
    z_tr = jnp.zeros((top.shape[0], bottom.shape[1]), top.dtype)
    z_bl = jnp.zeros((bottom.shape[0], top.shape[1]), top.dtype)
    return jnp.concatenate([jnp.concatenate([top, z_tr], axis=1),
                            jnp.concatenate([z_bl, bottom], axis=1)], axis=0)


def kernel(x_prompt, x_sample, mix_pre0, mix_post0, w_in0, lam_q1, lam_k1, lam_q2, lam_k2, subln_w,
           mu_r, mu_k, mu_v, mu_w, mu_a, mu_g, w0_f, w1_f, w2_f, w0_b, w1_b, w2_b,
           a0_f, a1_f, a2_f, a0_b, a1_b, a2_b, g1, g2, k_k, k_a, r_k, lnx_w, lnx_b, w_out0,
           ffn_pre0, ffn_post0, ffn_gate0, ffn_up0, ffn_down0,
           mix_pre1, mix_post1, w_in1, w_out1, ffn_pre1, ffn_post1, ffn_gate1, ffn_up1, ffn_down1):
    bf = lambda a: a.astype(BF16)
    row = lambda a: a.reshape(1, -1).astype(F32)
    lam_init = 0.8 - 0.6 * math.exp(-0.3 * 0)
    lamq = jnp.stack([lam_q1, lam_q2]).astype(F32)
    lamk = jnp.stack([lam_k1, lam_k2]).astype(F32)
    gate_pad = 2 * LANES - g1.shape[1]
    ones_bd = _block_diag2(jnp.ones((HEAD_DIM, HEAD_DIM), F32), jnp.ones((HEAD_DIM, HEAD_DIM), F32))
    prep_w = (
        jnp.stack([mu_w, mu_a, mu_g]).astype(F32),
        row(jnp.concatenate([mu_r, mu_k, mu_v])),
        bf(jnp.concatenate([w1_f, w1_b], axis=1)),
        bf(_block_diag2(w2_f, w2_b)),
        bf(jnp.concatenate([a1_f, a1_b], axis=1)),
        bf(_block_diag2(a2_f, a2_b)),
        bf(jnp.pad(g1, ((0, 0), (0, gate_pad)))),
        bf(jnp.pad(g2, ((0, gate_pad), (0, 0)))),
        row(jnp.concatenate([w0_f, w0_b])),
        row(jnp.concatenate([a0_f, a0_b])),
        row(k_k), row(k_a), row(r_k.reshape(-1)),
        ones_bd,
    )
    w_in0_b, w_out0_b, w_in1_b, w_out1_b = bf(w_in0), bf(w_out0), bf(w_in1), bf(w_out1)
    ffn0 = (ffn_pre0, ffn_post0, bf(ffn_gate0), bf(ffn_up0), bf(ffn_down0))
    ffn1 = (ffn_pre1, ffn_post1, bf(ffn_gate1), bf(ffn_up1), bf(ffn_down1))

    def run(x):
        b, s, _ = x.shape
        x2d = x.reshape(b * s, D_MODEL)
        cos, sin = _rope_tables(s)
        q, k, v, rkv, xn = _norm_proj(
            x2d, mix_pre0, w_in0_b, cos, sin, s,
            splits=(DIFF_WIDTH, DIFF_WIDTH, DIFF_WIDTH, 3 * RWKV_WIDTH), dtypes=(BF16, BF16, BF16, F32),
            n_rope=2 * DIFF_WIDTH, q_cols=DIFF_WIDTH, q_scale=HEAD_DIM ** -0.5 * LOG2E, emit_xn=True)
        sh = lambda a: a.reshape(b, s, -1)
        out_a = _diff_attention(sh(q), sh(k), sh(v), lamq, lamk, subln_w, lam_init)
        r, vv, kk, lwf, lwb, kf, kb, bfw, bbw, bonus, gate = _rwkv_prep(sh(xn), sh(rkv), prep_w)
        yf, yb = _rwkv_scan(r, vv, kk, lwf, kf, bfw, lwb, kb, bbw)
        fl = lambda a: a.reshape(b * s, -1)
        x1 = _mix0_out(x2d, fl(out_a), fl(yf), fl(yb), fl(bonus), fl(gate), lnx_w, lnx_b, ones_bd,
                       w_out0_b, mix_post0)
        x2 = _ffn(x1, *ffn0)
        x2 = x2.reshape(b, s, D_MODEL)
        qkv1 = _norm_proj1(x2, mix_pre1, w_in1_b, cos, sin)
        outs, lses = [], []
        for gi, (_, dilation) in enumerate(DIL_PAIRS):
            o, lse = _band_attention(qkv1[gi], qkv1[3 + gi], qkv1[6 + gi], dilation)
            outs.append(o)
            lses.append(lse)
        x3 = _mix1_out(x2, outs, lses, w_out1_b, mix_post1)
        x4 = _ffn(x3.reshape(b * s, D_MODEL), *ffn1)
        return x4.reshape(b, s, D_MODEL)

    return (run(x_prompt), run(x_sample))
```

```python
import functools
import math

import jax
import jax.numpy as jnp
from jax import lax
from jax.experimental import pallas as pl
from jax.experimental.pallas import tpu as pltpu

F32 = jnp.float32
BF16 = jnp.bfloat16
HIGHEST = lax.Precision.HIGHEST

D_MODEL = 1024
HEAD_DIM = 64
LANES = 128
MXU_COLS = 256
DIFF_WIDTH = 512
RWKV_WIDTH = 512
N_PAIRS = RWKV_WIDTH // LANES
DIL_PAIRS = ((128, 1), (512, 4), (2048, 16))
DIL_GROUP_WIDTH = 256
DIL_WIDTH = 768
DIL_RADIUS = 64
FFN_HIDDEN = 2816
ROPE_THETA = 10000.0
NORM_EPS = 1e-6
SUBLN_EPS = 1e-5
RWKV_GN_EPS = 64e-5
NEG_INF = -1e30
LOG2E = math.log2(math.e)
CHUNK = 64
VMEM_LIMIT = 56 * 1024 * 1024

NT_DIMS = (((1,), (1,)), ((), ()))
TN_DIMS = (((0,), (0,)), ((), ()))


def _params(*sem):
    return pltpu.CompilerParams(dimension_semantics=sem, vmem_limit_bytes=VMEM_LIMIT)


def _sigmoid(x):
    return 1.0 / (1.0 + jnp.exp(-x))


def _rms(x, gain, eps):
    return x * lax.rsqrt(jnp.mean(x * x, axis=-1, keepdims=True) + eps) * gain


def _rope_tile(x, cos, sin, upper):
    rot = jnp.where(upper, pltpu.roll(x, 32, 1), pltpu.roll(x, 96, 1))
    return x * cos + rot * sin


def _proj_kernel(x_ref, g_ref, w_ref, cos_ref, sin_ref, *out_refs, splits, n_rope, q_cols, q_scale, emit_xn):
    x = x_ref[...]
    xn = _rms(x, g_ref[...], NORM_EPS)
    if emit_xn:
        out_refs[-1][...] = xn
    xb = xn.astype(BF16)
    cos = cos_ref[...]
    sin = sin_ref[...]
    lane = lax.broadcasted_iota(jnp.int32, (1, LANES), 1)
    upper = (lane % HEAD_DIM) >= (HEAD_DIM // 2)
    col = 0
    for ref, width in zip(out_refs, splits):
        for c in range(width // MXU_COLS):
            y2 = jnp.dot(xb, w_ref[:, col:col + MXU_COLS], preferred_element_type=F32)
            for h in range(MXU_COLS // LANES):
                y = y2[:, h * LANES:(h + 1) * LANES]
                if col < n_rope:
                    y = _rope_tile(y, cos, sin, upper)
                if col < q_cols:
                    y = y * q_scale
                lo = c * MXU_COLS + h * LANES
                ref[:, lo:lo + LANES] = y.astype(ref.dtype)
            col += MXU_COLS


def _norm_proj(x2d, gain, w_bf16, cos, sin, seq, splits, dtypes, n_rope, q_cols, q_scale, emit_xn, tm=512):
    m = x2d.shape[0]
    n = w_bf16.shape[1]
    nseq = seq // tm
    out_shape = [jax.ShapeDtypeStruct((m, w), dt) for w, dt in zip(splits, dtypes)]
    out_specs = [pl.BlockSpec((tm, w), lambda i: (i, 0)) for w in splits]
    if emit_xn:
        out_shape.append(jax.ShapeDtypeStruct((m, D_MODEL), F32))
        out_specs.append(pl.BlockSpec((tm, D_MODEL), lambda i: (i, 0)))
    kern = functools.partial(_proj_kernel, splits=splits, n_rope=n_rope, q_cols=q_cols, q_scale=q_scale,
                             emit_xn=emit_xn)
    return pl.pallas_call(
        kern,
        out_shape=out_shape,
        grid=(m // tm,),
        in_specs=[
            pl.BlockSpec((tm, D_MODEL), lambda i: (i, 0)),
            pl.BlockSpec((1, D_MODEL), lambda i: (0, 0)),
            pl.BlockSpec((D_MODEL, n), lambda i: (0, 0)),
            pl.BlockSpec((tm, LANES), lambda i: (i % nseq, 0)),
            pl.BlockSpec((tm, LANES), lambda i: (i % nseq, 0)),
        ],
        out_specs=out_specs,
        compiler_params=_params("parallel"),
        name="norm_proj",
    )(x2d, gain.reshape(1, -1), w_bf16, cos, sin)


def _rope_tables(seq):
    half = HEAD_DIM // 2
    inv = ROPE_THETA ** (-jnp.arange(half, dtype=F32) / half)
    ang = jnp.arange(seq, dtype=F32)[:, None] * inv[None, :]
    cos = jnp.cos(ang)
    sin = jnp.sin(ang)
    cos_t = jnp.tile(jnp.concatenate([cos, cos], axis=-1), (1, LANES // HEAD_DIM))
    sin_t = jnp.tile(jnp.concatenate([-sin, sin], axis=-1), (1, LANES // HEAD_DIM))
    return cos_t, sin_t


def _diff_attn_kernel(lamq_ref, lamk_ref, subln_ref, q_ref, k_ref, v_ref, o_ref,
                      m_ref, l_ref, acc_ref, s_ref, *, seq, tk, lam_init):
    q = q_ref[...]
    tq = q.shape[0]
    lane = lax.broadcasted_iota(jnp.int32, (1, LANES), 1)
    zero = jnp.zeros_like(q)
    qs = (jnp.where(lane < HEAD_DIM, q, zero), jnp.where(lane >= HEAD_DIM, q, zero))
    m_ref[...] = jnp.full(m_ref.shape, -jnp.inf, F32)
    l_ref[...] = jnp.zeros(l_ref.shape, F32)
    acc_ref[...] = jnp.zeros(acc_ref.shape, F32)
    nck = tk // LANES
    nblk = seq // tk
    row_parts = 2

    def scores(j, slot):
        off = pl.multiple_of(j * tk, tk)
        kj = k_ref[pl.ds(off, tk), :]
        for c in range(2):
            s_ref[slot, c] = lax.dot_general(qs[c], kj, NT_DIMS, preferred_element_type=F32)

    def consume(j, slot):
        off = pl.multiple_of(j * tk, tk)
        vj = v_ref[pl.ds(off, tk), :]
        for c in range(2):
            for h in range(row_parts):
                rows = slice(h * tq // row_parts, (h + 1) * tq // row_parts)
                cols = [s_ref[slot, c, rows, i * LANES:(i + 1) * LANES] for i in range(nck)]
                mx = cols[0]
                for col in cols[1:]:
                    mx = jnp.maximum(mx, col)
                m_old = m_ref[c, rows, :]
                m_new = jnp.maximum(m_old, jnp.max(mx, axis=-1, keepdims=True))
                alpha = jnp.exp2(m_old - m_new)
                ps = [jnp.exp2(col - m_new) for col in cols]
                lsum = ps[0]
                for pc in ps[1:]:
                    lsum = lsum + pc
                l_ref[c, rows, :] = alpha * l_ref[c, rows, :] + lsum
                p = jnp.concatenate([pc.astype(BF16) for pc in ps], axis=1)
                acc_ref[c, rows, :] = (alpha * acc_ref[c, rows, :]
                                       + jnp.dot(p, vj, preferred_element_type=F32))
                m_ref[c, rows, :] = m_new

    scores(0, 0)

    def body(i, carry):
        scores(2 * i + 1, 1)
        consume(2 * i, 0)
        scores(2 * i + 2, 0)
        consume(2 * i + 1, 1)
        return carry

    lax.fori_loop(0, nblk // 2 - 1, body, 0)
    scores(nblk - 1, 1)
    consume(nblk - 2, 0)
    consume(nblk - 1, 1)

    e = jnp.exp(jnp.sum(lamq_ref[...] * lamk_ref[...], axis=-1, keepdims=True))
    lam = e[0:1] - e[1:2] + lam_init
    l0 = jnp.sum(l_ref[0], axis=-1, keepdims=True)
    l1 = jnp.sum(l_ref[1], axis=-1, keepdims=True)
    o = acc_ref[0] / l0 - lam * (acc_ref[1] / l1)
    o_ref[...] = _rms(o, subln_ref[...], SUBLN_EPS) * (1.0 - lam_init)


def _diff_attention(q, k, v, lamq, lamk, subln_w, lam_init, tq=512, tk=512):
    b, s, _ = q.shape
    heads = DIFF_WIDTH // LANES
    kern = functools.partial(_diff_attn_kernel, seq=s, tk=tk, lam_init=lam_init)
    return pl.pallas_call(
        kern,
        out_shape=jax.ShapeDtypeStruct((b, s, DIFF_WIDTH), F32),
        grid=(b, heads, s // tq),
        in_specs=[
            pl.BlockSpec((2, HEAD_DIM), lambda bi, h, i: (0, 0)),
            pl.BlockSpec((2, HEAD_DIM), lambda bi, h, i: (0, 0)),
            pl.BlockSpec((1, LANES), lambda bi, h, i: (0, 0)),
            pl.BlockSpec((None, tq, LANES), lambda bi, h, i: (bi, i, h)),
            pl.BlockSpec((None, s, LANES), lambda bi, h, i: (bi, 0, h)),
            pl.BlockSpec((None, s, LANES), lambda bi, h, i: (bi, 0, h)),
        ],
        out_specs=pl.BlockSpec((None, tq, LANES), lambda bi, h, i: (bi, i, h)),
        scratch_shapes=[
            pltpu.VMEM((2, tq, LANES), F32),
            pltpu.VMEM((2, tq, LANES), F32),
            pltpu.VMEM((2, tq, LANES), F32),
            pltpu.VMEM((2, 2, tq, tk), F32),
        ],
        compiler_params=_params("parallel", "parallel", "parallel"),
        name="diff_attn",
    )(lamq, lamk, subln_w.reshape(1, -1), q, k, v)


def _cshift(x, prev_row, next_row):
    t = x.shape[0]
    row = lax.broadcasted_iota(jnp.int32, (t, 1), 0)
    p = jnp.where(row == 0, prev_row, pltpu.roll(x, 1, 0))
    n = jnp.where(row == t - 1, next_row, pltpu.roll(x, t - 1, 0))
    return 0.5 * (p + n)


def _head_sum(x, ones_bd):
    hi = x.astype(BF16)
    lo = (x - hi.astype(F32)).astype(BF16)
    parts = []
    for p in range(x.shape[1] // LANES):
        sl = slice(p * LANES, (p + 1) * LANES)
        parts.append(jnp.dot(hi[:, sl], ones_bd, preferred_element_type=F32)
                     + jnp.dot(lo[:, sl], ones_bd, preferred_element_type=F32))
    return jnp.concatenate(parts, axis=1)


def _rwkv_prep_kernel(xn_ref, xnp_ref, xnn_ref, t_ref, tp_ref, tn_ref,
                      mux_ref, mut_ref, w1_ref, w2_ref, a1_ref, a2_ref, g1_ref, g2_ref,
                      w0_ref, a0_ref, kk_ref, ka_ref, rk_ref, bd_ref,
                      r_out, v_out, kk_out, lwf_out, lwb_out, kf_out, kb_out, bf_out, bb_out,
                      bonus_out, g_out):
    i = pl.program_id(1)
    first = jnp.where(i > 0, 1.0, 0.0).astype(F32)
    last = jnp.where(i < pl.num_programs(1) - 1, 1.0, 0.0).astype(F32)
    xn = xn_ref[...]
    xx = _cshift(xn, xnp_ref[7:8, :] * first, xnn_ref[0:1, :] * last) - xn
    mux = mux_ref[...]
    xw = (xn + xx * mux[0:1]).astype(BF16)
    xa = (xn + xx * mux[1:2]).astype(BF16)
    xg = (xn + xx * mux[2:3]).astype(BF16)

    t = t_ref[...]
    ts = t + (_cshift(t, tp_ref[7:8, :] * first, tn_ref[0:1, :] * last) - t) * mut_ref[...]
    r = ts[:, 0:RWKV_WIDTH]
    k = ts[:, RWKV_WIDTH:2 * RWKV_WIDTH]
    v = ts[:, 2 * RWKV_WIDTH:3 * RWKV_WIDTH]

    hw = jnp.tanh(jnp.dot(xw, w1_ref[...], preferred_element_type=F32))
    dec = jnp.dot(hw.astype(BF16), w2_ref[...], preferred_element_type=F32) + w0_ref[...]
    ha = jnp.dot(xa, a1_ref[...], preferred_element_type=F32)
    rate = _sigmoid(jnp.dot(ha.astype(BF16), a2_ref[...], preferred_element_type=F32) + a0_ref[...])
    hg = _sigmoid(jnp.dot(xg, g1_ref[...], preferred_element_type=F32))
    g_out[...] = jnp.dot(hg.astype(BF16), g2_ref[...], preferred_element_type=F32)

    lw = -math.exp(-0.5) * _sigmoid(dec)
    lwf_out[...] = lw[:, 0:RWKV_WIDTH]
    lwb_out[...] = lw[:, RWKV_WIDTH:]

    bd = bd_ref[...]
    kk = k * kk_ref[...]
    kk = kk / jnp.maximum(jnp.sqrt(_head_sum(kk * kk, bd)), 1e-12)
    a_f = rate[:, 0:RWKV_WIDTH]
    a_b = rate[:, RWKV_WIDTH:]
    ka = ka_ref[...]
    k_f = k * (1.0 + (a_f - 1.0) * ka)
    k_b = k * (1.0 + (a_b - 1.0) * ka)
    r_out[...] = r
    v_out[...] = v
    kk_out[...] = kk
    kf_out[...] = k_f
    kb_out[...] = k_b
    bf_out[...] = kk * a_f
    bb_out[...] = kk * a_b
    bonus_out[...] = _head_sum(r * (0.5 * (k_f + k_b)) * rk_ref[...], bd) * v


def _halo_specs(ts, width, seq):
    nb8 = seq // 8
    r8 = ts // 8
    return [
        pl.BlockSpec((None, ts, width), lambda b, i: (b, i, 0)),
        pl.BlockSpec((None, 8, width), lambda b, i: (b, jnp.maximum(i * r8 - 1, 0), 0)),
        pl.BlockSpec((None, 8, width), lambda b, i: (b, jnp.minimum((i + 1) * r8, nb8 - 1), 0)),
    ]


def _rwkv_prep(xn, rkv, wts, ts=256):
    b, s, _ = xn.shape
    full = lambda a: pl.BlockSpec(a.shape, lambda bi, i: (0,) * a.ndim)
    in_specs = (_halo_specs(ts, D_MODEL, s) + _halo_specs(ts, 3 * RWKV_WIDTH, s)
                + [full(a) for a in wts])
    out_spec = pl.BlockSpec((None, ts, RWKV_WIDTH), lambda bi, i: (bi, i, 0))
    return pl.pallas_call(
        _rwkv_prep_kernel,
        out_shape=[jax.ShapeDtypeStruct((b, s, RWKV_WIDTH), F32)] * 11,
        grid=(b, s // ts),
        in_specs=in_specs,
        out_specs=[out_spec] * 11,
        compiler_params=_params("parallel", "parallel"),
        name="rwkv_prep",
    )(xn, xn, xn, rkv, rkv, rkv, *wts)


def _mm(a, b):
    return jnp.dot(a, b, preferred_element_type=F32)


def _mm_nt(a, b):
    return lax.dot_general(a, b, NT_DIMS, preferred_element_type=F32)


def _mm_tn(a, b):
    return lax.dot_general(a, b, TN_DIMS, preferred_element_type=F32)


def _stack(x, m0):
    zero = jnp.zeros_like(x)
    return jnp.concatenate([jnp.where(m0, x, zero), jnp.where(m0, zero, x)], axis=0)


def _chunk_local(jobs, masks):
    eye, eye_side, same, m0 = masks["eye"], masks["eye_side"], masks["same"], masks["m0"]
    c = CHUNK
    cat0 = lambda *xs: jnp.concatenate(xs, axis=0)
    cat1 = lambda *xs: jnp.concatenate(xs, axis=1)
    st = lambda x: _stack(x, m0)
    ops = []
    for jb in jobs:
        r32 = jb["r"] * jb["p_inc"]
        ops.append(dict(
            a=(-jb["a"] * jb["p_exc"]).astype(BF16), r32=r32, r=r32.astype(BF16),
            b=(jb["b"] * jb["p_inv"]).astype(BF16), k=(jb["k"] * jb["p_inv"]).astype(BF16),
            v=jb["v"].astype(BF16),
            bh=(jb["b"] * jb["e_hat"]).astype(BF16), kh=(jb["k"] * jb["e_hat"]).astype(BF16)))
    gs = [_mm_nt(cat0(o["a"], o["r"]), cat0(st(o["b"]), st(o["k"]))) for o in ops]
    zero = jnp.zeros((c, LANES), F32)
    n_ab, a_ak, a_rb, a_rk = [], [], [], []
    for jb, g in zip(jobs, gs):
        strict, incl = masks["strict"][jb["dir"]], masks["incl"][jb["dir"]]
        n_ab.append(jnp.where(strict, g[:c, :LANES], zero))
        a_ak.append(jnp.where(strict, g[:c, LANES:], zero).astype(BF16))
        a_rb.append(jnp.where(incl, g[c:, :LANES], zero).astype(BF16))
        a_rk.append(jnp.where(incl, g[c:, LANES:], zero).astype(BF16))
    v_st = [st(o["v"]) for o in ops]
    akv = [_mm(m, v) for m, v in zip(a_ak, v_st)]
    minv = [jnp.where(eye_side, 1.0, 0.0).astype(F32) + n for n in n_ab]
    pw = [n.astype(BF16) for n in n_ab]
    pw = [_mm(p, st(p)).astype(BF16) for p in pw]
    for _ in range(int(math.log2(CHUNK)) - 2):
        res = [_mm(cat0(p, m.astype(BF16)), st(p)) for p, m in zip(pw, minv)]
        pw = [r_[:c].astype(BF16) for r_ in res]
        minv = [m + r_[c:] for m, r_ in zip(minv, res)]
    minv = [m + _mm(m.astype(BF16), st(p)) for m, p in zip(minv, pw)]
    xs = [_mm(m.astype(BF16), cat1(st(o["a"]), st(u.astype(BF16)))) for m, o, u in zip(minv, ops, akv)]
    out = []
    zero_b = jnp.zeros((c, LANES), BF16)
    for jb, o, x, rb_, rk_, vs in zip(jobs, ops, xs, a_rb, a_rk, v_st):
        w1 = x[:, :LANES].astype(BF16)
        u_loc = x[:, LANES:].astype(BF16)
        y_loc = _mm(cat1(rb_, rk_), cat0(st(u_loc), vs))
        rw = o["r32"] + _mm(rb_, st(w1))
        pd = _mm_tn(cat0(o["bh"], o["kh"]), cat0(cat1(w1, u_loc), cat1(zero_b, o["v"])))
        phi = jnp.where(eye, jb["p_tot"], 0.0) + jnp.where(same, pd[:, :LANES], 0.0)
        dm = jnp.where(same, pd[:, LANES:], 0.0)
        out.append((rw.astype(BF16), y_loc, phi.astype(BF16), dm))
    return out


def _rwkv_scan_kernel(rf_ref, vf_ref, af_ref, lwf_ref, kf_ref, bf_ref,
                      rb_ref, vb_ref, ab_ref, lwb_ref, kb_ref, bb_ref,
                      yf_ref, yb_ref, state_ref, *, nsub):
    @pl.when(pl.program_id(1) == 0)
    def _():
        state_ref[...] = jnp.zeros(state_ref.shape, F32)

    n2 = 2 * CHUNK
    ri = lax.broadcasted_iota(jnp.int32, (n2, n2), 0)
    ci = lax.broadcasted_iota(jnp.int32, (n2, n2), 1)
    ti = lax.broadcasted_iota(jnp.int32, (CHUNK, CHUNK), 0)
    si = lax.broadcasted_iota(jnp.int32, (CHUNK, CHUNK), 1)
    t_side = lax.broadcasted_iota(jnp.int32, (CHUNK, LANES), 0)
    s_side = lax.broadcasted_iota(jnp.int32, (CHUNK, LANES), 1) % CHUNK
    masks = dict(
        eye=ri == ci,
        same=(ri // CHUNK) == (ci // CHUNK),
        eye_side=s_side == t_side,
        m0=lax.broadcasted_iota(jnp.int32, (1, LANES), 1) < HEAD_DIM,
        strict=(s_side < t_side, s_side > t_side),
        incl=(s_side <= t_side, s_side >= t_side),
    )
    tris = (jnp.where(si <= ti, 1.0, 0.0).astype(F32), jnp.where(si >= ti, 1.0, 0.0).astype(F32))
    dirs = (
        (rf_ref, vf_ref, af_ref, lwf_ref, kf_ref, bf_ref, yf_ref),
        (rb_ref, vb_ref, ab_ref, lwb_ref, kb_ref, bb_ref, yb_ref),
    )
    jobs = []
    for d, (r_ref, v_ref, a_ref, lw_ref, k_ref, b_ref, _) in enumerate(dirs):
        tot_row = CHUNK - 1 if d == 0 else 0
        for sub in range(nsub):
            rows = slice(sub * CHUNK, (sub + 1) * CHUNK)
            lw = lw_ref[rows, :]
            cum = jnp.dot(tris[d], lw, precision=HIGHEST, preferred_element_type=F32)
            tot = cum[tot_row:tot_row + 1, :]
            rowops = dict(r=r_ref[rows, :], v=v_ref[rows, :], a=a_ref[rows, :], k=k_ref[rows, :],
                          b=b_ref[rows, :], p_inc=jnp.exp(cum), p_inv=jnp.exp(-cum),
                          p_exc=jnp.exp(cum - lw), e_hat=jnp.exp(tot - cum), p_tot=jnp.exp(tot))
            for p in range(N_PAIRS):
                sl = slice(p * LANES, (p + 1) * LANES)
                job = {name: val[:, sl] for name, val in rowops.items()}
                job.update(dir=d, sub=sub, pair=p)
                jobs.append(job)
    local = _chunk_local(jobs, masks)
    by_key = {(jb["dir"], jb["sub"], jb["pair"]): loc for jb, loc in zip(jobs, local)}
    states = {(d, p): state_ref[d, p] for d in range(2) for p in range(N_PAIRS)}
    for step in range(nsub):
        for d in range(2):
            sub = step if d == 0 else nsub - 1 - step
            y_ref = dirs[d][-1]
            for p in range(N_PAIRS):
                rw, y_loc, phi, dm = by_key[(d, sub, p)]
                res = _mm(jnp.concatenate([rw, phi], axis=0), states[(d, p)].astype(BF16))
                states[(d, p)] = res[CHUNK:] + dm
                y_ref[sub * CHUNK:(sub + 1) * CHUNK, p * LANES:(p + 1) * LANES] = res[:CHUNK] + y_loc
    for (d, p), t in states.items():
        state_ref[d, p] = t


def _rwkv_scan(r, v, kk, lwf, kf, bf, lwb, kb, bb, nsub=4):
    b, s, _ = r.shape
    tb = nsub * CHUNK
    nb = s // tb
    fwd = pl.BlockSpec((None, tb, RWKV_WIDTH), lambda bi, c: (bi, c, 0))
    bwd = pl.BlockSpec((None, tb, RWKV_WIDTH), lambda bi, c: (bi, nb - 1 - c, 0))
    return pl.pallas_call(
        functools.partial(_rwkv_scan_kernel, nsub=nsub),
        out_shape=[jax.ShapeDtypeStruct((b, s, RWKV_WIDTH), F32)] * 2,
        grid=(b, nb),
        in_specs=[fwd] * 6 + [bwd] * 6,
        out_specs=[fwd, bwd],
        scratch_shapes=[pltpu.VMEM((2, N_PAIRS, LANES, LANES), F32)],
        compiler_params=_params("parallel", "arbitrary"),
        name="rwkv_scan",
    )(r, v, kk, lwf, kf, bf, r, v, kk, lwb, kb, bb)


def _mix0_out_kernel(x_ref, oa_ref, yf_ref, yb_ref, bonus_ref, g_ref, lnw_ref, lnb_ref, bd_ref,
                     w_ref, gain_ref, o_ref):
    y = yf_ref[...] + yb_ref[...]
    bd = bd_ref[...]
    mean = _head_sum(y, bd) * (1.0 / HEAD_DIM)
    yc = y - mean
    var = _head_sum(yc * yc, bd) * (1.0 / HEAD_DIM)
    yn = yc * lax.rsqrt(var + RWKV_GN_EPS) * lnw_ref[...] + lnb_ref[...]
    ob = (yn + bonus_ref[...]) * g_ref[...]
    m = (jnp.dot(oa_ref[...].astype(BF16), w_ref[0:DIFF_WIDTH, :], preferred_element_type=F32)
         + jnp.dot(ob.astype(BF16), w_ref[DIFF_WIDTH:, :], preferred_element_type=F32))
    o_ref[...] = x_ref[...] + _rms(m, gain_ref[...], NORM_EPS)


def _mix0_out(x2d, oa, yf, yb, bonus, g, lnw, lnb, bd, w_bf16, gain, tm=256):
    m = x2d.shape[0]
    row = lambda w: pl.BlockSpec((tm, w), lambda i: (i, 0))
    full = lambda a: pl.BlockSpec(a.shape, lambda i: (0,) * a.ndim)
    small = (lnw.reshape(1, -1), lnb.reshape(1, -1), bd, w_bf16, gain.reshape(1, -1))
    return pl.pallas_call(
        _mix0_out_kernel,
        out_shape=jax.ShapeDtypeStruct((m, D_MODEL), F32),
        grid=(m // tm,),
        in_specs=[row(D_MODEL)] + [row(RWKV_WIDTH)] * 5 + [full(a) for a in small],
        out_specs=row(D_MODEL),
        compiler_params=_params("parallel"),
        name="mix0_out",
    )(x2d, oa, yf, yb, bonus, g, *small)


FFN_CHUNK = 256


def _ffn_kernel(x_ref, pre_ref, post_ref, wg_ref, wu_ref, wd_ref, o_ref):
    x = x_ref[...]
    xn = _rms(x, pre_ref[...], NORM_EPS).astype(BF16)
    acc = jnp.zeros(x.shape, F32)
    for c in range(FFN_HIDDEN // FFN_CHUNK):
        cols = slice(c * FFN_CHUNK, (c + 1) * FFN_CHUNK)
        gate = jnp.dot(xn, wg_ref[:, cols], preferred_element_type=F32)
        up = jnp.dot(xn, wu_ref[:, cols], preferred_element_type=F32)
        h = (gate * _sigmoid(gate) * up).astype(BF16)
        acc = acc + jnp.dot(h, wd_ref[cols, :], preferred_element_type=F32)
    o_ref[...] = x + _rms(acc, post_ref[...], NORM_EPS)


def _ffn(x2d, pre, post, wg, wu, wd, tm=512):
    m = x2d.shape[0]
    resident = lambda shape: pl.BlockSpec(shape, lambda i: (0, 0), pipeline_mode=pl.Buffered(1))
    return pl.pallas_call(
        _ffn_kernel,
        out_shape=jax.ShapeDtypeStruct((m, D_MODEL), F32),
        grid=(m // tm,),
        in_specs=[
            pl.BlockSpec((tm, D_MODEL), lambda i: (i, 0)),
            pl.BlockSpec((1, D_MODEL), lambda i: (0, 0)),
            pl.BlockSpec((1, D_MODEL), lambda i: (0, 0)),
            resident((D_MODEL, FFN_HIDDEN)),
            resident((D_MODEL, FFN_HIDDEN)),
            resident((FFN_HIDDEN, D_MODEL)),
        ],
        out_specs=pl.BlockSpec((tm, D_MODEL), lambda i: (i, 0)),
        compiler_params=_params("parallel"),
        name="ffn",
    )(x2d, pre.reshape(1, -1), post.reshape(1, -1), wg, wu, wd)


def _proj1_kernel(x_ref, g_ref, w_ref, cos_ref, sin_ref, *refs):
    out_refs, scr = refs[:-1], refs[-1]
    tm = x_ref.shape[0]
    xb = _rms(x_ref[...], g_ref[...], NORM_EPS).astype(BF16)
    cos = cos_ref[...]
    sin = sin_ref[...]
    lane = lax.broadcasted_iota(jnp.int32, (1, LANES), 1)
    upper = (lane % HEAD_DIM) >= (HEAD_DIM // 2)
    slot = 0
    for kind in range(3):
        for gi, (_, dil) in enumerate(DIL_PAIRS):
            ref = out_refs[kind * len(DIL_PAIRS) + gi]
            col = kind * DIL_WIDTH + gi * DIL_GROUP_WIDTH
            y2 = jnp.dot(xb, w_ref[:, col:col + DIL_GROUP_WIDTH], preferred_element_type=F32)
            for c in range(DIL_GROUP_WIDTH // LANES):
                y = y2[:, c * LANES:(c + 1) * LANES]
                if kind < 2:
                    y = _rope_tile(y, cos, sin, upper)
                if kind == 0:
                    y = y * (HEAD_DIM ** -0.5)
                if dil == 1:
                    ref[:, c * LANES:(c + 1) * LANES] = y.astype(BF16)
                    continue
                scr[slot] = y
                for rho in range(dil):
                    rows = scr[slot, pl.ds(rho, tm // dil, stride=dil), :]
                    lo = rho * DIL_GROUP_WIDTH + c * LANES
                    ref[:, lo:lo + LANES] = rows.astype(BF16)
                slot += 1


def _norm_proj1(x3d, gain, w_bf16, cos, sin, tm=512):
    b, s, _ = x3d.shape
    n_fold = sum(1 for _, d in DIL_PAIRS if d > 1) * 3 * (DIL_GROUP_WIDTH // LANES)
    out_shape, out_specs = [], []
    for _ in range(3):
        for _, d in DIL_PAIRS:
            out_shape.append(jax.ShapeDtypeStruct((b, s // d, d * DIL_GROUP_WIDTH), BF16))
            out_specs.append(pl.BlockSpec((None, tm // d, d * DIL_GROUP_WIDTH), lambda bi, i: (bi, i, 0)))
    return pl.pallas_call(
        _proj1_kernel,
        out_shape=out_shape,
        grid=(b, s // tm),
        in_specs=[
            pl.BlockSpec((None, tm, D_MODEL), lambda bi, i: (bi, i, 0)),
            pl.BlockSpec((1, D_MODEL), lambda bi, i: (0, 0)),
            pl.BlockSpec(w_bf16.shape, lambda bi, i: (0, 0)),
            pl.BlockSpec((tm, LANES), lambda bi, i: (i, 0)),
            pl.BlockSpec((tm, LANES), lambda bi, i: (i, 0)),
        ],
        out_specs=out_specs,
        scratch_shapes=[pltpu.VMEM((n_fold, tm, LANES), F32)],
        compiler_params=_params("parallel", "parallel"),
        name="norm_proj1",
    )(x3d, gain.reshape(1, -1), w_bf16, cos, sin)


def _band_attn_kernel(q_ref, kp_ref, kc_ref, kn_ref, vp_ref, vc_ref, vn_ref, o_ref, lse_ref, *,
                      length, nsub, nres):
    qb = LANES
    halo = DIL_RADIUS
    wlen = qb + 2 * halo
    l0 = pl.program_id(2) * (nsub * qb)
    m0 = lax.broadcasted_iota(jnp.int32, (1, LANES), 1) < HEAD_DIM
    ti = lax.broadcasted_iota(jnp.int32, (2 * qb, wlen), 0) % qb
    ji = lax.broadcasted_iota(jnp.int32, (2 * qb, wlen), 1)
    band = jnp.abs(ji - halo - ti) <= halo
    jcol = lax.broadcasted_iota(jnp.int32, (1, wlen), 1)

    def window(p_ref, c_ref, n_ref, j, cols):
        lo = j * qb - halo
        parts = []
        if lo < 0:
            parts.append(p_ref[:, cols])
            lo = 0
        hi = min((j + 1) * qb + halo, nsub * qb)
        parts.append(c_ref[lo:hi, cols])
        if (j + 1) * qb + halo > nsub * qb:
            parts.append(n_ref[:, cols])
        return jnp.concatenate(parts, axis=0) if len(parts) > 1 else parts[0]

    jobs = [(r, j, p) for r in range(nres) for j in range(nsub) for p in range(DIL_GROUP_WIDTH // LANES)]
    scores = []
    for r, j, p in jobs:
        cols = slice(r * DIL_GROUP_WIDTH + p * LANES, r * DIL_GROUP_WIDTH + (p + 1) * LANES)
        q_st = _stack(q_ref[j * qb:(j + 1) * qb, cols], m0)
        s = _mm_nt(q_st, window(kp_ref, kc_ref, kn_ref, j, cols))
        kpos0 = l0 + j * qb - halo
        colbias = jnp.where((jcol + kpos0 >= 0) & (jcol + kpos0 < length), 0.0, NEG_INF).astype(F32)
        scores.append(jnp.where(band, s + colbias, NEG_INF))
    stats = []
    for s in scores:
        mx = jnp.max(s, axis=-1, keepdims=True)
        pr = jnp.exp(s - mx)
        stats.append((mx, jnp.sum(pr, axis=-1, keepdims=True), pr.astype(BF16)))
    for (r, j, p), (mx, den, pr) in zip(jobs, stats):
        cols = slice(r * DIL_GROUP_WIDTH + p * LANES, r * DIL_GROUP_WIDTH + (p + 1) * LANES)
        o_st = _mm(pr, window(vp_ref, vc_ref, vn_ref, j, cols)) / den
        lse = mx + jnp.log(den)
        o_ref[j * qb:(j + 1) * qb, cols] = jnp.where(m0, o_st[:qb], o_st[qb:])
        lse_ref[j * qb:(j + 1) * qb, cols] = jnp.where(m0, lse[:qb], lse[qb:])


def _band_attention(q, k, v, dilation):
    b, length, width = q.shape
    nsub = min(4, length // LANES)
    nres = min(dilation, 4 // nsub)
    tq = nsub * LANES
    hb = tq // DIL_RADIUS
    nh = length // DIL_RADIUS
    bw = nres * DIL_GROUP_WIDTH
    cur = lambda bi, r, i: (bi, i, r)
    prev = lambda bi, r, i: (bi, jnp.maximum(i * hb - 1, 0), r)
    nxt = lambda bi, r, i: (bi, jnp.minimum((i + 1) * hb, nh - 1), r)
    main = pl.BlockSpec((None, tq, bw), cur)
    hp = pl.BlockSpec((None, DIL_RADIUS, bw), prev)
    hn = pl.BlockSpec((None, DIL_RADIUS, bw), nxt)
    out_sds = jax.ShapeDtypeStruct((b, length, width), F32)
    return pl.pallas_call(
        functools.partial(_band_attn_kernel, length=length, nsub=nsub, nres=nres),
        out_shape=[out_sds, out_sds],
        grid=(b, dilation // nres, length // tq),
        in_specs=[main, hp, main, hn, hp, main, hn],
        out_specs=[main, main],
        compiler_params=_params("parallel", "parallel", "parallel"),
        name="band_attn",
    )(q, k, k, k, v, v, v)


def _mix1_out_kernel(x_ref, o0_ref, o1_ref, o2_ref, l0_ref, l1_ref, l2_ref, w_ref, gain_ref, out_ref,
                     scr):
    tm = x_ref.shape[0]

    def unfold(ref, dil, slot):
        if dil == 1:
            return ref[...]
        halves = DIL_GROUP_WIDTH // LANES
        for rho in range(dil):
            for c in range(halves):
                lo = rho * DIL_GROUP_WIDTH + c * LANES
                scr[slot * halves + c, pl.ds(rho, tm // dil, stride=dil), :] = ref[:, lo:lo + LANES]
        return jnp.concatenate([scr[slot * halves + c] for c in range(halves)], axis=1)

    dils = [d for _, d in DIL_PAIRS]
    os_, ls, slot = [], [], 0
    for o_ref, l_ref, d in zip((o0_ref, o1_ref, o2_ref), (l0_ref, l1_ref, l2_ref), dils):
        os_.append(unfold(o_ref, d, slot))
        ls.append(unfold(l_ref, d, slot + 1))
        slot += 2 if d > 1 else 0
    mx = jnp.maximum(jnp.maximum(ls[0], ls[1]), ls[2])
    es = [jnp.exp(l - mx) for l in ls]
    den = es[0] + es[1] + es[2]
    m = jnp.zeros((tm, D_MODEL), F32)
    for gi in range(3):
        y = (os_[gi] * (es[gi] / den)).astype(BF16)
        m = m + jnp.dot(y, w_ref[gi * DIL_GROUP_WIDTH:(gi + 1) * DIL_GROUP_WIDTH, :],
                        preferred_element_type=F32)
    out_ref[...] = x_ref[...] + _rms(m, gain_ref[...], NORM_EPS)


def _mix1_out(x3d, outs, lses, w_bf16, gain, tm=256):
    b, s, _ = x3d.shape
    row = pl.BlockSpec((None, tm, D_MODEL), lambda bi, i: (bi, i, 0))
    folded = [pl.BlockSpec((None, tm // d, d * DIL_GROUP_WIDTH), lambda bi, i: (bi, i, 0)) for _, d in DIL_PAIRS]
    n_slots = 2 * sum(1 for _, d in DIL_PAIRS if d > 1)
    return pl.pallas_call(
        _mix1_out_kernel,
        out_shape=jax.ShapeDtypeStruct((b, s, D_MODEL), F32),
        grid=(b, s // tm),
        in_specs=[row] + folded + folded
        + [pl.BlockSpec(w_bf16.shape, lambda bi, i: (0, 0)), pl.BlockSpec((1, D_MODEL), lambda bi, i: (0, 0))],
        out_specs=row,
        scratch_shapes=[pltpu.VMEM((n_slots * DIL_GROUP_WIDTH // LANES, tm, LANES), F32)],
        compiler_params=_params("parallel", "parallel"),
        name="mix1_out",
    )(x3d, *outs, *lses, w_bf16, gain.reshape(1, -1))


def _block_diag2(top, bottom):
    z_tr = jnp.zeros((top.shape[0], bottom.shape[1]), top.dtype)
    z_bl = jnp.zeros((bottom.shape[0], top.shape[1]), top.dtype)
    return jnp.concatenate([jnp.concatenate([top, z_tr], axis=1),
                            jnp.concatenate([z_bl, bottom], axis=1)], axis=0)


def kernel(x_prompt, x_sample, mix_pre0, mix_post0, w_in0, lam_q1, lam_k1, lam_q2, lam_k2, subln_w,
           mu_r, mu_k, mu_v, mu_w, mu_a, mu_g, w0_f, w1_f, w2_f, w0_b, w1_b, w2_b,
           a0_f, a1_f, a2_f, a0_b, a1_b, a2_b, g1, g2, k_k, k_a, r_k, lnx_w, lnx_b, w_out0,
           ffn_pre0, ffn_post0, ffn_gate0, ffn_up0, ffn_down0,
           mix_pre1, mix_post1, w_in1, w_out1, ffn_pre1, ffn_post1, ffn_gate1, ffn_up1, ffn_down1):
    bf = lambda a: a.astype(BF16)
    row = lambda a: a.reshape(1, -1).astype(F32)
    lam_init = 0.8 - 0.6 * math.exp(-0.3 * 0)
    lamq = jnp.stack([lam_q1, lam_q2]).astype(F32)
    lamk = jnp.stack([lam_k1, lam_k2]).astype(F32)
    gate_pad = 2 * LANES - g1.shape[1]
    ones_bd = _block_diag2(jnp.ones((HEAD_DIM, HEAD_DIM), BF16), jnp.ones((HEAD_DIM, HEAD_DIM), BF16))
    prep_w = (
        jnp.stack([mu_w, mu_a, mu_g]).astype(F32),
        row(jnp.concatenate([mu_r, mu_k, mu_v])),
        bf(jnp.concatenate([w1_f, w1_b], axis=1)),
        bf(_block_diag2(w2_f, w2_b)),
        bf(jnp.concatenate([a1_f, a1_b], axis=1)),
        bf(_block_diag2(a2_f, a2_b)),
        bf(jnp.pad(g1, ((0, 0), (0, gate_pad)))),
        bf(jnp.pad(g2, ((0, gate_pad), (0, 0)))),
        row(jnp.concatenate([w0_f, w0_b])),
        row(jnp.concatenate([a0_f, a0_b])),
        row(k_k), row(k_a), row(r_k.reshape(-1)),
        ones_bd,
    )
    w_in0_b, w_out0_b, w_in1_b, w_out1_b = bf(w_in0), bf(w_out0), bf(w_in1), bf(w_out1)
    ffn0 = (ffn_pre0, ffn_post0, bf(ffn_gate0), bf(ffn_up0), bf(ffn_down0))
    ffn1 = (ffn_pre1, ffn_post1, bf(ffn_gate1), bf(ffn_up1), bf(ffn_down1))

    def run(x):
        b, s, _ = x.shape
        x2d = x.reshape(b * s, D_MODEL)
        cos, sin = _rope_tables(s)
        q, k, v, rkv, xn = _norm_proj(
            x2d, mix_pre0, w_in0_b, cos, sin, s,
            splits=(DIFF_WIDTH, DIFF_WIDTH, DIFF_WIDTH, 3 * RWKV_WIDTH), dtypes=(BF16, BF16, BF16, F32),
            n_rope=2 * DIFF_WIDTH, q_cols=DIFF_WIDTH, q_scale=HEAD_DIM ** -0.5 * LOG2E, emit_xn=True)
        sh = lambda a: a.reshape(b, s, -1)
        out_a = _diff_attention(sh(q), sh(k), sh(v), lamq, lamk, subln_w, lam_init)
        r, vv, kk, lwf, lwb, kf, kb, bfw, bbw, bonus, gate = _rwkv_prep(sh(xn), sh(rkv), prep_w)
        yf, yb = _rwkv_scan(r, vv, kk, lwf, kf, bfw, lwb, kb, bbw)
        fl = lambda a: a.reshape(b * s, -1)
        x1 = _mix0_out(x2d, fl(out_a), fl(yf), fl(yb), fl(bonus), fl(gate), lnx_w, lnx_b, ones_bd,
                       w_out0_b, mix_post0)
        x2 = _ffn(x1, *ffn0)
        x2 = x2.reshape(b, s, D_MODEL)
        qkv1 = _norm_proj1(x2, mix_pre1, w_in1_b, cos, sin)
        outs, lses = [], []
        for gi, (_, dilation) in enumerate(DIL_PAIRS):
            o, lse = _band_attention(qkv1[gi], qkv1[3 + gi], qkv1[6 + gi], dilation)
            outs.append(o)
            lses.append(lse)
        x3 = _mix1_out(x2, outs, lses, w_out1_b, mix_post1)
        x4 = _ffn(x3.reshape(b * s, D_MODEL), *ffn1)
        return x4.reshape(b, s, D_MODEL)

    return (run(x_prompt), run(x_sample))
```

```python
import functools
import math

import jax
import jax.numpy as jnp
from jax import lax
from jax.experimental import pallas as pl
from jax.experimental.pallas import tpu as pltpu

F32 = jnp.float32
BF16 = jnp.bfloat16
HIGHEST = lax.Precision.HIGHEST

D_MODEL = 1024
HEAD_DIM = 64
LANES = 128
MXU_COLS = 256
DIFF_WIDTH = 512
RWKV_WIDTH = 512
N_PAIRS = RWKV_WIDTH // LANES
DIL_PAIRS = ((128, 1), (512, 4), (2048, 16))
DIL_GROUP_WIDTH = 256
DIL_WIDTH = 768
DIL_RADIUS = 64
FFN_HIDDEN = 2816
ROPE_THETA = 10000.0
NORM_EPS = 1e-6
SUBLN_EPS = 1e-5
RWKV_GN_EPS = 64e-5
NEG_INF = -1e30
LOG2E = math.log2(math.e)
CHUNK = 64
VMEM_LIMIT = 56 * 1024 * 1024

NT_DIMS = (((1,), (1,)), ((), ()))
TN_DIMS = (((0,), (0,)), ((), ()))


def _params(*sem):
    return pltpu.CompilerParams(dimension_semantics=sem, vmem_limit_bytes=VMEM_LIMIT)


def _sigmoid(x):
    return 1.0 / (1.0 + jnp.exp(-x))


def _rms(x, gain, eps):
    return x * lax.rsqrt(jnp.mean(x * x, axis=-1, keepdims=True) + eps) * gain


def _rope_tile(x, cos, sin, upper):
    rot = jnp.where(upper, pltpu.roll(x, 32, 1), pltpu.roll(x, 96, 1))
    return x * cos + rot * sin


def _rope_tables(seq):
    half = HEAD_DIM // 2
    inv = ROPE_THETA ** (-jnp.arange(half, dtype=F32) / half)
    ang = jnp.arange(seq, dtype=F32)[:, None] * inv[None, :]
    cos = jnp.cos(ang)
    sin = jnp.sin(ang)
    cos_t = jnp.tile(jnp.concatenate([cos, cos], axis=-1), (1, LANES // HEAD_DIM))
    sin_t = jnp.tile(jnp.concatenate([-sin, sin], axis=-1), (1, LANES // HEAD_DIM))
    return cos_t, sin_t


def _diff_attn_kernel(lamq_ref, lamk_ref, subln_ref, q_ref, k_ref, v_ref, o_ref,
                      m_ref, l_ref, acc_ref, s_ref, *, seq, tk, lam_init):
    q = q_ref[...]
    tq = q.shape[0]
    lane = lax.broadcasted_iota(jnp.int32, (1, LANES), 1)
    zero = jnp.zeros_like(q)
    qs = (jnp.where(lane < HEAD_DIM, q, zero), jnp.where(lane >= HEAD_DIM, q, zero))
    m_ref[...] = jnp.full(m_ref.shape, -jnp.inf, F32)
    l_ref[...] = jnp.zeros(l_ref.shape, F32)
    acc_ref[...] = jnp.zeros(acc_ref.shape, F32)
    nck = tk // LANES
    nblk = seq // tk
    row_parts = tq // 256

    def scores(j, slot):
        off = pl.multiple_of(j * tk, tk)
        kj = k_ref[pl.ds(off, tk), :]
        for c in range(2):
            s_ref[slot, c] = lax.dot_general(qs[c], kj, NT_DIMS, preferred_element_type=F32)

    def consume(j, slot):
        off = pl.multiple_of(j * tk, tk)
        vj = v_ref[pl.ds(off, tk), :]
        for c in range(2):
            for h in range(row_parts):
                rows = slice(h * tq // row_parts, (h + 1) * tq // row_parts)
                cols = [s_ref[slot, c, rows, i * LANES:(i + 1) * LANES] for i in range(nck)]
                mx = cols[0]
                for col in cols[1:]:
                    mx = jnp.maximum(mx, col)
                m_old = m_ref[c, rows, :]
                m_new = jnp.maximum(m_old, jnp.max(mx, axis=-1, keepdims=True))
                alpha = jnp.exp2(m_old - m_new)
                ps = [jnp.exp2(col - m_new) for col in cols]
                lsum = ps[0]
                for pc in ps[1:]:
                    lsum = lsum + pc
                l_ref[c, rows, :] = alpha * l_ref[c, rows, :] + lsum
                p = jnp.concatenate([pc.astype(BF16) for pc in ps], axis=1)
                acc_ref[c, rows, :] = (alpha * acc_ref[c, rows, :]
                                       + jnp.dot(p, vj, preferred_element_type=F32))
                m_ref[c, rows, :] = m_new

    scores(0, 0)

    def body(i, carry):
        scores(2 * i + 1, 1)
        consume(2 * i, 0)
        scores(2 * i + 2, 0)
        consume(2 * i + 1, 1)
        return carry

    lax.fori_loop(0, nblk // 2 - 1, body, 0)
    scores(nblk - 1, 1)
    consume(nblk - 2, 0)
    consume(nblk - 1, 1)

    e = jnp.exp(jnp.sum(lamq_ref[...] * lamk_ref[...], axis=-1, keepdims=True))
    lam = e[0:1] - e[1:2] + lam_init
    l0 = jnp.sum(l_ref[0], axis=-1, keepdims=True)
    l1 = jnp.sum(l_ref[1], axis=-1, keepdims=True)
    o = acc_ref[0] / l0 - lam * (acc_ref[1] / l1)
    o_ref[...] = (_rms(o, subln_ref[...], SUBLN_EPS) * (1.0 - lam_init)).astype(o_ref.dtype)


def _diff_attention(q, k, v, lamq, lamk, subln_w, lam_init, tq=1024, tk=512):
    b, s, _ = q.shape
    heads = DIFF_WIDTH // LANES
    kern = functools.partial(_diff_attn_kernel, seq=s, tk=tk, lam_init=lam_init)
    return pl.pallas_call(
        kern,
        out_shape=jax.ShapeDtypeStruct((b, s, DIFF_WIDTH), BF16),
        grid=(b, heads, s // tq),
        in_specs=[
            pl.BlockSpec((2, HEAD_DIM), lambda bi, h, i: (0, 0)),
            pl.BlockSpec((2, HEAD_DIM), lambda bi, h, i: (0, 0)),
            pl.BlockSpec((1, LANES), lambda bi, h, i: (0, 0)),
            pl.BlockSpec((None, tq, LANES), lambda bi, h, i: (bi, i, h)),
            pl.BlockSpec((None, s, LANES), lambda bi, h, i: (bi, 0, h)),
            pl.BlockSpec((None, s, LANES), lambda bi, h, i: (bi, 0, h)),
        ],
        out_specs=pl.BlockSpec((None, tq, LANES), lambda bi, h, i: (bi, i, h)),
        scratch_shapes=[
            pltpu.VMEM((2, tq, LANES), F32),
            pltpu.VMEM((2, tq, LANES), F32),
            pltpu.VMEM((2, tq, LANES), F32),
            pltpu.VMEM((2, 2, tq, tk), F32),
        ],
        compiler_params=_params("parallel", "parallel", "parallel"),
        name="diff_attn",
    )(lamq, lamk, subln_w.reshape(1, -1), q, k, v)


def _cshift(x, prev_row, next_row):
    t = x.shape[0]
    row = lax.broadcasted_iota(jnp.int32, (t, 1), 0)
    p = jnp.where(row == 0, prev_row, pltpu.roll(x, 1, 0))
    n = jnp.where(row == t - 1, next_row, pltpu.roll(x, t - 1, 0))
    return 0.5 * (p + n)


def _head_sum(x, ones_bd):
    hi = x.astype(BF16)
    lo = (x - hi.astype(F32)).astype(BF16)
    parts = []
    for p in range(x.shape[1] // LANES):
        sl = slice(p * LANES, (p + 1) * LANES)
        parts.append(jnp.dot(hi[:, sl], ones_bd, preferred_element_type=F32)
                     + jnp.dot(lo[:, sl], ones_bd, preferred_element_type=F32))
    return jnp.concatenate(parts, axis=1)


def _proj0_prep_kernel(x_ref, xp_ref, xq_ref, gain_ref, w_ref, cos_ref, sin_ref,
                       mux_ref, mut_ref, w1_ref, w2_ref, a1_ref, a2_ref, g1_ref, g2_ref,
                       w0_ref, a0_ref, kk_ref, ka_ref, rk_ref, bd_ref,
                       q_out, k_out, v_out,
                       r_out, rv_out, kk_out, lwf_out, lwb_out, kf_out, kb_out, bf_out, bb_out,
                       bonus_out, g_out):
    i = pl.program_id(1)
    first = jnp.where(i > 0, 1.0, 0.0).astype(F32)
    last = jnp.where(i < pl.num_programs(1) - 1, 1.0, 0.0).astype(F32)
    gain = gain_ref[...]
    xn = _rms(x_ref[...], gain, NORM_EPS)
    xn_p = _rms(xp_ref[...], gain, NORM_EPS) * first
    xn_q = _rms(xq_ref[...], gain, NORM_EPS) * last
    xb = xn.astype(BF16)

    cos = cos_ref[...]
    sin = sin_ref[...]
    lane = lax.broadcasted_iota(jnp.int32, (1, LANES), 1)
    upper = (lane % HEAD_DIM) >= (HEAD_DIM // 2)
    for kind, ref in enumerate((q_out, k_out, v_out)):
        for c in range(DIFF_WIDTH // MXU_COLS):
            col = kind * DIFF_WIDTH + c * MXU_COLS
            y2 = jnp.dot(xb, w_ref[:, col:col + MXU_COLS], preferred_element_type=F32)
            for h in range(MXU_COLS // LANES):
                y = y2[:, h * LANES:(h + 1) * LANES]
                if kind < 2:
                    y = _rope_tile(y, cos, sin, upper)
                if kind == 0:
                    y = y * (HEAD_DIM ** -0.5 * LOG2E)
                lo = c * MXU_COLS + h * LANES
                ref[:, lo:lo + LANES] = y.astype(BF16)

    rkv0 = 3 * DIFF_WIDTH
    rows = x_ref.shape[0]
    xb_ext = jnp.concatenate([xb, xn_p.astype(BF16), xn_q.astype(BF16)], axis=0)
    t_ext = jnp.concatenate(
        [jnp.dot(xb_ext, w_ref[:, rkv0 + c * MXU_COLS:rkv0 + (c + 1) * MXU_COLS], preferred_element_type=F32)
         for c in range(3 * RWKV_WIDTH // MXU_COLS)], axis=1)
    t = t_ext[:rows]
    t_p = t_ext[rows:rows + 8]
    t_q = t_ext[rows + 8:]

    xx = _cshift(xn, xn_p[7:8, :], xn_q[0:1, :]) - xn
    mux = mux_ref[...]
    xw = (xn + xx * mux[0:1]).astype(BF16)
    xa = (xn + xx * mux[1:2]).astype(BF16)
    xg = (xn + xx * mux[2:3]).astype(BF16)

    ts = t + (_cshift(t, t_p[7:8, :], t_q[0:1, :]) - t) * mut_ref[...]
    r = ts[:, 0:RWKV_WIDTH]
    k = ts[:, RWKV_WIDTH:2 * RWKV_WIDTH]
    v = ts[:, 2 * RWKV_WIDTH:3 * RWKV_WIDTH]

    hw = jnp.tanh(jnp.dot(xw, w1_ref[...], preferred_element_type=F32))
    dec = jnp.dot(hw.astype(BF16), w2_ref[...], preferred_element_type=F32) + w0_ref[...]
    ha = jnp.dot(xa, a1_ref[...], preferred_element_type=F32)
    rate = _sigmoid(jnp.dot(ha.astype(BF16), a2_ref[...], preferred_element_type=F32) + a0_ref[...])
    hg = _sigmoid(jnp.dot(xg, g1_ref[...], preferred_element_type=F32))
    g_out[...] = jnp.dot(hg.astype(BF16), g2_ref[...], preferred_element_type=F32).astype(g_out.dtype)

    lw = -math.exp(-0.5) * _sigmoid(dec)
    lwf_out[...] = lw[:, 0:RWKV_WIDTH]
    lwb_out[...] = lw[:, RWKV_WIDTH:]

    bd = bd_ref[...]
    kk = k * kk_ref[...]
    kk = kk / jnp.maximum(jnp.sqrt(_head_sum(kk * kk, bd)), 1e-12)
    a_f = rate[:, 0:RWKV_WIDTH]
    a_b = rate[:, RWKV_WIDTH:]
    ka = ka_ref[...]
    k_f = k * (1.0 + (a_f - 1.0) * ka)
    k_b = k * (1.0 + (a_b - 1.0) * ka)
    store = lambda ref, val: ref.__setitem__(Ellipsis, val.astype(ref.dtype))
    store(r_out, r)
    store(rv_out, v)
    store(kk_out, kk)
    store(kf_out, k_f)
    store(kb_out, k_b)
    store(bf_out, kk * a_f)
    store(bb_out, kk * a_b)
    store(bonus_out, _head_sum(r * (0.5 * (k_f + k_b)) * rk_ref[...], bd) * v)


def _proj0_prep(x3d, gain, w_bf16, cos, sin, wts, ts=512):
    b, s, _ = x3d.shape
    nb8 = s // 8
    r8 = ts // 8
    full = lambda a: pl.BlockSpec(a.shape, lambda bi, i: (0,) * a.ndim)
    resident = lambda a: pl.BlockSpec(a.shape, lambda bi, i: (0,) * a.ndim, pipeline_mode=pl.Buffered(1))
    in_specs = [
        pl.BlockSpec((None, ts, D_MODEL), lambda bi, i: (bi, i, 0)),
        pl.BlockSpec((None, 8, D_MODEL), lambda bi, i: (bi, jnp.maximum(i * r8 - 1, 0), 0)),
        pl.BlockSpec((None, 8, D_MODEL), lambda bi, i: (bi, jnp.minimum((i + 1) * r8, nb8 - 1), 0)),
        pl.BlockSpec((1, D_MODEL), lambda bi, i: (0, 0)),
        resident(w_bf16),
        pl.BlockSpec((ts, LANES), lambda bi, i: (i, 0)),
        pl.BlockSpec((ts, LANES), lambda bi, i: (i, 0)),
    ] + [full(a) for a in wts]
    row = pl.BlockSpec((None, ts, RWKV_WIDTH), lambda bi, i: (bi, i, 0))
    dtypes = [BF16] * 3 + [BF16, BF16, BF16, F32, F32, BF16, BF16, BF16, BF16, BF16, BF16]
    return pl.pallas_call(
        _proj0_prep_kernel,
        out_shape=[jax.ShapeDtypeStruct((b, s, RWKV_WIDTH), dt) for dt in dtypes],
        grid=(b, s // ts),
        in_specs=in_specs,
        out_specs=[row] * len(dtypes),
        compiler_params=_params("parallel", "parallel"),
        name="proj0_prep",
    )(x3d, x3d, x3d, gain.reshape(1, -1), w_bf16, cos, sin, *wts)


def _mm(a, b):
    return jnp.dot(a, b, preferred_element_type=F32)


def _mm_nt(a, b):
    return lax.dot_general(a, b, NT_DIMS, preferred_element_type=F32)


def _mm_tn(a, b):
    return lax.dot_general(a, b, TN_DIMS, preferred_element_type=F32)


def _stack(x, m0):
    zero = jnp.zeros_like(x)
    return jnp.concatenate([jnp.where(m0, x, zero), jnp.where(m0, zero, x)], axis=0)


def _chunk_local(jobs, masks):
    eye, eye_side, same, m0 = masks["eye"], masks["eye_side"], masks["same"], masks["m0"]
    c = CHUNK
    cat0 = lambda *xs: jnp.concatenate(xs, axis=0)
    cat1 = lambda *xs: jnp.concatenate(xs, axis=1)
    st = lambda x: _stack(x, m0)
    ops = []
    for jb in jobs:
        r32 = jb["r"] * jb["p_inc"]
        ops.append(dict(
            a=(-jb["a"] * jb["p_exc"]).astype(BF16), r32=r32, r=r32.astype(BF16),
            b=(jb["b"] * jb["p_inv"]).astype(BF16), k=(jb["k"] * jb["p_inv"]).astype(BF16),
            v=jb["v"].astype(BF16),
            bh=(jb["b"] * jb["e_hat"]).astype(BF16), kh=(jb["k"] * jb["e_hat"]).astype(BF16)))
    gs = [_mm_nt(cat0(o["a"], o["r"]), cat0(st(o["b"]), st(o["k"]))) for o in ops]
    zero = jnp.zeros((c, LANES), F32)
    n_ab, a_ak, a_rb, a_rk = [], [], [], []
    for jb, g in zip(jobs, gs):
        strict, incl = masks["strict"][jb["dir"]], masks["incl"][jb["dir"]]
        n_ab.append(jnp.where(strict, g[:c, :LANES], zero))
        a_ak.append(jnp.where(strict, g[:c, LANES:], zero).astype(BF16))
        a_rb.append(jnp.where(incl, g[c:, :LANES], zero).astype(BF16))
        a_rk.append(jnp.where(incl, g[c:, LANES:], zero).astype(BF16))
    v_st = [st(o["v"]) for o in ops]
    akv = [_mm(m, v) for m, v in zip(a_ak, v_st)]
    minv = [jnp.where(eye_side, 1.0, 0.0).astype(F32) + n for n in n_ab]
    pw = [n.astype(BF16) for n in n_ab]
    pw = [_mm(p, st(p)).astype(BF16) for p in pw]
    for _ in range(int(math.log2(CHUNK)) - 2):
        res = [_mm(cat0(p, m.astype(BF16)), st(p)) for p, m in zip(pw, minv)]
        pw = [r_[:c].astype(BF16) for r_ in res]
        minv = [m + r_[c:] for m, r_ in zip(minv, res)]
    minv = [m + _mm(m.astype(BF16), st(p)) for m, p in zip(minv, pw)]
    xs = [_mm(m.astype(BF16), cat1(st(o["a"]), st(u.astype(BF16)))) for m, o, u in zip(minv, ops, akv)]
    out = []
    zero_b = jnp.zeros((c, LANES), BF16)
    for jb, o, x, rb_, rk_, vs in zip(jobs, ops, xs, a_rb, a_rk, v_st):
        w1 = x[:, :LANES].astype(BF16)
        u_loc = x[:, LANES:].astype(BF16)
        y_loc = _mm(cat1(rb_, rk_), cat0(st(u_loc), vs))
        rw = o["r32"] + _mm(rb_, st(w1))
        pd = _mm_tn(cat0(o["bh"], o["kh"]), cat0(cat1(w1, u_loc), cat1(zero_b, o["v"])))
        phi = jnp.where(eye, jb["p_tot"], 0.0) + jnp.where(same, pd[:, :LANES], 0.0)
        dm = jnp.where(same, pd[:, LANES:], 0.0)
        out.append((rw.astype(BF16), y_loc, phi.astype(BF16), dm))
    return out


def _rwkv_scan_kernel(rf_ref, vf_ref, af_ref, lwf_ref, kf_ref, bf_ref,
                      rb_ref, vb_ref, ab_ref, lwb_ref, kb_ref, bb_ref,
                      yf_ref, yb_ref, state_ref, *, nsub):
    @pl.when(pl.program_id(1) == 0)
    def _():
        state_ref[...] = jnp.zeros(state_ref.shape, F32)

    n2 = 2 * CHUNK
    ri = lax.broadcasted_iota(jnp.int32, (n2, n2), 0)
    ci = lax.broadcasted_iota(jnp.int32, (n2, n2), 1)
    ti = lax.broadcasted_iota(jnp.int32, (CHUNK, CHUNK), 0)
    si = lax.broadcasted_iota(jnp.int32, (CHUNK, CHUNK), 1)
    t_side = lax.broadcasted_iota(jnp.int32, (CHUNK, LANES), 0)
    s_side = lax.broadcasted_iota(jnp.int32, (CHUNK, LANES), 1) % CHUNK
    masks = dict(
        eye=ri == ci,
        same=(ri // CHUNK) == (ci // CHUNK),
        eye_side=s_side == t_side,
        m0=lax.broadcasted_iota(jnp.int32, (1, LANES), 1) < HEAD_DIM,
        strict=(s_side < t_side, s_side > t_side),
        incl=(s_side <= t_side, s_side >= t_side),
    )
    tris = (jnp.where(si <= ti, 1.0, 0.0).astype(F32), jnp.where(si >= ti, 1.0, 0.0).astype(F32))
    dirs = (
        (rf_ref, vf_ref, af_ref, lwf_ref, kf_ref, bf_ref, yf_ref),
        (rb_ref, vb_ref, ab_ref, lwb_ref, kb_ref, bb_ref, yb_ref),
    )
    jobs = []
    for d, (r_ref, v_ref, a_ref, lw_ref, k_ref, b_ref, _) in enumerate(dirs):
        tot_row = CHUNK - 1 if d == 0 else 0
        for sub in range(nsub):
            rows = slice(sub * CHUNK, (sub + 1) * CHUNK)
            lw = lw_ref[rows, :]
            cum = jnp.dot(tris[d], lw, precision=HIGHEST, preferred_element_type=F32)
            tot = cum[tot_row:tot_row + 1, :]
            rowops = dict(r=r_ref[rows, :], v=v_ref[rows, :], a=a_ref[rows, :], k=k_ref[rows, :],
                          b=b_ref[rows, :], p_inc=jnp.exp(cum), p_inv=jnp.exp(-cum),
                          p_exc=jnp.exp(cum - lw), e_hat=jnp.exp(tot - cum), p_tot=jnp.exp(tot))
            for p in range(N_PAIRS):
                sl = slice(p * LANES, (p + 1) * LANES)
                job = {name: val[:, sl] for name, val in rowops.items()}
                job.update(dir=d, sub=sub, pair=p)
                jobs.append(job)
    local = _chunk_local(jobs, masks)
    by_key = {(jb["dir"], jb["sub"], jb["pair"]): loc for jb, loc in zip(jobs, local)}
    states = {(d, p): state_ref[d, p] for d in range(2) for p in range(N_PAIRS)}
    for step in range(nsub):
        for d in range(2):
            sub = step if d == 0 else nsub - 1 - step
            y_ref = dirs[d][-1]
            for p in range(N_PAIRS):
                rw, y_loc, phi, dm = by_key[(d, sub, p)]
                res = _mm(jnp.concatenate([rw, phi], axis=0), states[(d, p)].astype(BF16))
                states[(d, p)] = res[CHUNK:] + dm
                y_ref[sub * CHUNK:(sub + 1) * CHUNK, p * LANES:(p + 1) * LANES] = res[:CHUNK] + y_loc
    for (d, p), t in states.items():
        state_ref[d, p] = t


def _rwkv_scan(r, v, kk, lwf, kf, bf, lwb, kb, bb, nsub=4):
    b, s, _ = r.shape
    tb = nsub * CHUNK
    nb = s // tb
    fwd = pl.BlockSpec((None, tb, RWKV_WIDTH), lambda bi, c: (bi, c, 0))
    bwd = pl.BlockSpec((None, tb, RWKV_WIDTH), lambda bi, c: (bi, nb - 1 - c, 0))
    return pl.pallas_call(
        functools.partial(_rwkv_scan_kernel, nsub=nsub),
        out_shape=[jax.ShapeDtypeStruct((b, s, RWKV_WIDTH), F32)] * 2,
        grid=(b, nb),
        in_specs=[fwd] * 6 + [bwd] * 6,
        out_specs=[fwd, bwd],
        scratch_shapes=[pltpu.VMEM((2, N_PAIRS, LANES, LANES), F32)],
        compiler_params=_params("parallel", "arbitrary"),
        name="rwkv_scan",
    )(r, v, kk, lwf, kf, bf, r, v, kk, lwb, kb, bb)


def _mix0_out_kernel(x_ref, oa_ref, yf_ref, yb_ref, bonus_ref, g_ref, lnw_ref, lnb_ref, bd_ref,
                     w_ref, gain_ref, o_ref):
    y = yf_ref[...] + yb_ref[...]
    bd = bd_ref[...]
    mean = _head_sum(y, bd) * (1.0 / HEAD_DIM)
    yc = y - mean
    var = _head_sum(yc * yc, bd) * (1.0 / HEAD_DIM)
    yn = yc * lax.rsqrt(var + RWKV_GN_EPS) * lnw_ref[...] + lnb_ref[...]
    ob = (yn + bonus_ref[...]) * g_ref[...]
    m = (jnp.dot(oa_ref[...].astype(BF16), w_ref[0:DIFF_WIDTH, :], preferred_element_type=F32)
         + jnp.dot(ob.astype(BF16), w_ref[DIFF_WIDTH:, :], preferred_element_type=F32))
    o_ref[...] = x_ref[...] + _rms(m, gain_ref[...], NORM_EPS)


def _mix0_out(x2d, oa, yf, yb, bonus, g, lnw, lnb, bd, w_bf16, gain, tm=256):
    m = x2d.shape[0]
    row = lambda w: pl.BlockSpec((tm, w), lambda i: (i, 0))
    full = lambda a: pl.BlockSpec(a.shape, lambda i: (0,) * a.ndim)
    small = (lnw.reshape(1, -1), lnb.reshape(1, -1), bd, w_bf16, gain.reshape(1, -1))
    return pl.pallas_call(
        _mix0_out_kernel,
        out_shape=jax.ShapeDtypeStruct((m, D_MODEL), F32),
        grid=(m // tm,),
        in_specs=[row(D_MODEL)] + [row(RWKV_WIDTH)] * 5 + [full(a) for a in small],
        out_specs=row(D_MODEL),
        compiler_params=_params("parallel"),
        name="mix0_out",
    )(x2d, oa, yf, yb, bonus, g, *small)


FFN_CHUNK = 256


def _ffn_kernel(x_ref, pre_ref, post_ref, wg_ref, wu_ref, wd_ref, o_ref):
    x = x_ref[...]
    xn = _rms(x, pre_ref[...], NORM_EPS).astype(BF16)
    acc = jnp.zeros(x.shape, F32)
    for c in range(FFN_HIDDEN // FFN_CHUNK):
        cols = slice(c * FFN_CHUNK, (c + 1) * FFN_CHUNK)
        gate = jnp.dot(xn, wg_ref[:, cols], preferred_element_type=F32)
        up = jnp.dot(xn, wu_ref[:, cols], preferred_element_type=F32)
        h = (gate * _sigmoid(gate) * up).astype(BF16)
        acc = acc + jnp.dot(h, wd_ref[cols, :], preferred_element_type=F32)
    o_ref[...] = x + _rms(acc, post_ref[...], NORM_EPS)


def _ffn(x2d, pre, post, wg, wu, wd, tm=512):
    m = x2d.shape[0]
    resident = lambda shape: pl.BlockSpec(shape, lambda i: (0, 0), pipeline_mode=pl.Buffered(1))
    return pl.pallas_call(
        _ffn_kernel,
        out_shape=jax.ShapeDtypeStruct((m, D_MODEL), F32),
        grid=(m // tm,),
        in_specs=[
            pl.BlockSpec((tm, D_MODEL), lambda i: (i, 0)),
            pl.BlockSpec((1, D_MODEL), lambda i: (0, 0)),
            pl.BlockSpec((1, D_MODEL), lambda i: (0, 0)),
            resident((D_MODEL, FFN_HIDDEN)),
            resident((D_MODEL, FFN_HIDDEN)),
            resident((FFN_HIDDEN, D_MODEL)),
        ],
        out_specs=pl.BlockSpec((tm, D_MODEL), lambda i: (i, 0)),
        compiler_params=_params("parallel"),
        name="ffn",
    )(x2d, pre.reshape(1, -1), post.reshape(1, -1), wg, wu, wd)


def _proj1_kernel(x_ref, g_ref, w_ref, cos_ref, sin_ref, *refs):
    out_refs, scr = refs[:-1], refs[-1]
    tm = x_ref.shape[0]
    xb = _rms(x_ref[...], g_ref[...], NORM_EPS).astype(BF16)
    cos = cos_ref[...]
    sin = sin_ref[...]
    lane = lax.broadcasted_iota(jnp.int32, (1, LANES), 1)
    upper = (lane % HEAD_DIM) >= (HEAD_DIM // 2)
    slot = 0
    for kind in range(3):
        for gi, (_, dil) in enumerate(DIL_PAIRS):
            ref = out_refs[kind * len(DIL_PAIRS) + gi]
            col = kind * DIL_WIDTH + gi * DIL_GROUP_WIDTH
            y2 = jnp.dot(xb, w_ref[:, col:col + DIL_GROUP_WIDTH], preferred_element_type=F32)
            for c in range(DIL_GROUP_WIDTH // LANES):
                y = y2[:, c * LANES:(c + 1) * LANES]
                if kind < 2:
                    y = _rope_tile(y, cos, sin, upper)
                if kind == 0:
                    y = y * (HEAD_DIM ** -0.5)
                if dil == 1:
                    ref[:, c * LANES:(c + 1) * LANES] = y.astype(BF16)
                    continue
                scr[slot] = y
                for rho in range(dil):
                    rows = scr[slot, pl.ds(rho, tm // dil, stride=dil), :]
                    lo = rho * DIL_GROUP_WIDTH + c * LANES
                    ref[:, lo:lo + LANES] = rows.astype(BF16)
                slot += 1


def _norm_proj1(x3d, gain, w_bf16, cos, sin, tm=512):
    b, s, _ = x3d.shape
    n_fold = sum(1 for _, d in DIL_PAIRS if d > 1) * 3 * (DIL_GROUP_WIDTH // LANES)
    out_shape, out_specs = [], []
    for _ in range(3):
        for _, d in DIL_PAIRS:
            out_shape.append(jax.ShapeDtypeStruct((b, s // d, d * DIL_GROUP_WIDTH), BF16))
            out_specs.append(pl.BlockSpec((None, tm // d, d * DIL_GROUP_WIDTH), lambda bi, i: (bi, i, 0)))
    return pl.pallas_call(
        _proj1_kernel,
        out_shape=out_shape,
        grid=(b, s // tm),
        in_specs=[
            pl.BlockSpec((None, tm, D_MODEL), lambda bi, i: (bi, i, 0)),
            pl.BlockSpec((1, D_MODEL), lambda bi, i: (0, 0)),
            pl.BlockSpec(w_bf16.shape, lambda bi, i: (0, 0)),
            pl.BlockSpec((tm, LANES), lambda bi, i: (i, 0)),
            pl.BlockSpec((tm, LANES), lambda bi, i: (i, 0)),
        ],
        out_specs=out_specs,
        scratch_shapes=[pltpu.VMEM((n_fold, tm, LANES), F32)],
        compiler_params=_params("parallel", "parallel"),
        name="norm_proj1",
    )(x3d, gain.reshape(1, -1), w_bf16, cos, sin)


def _band_attn_kernel(q_ref, kp_ref, kc_ref, kn_ref, vp_ref, vc_ref, vn_ref, o_ref, lse_ref, *,
                      length, nsub, nres):
    qb = LANES
    halo = DIL_RADIUS
    wlen = qb + 2 * halo
    l0 = pl.program_id(2) * (nsub * qb)
    m0 = lax.broadcasted_iota(jnp.int32, (1, LANES), 1) < HEAD_DIM
    ti = lax.broadcasted_iota(jnp.int32, (2 * qb, wlen), 0) % qb
    ji = lax.broadcasted_iota(jnp.int32, (2 * qb, wlen), 1)
    band = jnp.abs(ji - halo - ti) <= halo
    jcol = lax.broadcasted_iota(jnp.int32, (1, wlen), 1)

    def window(p_ref, c_ref, n_ref, j, cols):
        lo = j * qb - halo
        parts = []
        if lo < 0:
            parts.append(p_ref[:, cols])
            lo = 0
        hi = min((j + 1) * qb + halo, nsub * qb)
        parts.append(c_ref[lo:hi, cols])
        if (j + 1) * qb + halo > nsub * qb:
            parts.append(n_ref[:, cols])
        return jnp.concatenate(parts, axis=0) if len(parts) > 1 else parts[0]

    jobs = [(r, j, p) for r in range(nres) for j in range(nsub) for p in range(DIL_GROUP_WIDTH // LANES)]
    scores = []
    for r, j, p in jobs:
        cols = slice(r * DIL_GROUP_WIDTH + p * LANES, r * DIL_GROUP_WIDTH + (p + 1) * LANES)
        q_st = _stack(q_ref[j * qb:(j + 1) * qb, cols], m0)
        s = _mm_nt(q_st, window(kp_ref, kc_ref, kn_ref, j, cols))
        kpos0 = l0 + j * qb - halo
        colbias = jnp.where((jcol + kpos0 >= 0) & (jcol + kpos0 < length), 0.0, NEG_INF).astype(F32)
        scores.append(jnp.where(band, s + colbias, NEG_INF))
    stats = []
    for s in scores:
        mx = jnp.max(s, axis=-1, keepdims=True)
        pr = jnp.exp(s - mx)
        stats.append((mx, jnp.sum(pr, axis=-1, keepdims=True), pr.astype(BF16)))
    for (r, j, p), (mx, den, pr) in zip(jobs, stats):
        cols = slice(r * DIL_GROUP_WIDTH + p * LANES, r * DIL_GROUP_WIDTH + (p + 1) * LANES)
        o_st = _mm(pr, window(vp_ref, vc_ref, vn_ref, j, cols)) / den
        lse = mx + jnp.log(den)
        o_ref[j * qb:(j + 1) * qb, cols] = jnp.where(m0, o_st[:qb], o_st[qb:])
        lse_ref[j * qb:(j + 1) * qb, cols] = jnp.where(m0, lse[:qb], lse[qb:])


def _band_attention(q, k, v, dilation):
    b, length, width = q.shape
    nsub = min(4, length // LANES)
    nres = min(dilation, 4 // nsub)
    tq = nsub * LANES
    hb = tq // DIL_RADIUS
    nh = length // DIL_RADIUS
    bw = nres * DIL_GROUP_WIDTH
    cur = lambda bi, r, i: (bi, i, r)
    prev = lambda bi, r, i: (bi, jnp.maximum(i * hb - 1, 0), r)
    nxt = lambda bi, r, i: (bi, jnp.minimum((i + 1) * hb, nh - 1), r)
    main = pl.BlockSpec((None, tq, bw), cur)
    hp = pl.BlockSpec((None, DIL_RADIUS, bw), prev)
    hn = pl.BlockSpec((None, DIL_RADIUS, bw), nxt)
    out_sds = jax.ShapeDtypeStruct((b, length, width), F32)
    return pl.pallas_call(
        functools.partial(_band_attn_kernel, length=length, nsub=nsub, nres=nres),
        out_shape=[out_sds, out_sds],
        grid=(b, dilation // nres, length // tq),
        in_specs=[main, hp, main, hn, hp, main, hn],
        out_specs=[main, main],
        compiler_params=_params("parallel", "parallel", "parallel"),
        name="band_attn",
    )(q, k, k, k, v, v, v)


def _mix1_out_kernel(x_ref, o0_ref, o1_ref, o2_ref, l0_ref, l1_ref, l2_ref, w_ref, gain_ref, out_ref,
                     scr):
    tm = x_ref.shape[0]

    def unfold(ref, dil, slot):
        if dil == 1:
            return ref[...]
        halves = DIL_GROUP_WIDTH // LANES
        for rho in range(dil):
            for c in range(halves):
                lo = rho * DIL_GROUP_WIDTH + c * LANES
                scr[slot * halves + c, pl.ds(rho, tm // dil, stride=dil), :] = ref[:, lo:lo + LANES]
        return jnp.concatenate([scr[slot * halves + c] for c in range(halves)], axis=1)

    dils = [d for _, d in DIL_PAIRS]
    os_, ls, slot = [], [], 0
    for o_ref, l_ref, d in zip((o0_ref, o1_ref, o2_ref), (l0_ref, l1_ref, l2_ref), dils):
        os_.append(unfold(o_ref, d, slot))
        ls.append(unfold(l_ref, d, slot + 1))
        slot += 2 if d > 1 else 0
    mx = jnp.maximum(jnp.maximum(ls[0], ls[1]), ls[2])
    es = [jnp.exp(l - mx) for l in ls]
    den = es[0] + es[1] + es[2]
    m = jnp.zeros((tm, D_MODEL), F32)
    for gi in range(3):
        y = (os_[gi] * (es[gi] / den)).astype(BF16)
        m = m + jnp.dot(y, w_ref[gi * DIL_GROUP_WIDTH:(gi + 1) * DIL_GROUP_WIDTH, :],
                        preferred_element_type=F32)
    out_ref[...] = x_ref[...] + _rms(m, gain_ref[...], NORM_EPS)


def _mix1_out(x3d, outs, lses, w_bf16, gain, tm=256):
    b, s, _ = x3d.shape
    row = pl.BlockSpec((None, tm, D_MODEL), lambda bi, i: (bi, i, 0))
    folded = [pl.BlockSpec((None, tm // d, d * DIL_GROUP_WIDTH), lambda bi, i: (bi, i, 0)) for _, d in DIL_PAIRS]
    n_slots = 2 * sum(1 for _, d in DIL_PAIRS if d > 1)
    return pl.pallas_call(
        _mix1_out_kernel,
        out_shape=jax.ShapeDtypeStruct((b, s, D_MODEL), F32),
        grid=(b, s // tm),
        in_specs=[row] + folded + folded
        + [pl.BlockSpec(w_bf16.shape, lambda bi, i: (0, 0)), pl.BlockSpec((1, D_MODEL), lambda bi, i: (0, 0))],
        out_specs=row,
        scratch_shapes=[pltpu.VMEM((n_slots * DIL_GROUP_WIDTH // LANES, tm, LANES), F32)],
        compiler_params=_params("parallel", "parallel"),
        name="mix1_out",
    )(x3d, *outs, *lses, w_bf16, gain.reshape(1, -1))


def _block_diag2(top, bottom):
    z_tr = jnp.zeros((top.shape[0], bottom.shape[1]), top.dtype)
    z_bl = jnp.zeros((bottom.shape[0], top.shape[1]), top.dtype)
    return jnp.concatenate([jnp.concatenate([top, z_tr], axis=1),
                            jnp.concatenate([z_bl, bottom], axis=1)], axis=0)


def kernel(x_prompt, x_sample, mix_pre0, mix_post0, w_in0, lam_q1, lam_k1, lam_q2, lam_k2, subln_w,
           mu_r, mu_k, mu_v, mu_w, mu_a, mu_g, w0_f, w1_f, w2_f, w0_b, w1_b, w2_b,
           a0_f, a1_f, a2_f, a0_b, a1_b, a2_b, g1, g2, k_k, k_a, r_k, lnx_w, lnx_b, w_out0,
           ffn_pre0, ffn_post0, ffn_gate0, ffn_up0, ffn_down0,
           mix_pre1, mix_post1, w_in1, w_out1, ffn_pre1, ffn_post1, ffn_gate1, ffn_up1, ffn_down1):
    bf = lambda a: a.astype(BF16)
    row = lambda a: a.reshape(1, -1).astype(F32)
    lam_init = 0.8 - 0.6 * math.exp(-0.3 * 0)
    lamq = jnp.stack([lam_q1, lam_q2]).astype(F32)
    lamk = jnp.stack([lam_k1, lam_k2]).astype(F32)
    gate_pad = 2 * LANES - g1.shape[1]
    ones_bd = _block_diag2(jnp.ones((HEAD_DIM, HEAD_DIM), BF16), jnp.ones((HEAD_DIM, HEAD_DIM), BF16))
    prep_w = (
        jnp.stack([mu_w, mu_a, mu_g]).astype(F32),
        row(jnp.concatenate([mu_r, mu_k, mu_v])),
        bf(jnp.concatenate([w1_f, w1_b], axis=1)),
        bf(_block_diag2(w2_f, w2_b)),
        bf(jnp.concatenate([a1_f, a1_b], axis=1)),
        bf(_block_diag2(a2_f, a2_b)),
        bf(jnp.pad(g1, ((0, 0), (0, gate_pad)))),
        bf(jnp.pad(g2, ((0, gate_pad), (0, 0)))),
        row(jnp.concatenate([w0_f, w0_b])),
        row(jnp.concatenate([a0_f, a0_b])),
        row(k_k), row(k_a), row(r_k.reshape(-1)),
        ones_bd,
    )
    w_in0_b, w_out0_b, w_in1_b, w_out1_b = bf(w_in0), bf(w_out0), bf(w_in1), bf(w_out1)
    ffn0 = (ffn_pre0, ffn_post0, bf(ffn_gate0), bf(ffn_up0), bf(ffn_down0))
    ffn1 = (ffn_pre1, ffn_post1, bf(ffn_gate1), bf(ffn_up1), bf(ffn_down1))

    def run(x):
        b, s, _ = x.shape
        x2d = x.reshape(b * s, D_MODEL)
        cos, sin = _rope_tables(s)
        (q, k, v, r, vv, kk, lwf, lwb, kf, kb, bfw, bbw, bonus, gate) = _proj0_prep(
            x, mix_pre0, w_in0_b, cos, sin, prep_w)
        out_a = _diff_attention(q, k, v, lamq, lamk, subln_w, lam_init)
        yf, yb = _rwkv_scan(r, vv, kk, lwf, kf, bfw, lwb, kb, bbw)
        fl = lambda a: a.reshape(b * s, -1)
        x1 = _mix0_out(x2d, fl(out_a), fl(yf), fl(yb), fl(bonus), fl(gate), lnx_w, lnx_b, ones_bd,
                       w_out0_b, mix_post0)
        x2 = _ffn(x1, *ffn0)
        x2 = x2.reshape(b, s, D_MODEL)
        qkv1 = _norm_proj1(x2, mix_pre1, w_in1_b, cos, sin)
        outs, lses = [], []
        for gi, (_, dilation) in enumerate(DIL_PAIRS):
            o, lse = _band_attention(qkv1[gi], qkv1[3 + gi], qkv1[6 + gi], dilation)
            outs.append(o)
            lses.append(lse)
        x3 = _mix1_out(x2, outs, lses, w_out1_b, mix_post1)
        x4 = _ffn(x3.reshape(b * s, D_MODEL), *ffn1)
        return x4.reshape(b, s, D_MODEL)

    return (run(x_prompt), run(x_sample))
```

```python
import functools
import math

import jax
import jax.numpy as jnp
from jax import lax
from jax.experimental import pallas as pl
from jax.experimental.pallas import tpu as pltpu

F32 = jnp.float32
BF16 = jnp.bfloat16
HIGHEST = lax.Precision.HIGHEST

D_MODEL = 1024
HEAD_DIM = 64
LANES = 128
MXU_COLS = 256
DIFF_WIDTH = 512
RWKV_WIDTH = 512
N_PAIRS = RWKV_WIDTH // LANES
DIL_PAIRS = ((128, 1), (512, 4), (2048, 16))
DIL_GROUP_WIDTH = 256
DIL_WIDTH = 768
DIL_RADIUS = 64
FFN_HIDDEN = 2816
ROPE_THETA = 10000.0
NORM_EPS = 1e-6
SUBLN_EPS = 1e-5
RWKV_GN_EPS = 64e-5
NEG_INF = -1e30
LOG2E = math.log2(math.e)
CHUNK = 64
VMEM_LIMIT = 56 * 1024 * 1024

NT_DIMS = (((1,), (1,)), ((), ()))
TN_DIMS = (((0,), (0,)), ((), ()))


def _params(*sem):
    return pltpu.CompilerParams(dimension_semantics=sem, vmem_limit_bytes=VMEM_LIMIT)


def _sigmoid(x):
    return 1.0 / (1.0 + jnp.exp(-x))


def _rms(x, gain, eps):
    return x * lax.rsqrt(jnp.mean(x * x, axis=-1, keepdims=True) + eps) * gain


def _rope_tile(x, cos, sin, upper):
    rot = jnp.where(upper, pltpu.roll(x, 32, 1), pltpu.roll(x, 96, 1))
    return x * cos + rot * sin


def _rope_tables(seq):
    half = HEAD_DIM // 2
    inv = ROPE_THETA ** (-jnp.arange(half, dtype=F32) / half)
    ang = jnp.arange(seq, dtype=F32)[:, None] * inv[None, :]
    cos = jnp.cos(ang)
    sin = jnp.sin(ang)
    cos_t = jnp.tile(jnp.concatenate([cos, cos], axis=-1), (1, LANES // HEAD_DIM))
    sin_t = jnp.tile(jnp.concatenate([-sin, sin], axis=-1), (1, LANES // HEAD_DIM))
    return cos_t, sin_t


def _diff_attn_kernel(lamq_ref, lamk_ref, subln_ref, q_ref, k_ref, v_ref, o_ref,
                      m_ref, l_ref, acc_ref, s_ref, *, seq, tk, lam_init):
    q = q_ref[...]
    tq = q.shape[0]
    lane = lax.broadcasted_iota(jnp.int32, (1, LANES), 1)
    zero = jnp.zeros_like(q)
    qs = (jnp.where(lane < HEAD_DIM, q, zero), jnp.where(lane >= HEAD_DIM, q, zero))
    m_ref[...] = jnp.full(m_ref.shape, -jnp.inf, F32)
    l_ref[...] = jnp.zeros(l_ref.shape, F32)
    acc_ref[...] = jnp.zeros(acc_ref.shape, F32)
    nck = tk // LANES
    nblk = seq // tk
    row_parts = tq // 256

    def scores(j, slot):
        off = pl.multiple_of(j * tk, tk)
        kj = k_ref[pl.ds(off, tk), :]
        for c in range(2):
            s_ref[slot, c] = lax.dot_general(qs[c], kj, NT_DIMS, preferred_element_type=F32)

    def consume(j, slot):
        off = pl.multiple_of(j * tk, tk)
        vj = v_ref[pl.ds(off, tk), :]
        for c in range(2):
            for h in range(row_parts):
                rows = slice(h * tq // row_parts, (h + 1) * tq // row_parts)
                cols = [s_ref[slot, c, rows, i * LANES:(i + 1) * LANES] for i in range(nck)]
                mx = cols[0]
                for col in cols[1:]:
                    mx = jnp.maximum(mx, col)
                m_old = m_ref[c, rows, :]
                m_new = jnp.maximum(m_old, jnp.max(mx, axis=-1, keepdims=True))
                alpha = jnp.exp2(m_old - m_new)
                ps = [jnp.exp2(col - m_new) for col in cols]
                lsum = ps[0]
                for pc in ps[1:]:
                    lsum = lsum + pc
                l_ref[c, rows, :] = alpha * l_ref[c, rows, :] + lsum
                p = jnp.concatenate([pc.astype(BF16) for pc in ps], axis=1)
                acc_ref[c, rows, :] = (alpha * acc_ref[c, rows, :]
                                       + jnp.dot(p, vj, preferred_element_type=F32))
                m_ref[c, rows, :] = m_new

    scores(0, 0)

    def body(i, carry):
        scores(2 * i + 1, 1)
        consume(2 * i, 0)
        scores(2 * i + 2, 0)
        consume(2 * i + 1, 1)
        return carry

    lax.fori_loop(0, nblk // 2 - 1, body, 0)
    scores(nblk - 1, 1)
    consume(nblk - 2, 0)
    consume(nblk - 1, 1)

    e = jnp.exp(jnp.sum(lamq_ref[...] * lamk_ref[...], axis=-1, keepdims=True))
    lam = e[0:1] - e[1:2] + lam_init
    l0 = jnp.sum(l_ref[0], axis=-1, keepdims=True)
    l1 = jnp.sum(l_ref[1], axis=-1, keepdims=True)
    o = acc_ref[0] / l0 - lam * (acc_ref[1] / l1)
    o_ref[...] = (_rms(o, subln_ref[...], SUBLN_EPS) * (1.0 - lam_init)).astype(o_ref.dtype)


def _diff_attention(q, k, v, lamq, lamk, subln_w, lam_init, tq=1024, tk=512):
    b, s, _ = q.shape
    heads = DIFF_WIDTH // LANES
    kern = functools.partial(_diff_attn_kernel, seq=s, tk=tk, lam_init=lam_init)
    return pl.pallas_call(
        kern,
        out_shape=jax.ShapeDtypeStruct((b, s, DIFF_WIDTH), BF16),
        grid=(b, heads, s // tq),
        in_specs=[
            pl.BlockSpec((2, HEAD_DIM), lambda bi, h, i: (0, 0)),
            pl.BlockSpec((2, HEAD_DIM), lambda bi, h, i: (0, 0)),
            pl.BlockSpec((1, LANES), lambda bi, h, i: (0, 0)),
            pl.BlockSpec((None, tq, LANES), lambda bi, h, i: (bi, i, h)),
            pl.BlockSpec((None, s, LANES), lambda bi, h, i: (bi, 0, h)),
            pl.BlockSpec((None, s, LANES), lambda bi, h, i: (bi, 0, h)),
        ],
        out_specs=pl.BlockSpec((None, tq, LANES), lambda bi, h, i: (bi, i, h)),
        scratch_shapes=[
            pltpu.VMEM((2, tq, LANES), F32),
            pltpu.VMEM((2, tq, LANES), F32),
            pltpu.VMEM((2, tq, LANES), F32),
            pltpu.VMEM((2, 2, tq, tk), F32),
        ],
        compiler_params=_params("parallel", "parallel", "parallel"),
        name="diff_attn",
    )(lamq, lamk, subln_w.reshape(1, -1), q, k, v)


def _cshift(x, prev_row, next_row):
    t = x.shape[0]
    row = lax.broadcasted_iota(jnp.int32, (t, 1), 0)
    p = jnp.where(row == 0, prev_row, pltpu.roll(x, 1, 0))
    n = jnp.where(row == t - 1, next_row, pltpu.roll(x, t - 1, 0))
    return 0.5 * (p + n)


def _head_sum(x, ones_bd):
    hi = x.astype(BF16)
    lo = (x - hi.astype(F32)).astype(BF16)
    parts = []
    for p in range(x.shape[1] // LANES):
        sl = slice(p * LANES, (p + 1) * LANES)
        parts.append(jnp.dot(hi[:, sl], ones_bd, preferred_element_type=F32)
                     + jnp.dot(lo[:, sl], ones_bd, preferred_element_type=F32))
    return jnp.concatenate(parts, axis=1)


def _proj0_prep_kernel(x_ref, xp_ref, xq_ref, gain_ref, w_ref, cos_ref, sin_ref,
                       mux_ref, mut_ref, w1_ref, w2_ref, a1_ref, a2_ref, g1_ref, g2_ref,
                       w0_ref, a0_ref, kk_ref, ka_ref, rk_ref, bd_ref,
                       q_out, k_out, v_out,
                       r_out, rv_out, kk_out, lwf_out, lwb_out, kf_out, kb_out, bf_out, bb_out,
                       bonus_out, g_out):
    i = pl.program_id(1)
    first = jnp.where(i > 0, 1.0, 0.0).astype(F32)
    last = jnp.where(i < pl.num_programs(1) - 1, 1.0, 0.0).astype(F32)
    gain = gain_ref[...]
    xn = _rms(x_ref[...], gain, NORM_EPS)
    xn_p = _rms(xp_ref[...], gain, NORM_EPS) * first
    xn_q = _rms(xq_ref[...], gain, NORM_EPS) * last
    xb = xn.astype(BF16)

    cos = cos_ref[...]
    sin = sin_ref[...]
    lane = lax.broadcasted_iota(jnp.int32, (1, LANES), 1)
    upper = (lane % HEAD_DIM) >= (HEAD_DIM // 2)
    for kind, ref in enumerate((q_out, k_out, v_out)):
        for c in range(DIFF_WIDTH // MXU_COLS):
            col = kind * DIFF_WIDTH + c * MXU_COLS
            y2 = jnp.dot(xb, w_ref[:, col:col + MXU_COLS], preferred_element_type=F32)
            for h in range(MXU_COLS // LANES):
                y = y2[:, h * LANES:(h + 1) * LANES]
                if kind < 2:
                    y = _rope_tile(y, cos, sin, upper)
                if kind == 0:
                    y = y * (HEAD_DIM ** -0.5 * LOG2E)
                lo = c * MXU_COLS + h * LANES
                ref[:, lo:lo + LANES] = y.astype(BF16)

    rkv0 = 3 * DIFF_WIDTH
    rows = x_ref.shape[0]
    xb_ext = jnp.concatenate([xb, xn_p.astype(BF16), xn_q.astype(BF16)], axis=0)
    t_ext = jnp.concatenate(
        [jnp.dot(xb_ext, w_ref[:, rkv0 + c * MXU_COLS:rkv0 + (c + 1) * MXU_COLS], preferred_element_type=F32)
         for c in range(3 * RWKV_WIDTH // MXU_COLS)], axis=1)
    t = t_ext[:rows]
    t_p = t_ext[rows:rows + 8]
    t_q = t_ext[rows + 8:]

    xx = _cshift(xn, xn_p[7:8, :], xn_q[0:1, :]) - xn
    mux = mux_ref[...]
    xw = (xn + xx * mux[0:1]).astype(BF16)
    xa = (xn + xx * mux[1:2]).astype(BF16)
    xg = (xn + xx * mux[2:3]).astype(BF16)

    ts = t + (_cshift(t, t_p[7:8, :], t_q[0:1, :]) - t) * mut_ref[...]
    r = ts[:, 0:RWKV_WIDTH]
    k = ts[:, RWKV_WIDTH:2 * RWKV_WIDTH]
    v = ts[:, 2 * RWKV_WIDTH:3 * RWKV_WIDTH]

    hw = jnp.tanh(jnp.dot(xw, w1_ref[...], preferred_element_type=F32))
    dec = jnp.dot(hw.astype(BF16), w2_ref[...], preferred_element_type=F32) + w0_ref[...]
    ha = jnp.dot(xa, a1_ref[...], preferred_element_type=F32)
    rate = _sigmoid(jnp.dot(ha.astype(BF16), a2_ref[...], preferred_element_type=F32) + a0_ref[...])
    hg = _sigmoid(jnp.dot(xg, g1_ref[...], preferred_element_type=F32))
    g_out[...] = jnp.dot(hg.astype(BF16), g2_ref[...], preferred_element_type=F32).astype(g_out.dtype)

    lw = -math.exp(-0.5) * _sigmoid(dec)
    lwf_out[...] = lw[:, 0:RWKV_WIDTH]
    lwb_out[...] = lw[:, RWKV_WIDTH:]

    bd = bd_ref[...]
    kk = k * kk_ref[...]
    kk = kk / jnp.maximum(jnp.sqrt(_head_sum(kk * kk, bd)), 1e-12)
    a_f = rate[:, 0:RWKV_WIDTH]
    a_b = rate[:, RWKV_WIDTH:]
    ka = ka_ref[...]
    k_f = k * (1.0 + (a_f - 1.0) * ka)
    k_b = k * (1.0 + (a_b - 1.0) * ka)
    store = lambda ref, val: ref.__setitem__(Ellipsis, val.astype(ref.dtype))
    store(r_out, r)
    store(rv_out, v)
    store(kk_out, kk)
    store(kf_out, k_f)
    store(kb_out, k_b)
    store(bf_out, kk * a_f)
    store(bb_out, kk * a_b)
    store(bonus_out, _head_sum(r * (0.5 * (k_f + k_b)) * rk_ref[...], bd) * v)


def _proj0_prep(x3d, gain, w_bf16, cos, sin, wts, ts=512):
    b, s, _ = x3d.shape
    nb8 = s // 8
    r8 = ts // 8
    full = lambda a: pl.BlockSpec(a.shape, lambda bi, i: (0,) * a.ndim)
    resident = lambda a: pl.BlockSpec(a.shape, lambda bi, i: (0,) * a.ndim, pipeline_mode=pl.Buffered(1))
    in_specs = [
        pl.BlockSpec((None, ts, D_MODEL), lambda bi, i: (bi, i, 0)),
        pl.BlockSpec((None, 8, D_MODEL), lambda bi, i: (bi, jnp.maximum(i * r8 - 1, 0), 0)),
        pl.BlockSpec((None, 8, D_MODEL), lambda bi, i: (bi, jnp.minimum((i + 1) * r8, nb8 - 1), 0)),
        pl.BlockSpec((1, D_MODEL), lambda bi, i: (0, 0)),
        resident(w_bf16),
        pl.BlockSpec((ts, LANES), lambda bi, i: (i, 0)),
        pl.BlockSpec((ts, LANES), lambda bi, i: (i, 0)),
    ] + [full(a) for a in wts]
    row = pl.BlockSpec((None, ts, RWKV_WIDTH), lambda bi, i: (bi, i, 0))
    dtypes = [BF16] * 3 + [BF16, BF16, BF16, F32, F32, BF16, BF16, BF16, BF16, BF16, BF16]
    return pl.pallas_call(
        _proj0_prep_kernel,
        out_shape=[jax.ShapeDtypeStruct((b, s, RWKV_WIDTH), dt) for dt in dtypes],
        grid=(b, s // ts),
        in_specs=in_specs,
        out_specs=[row] * len(dtypes),
        compiler_params=_params("parallel", "parallel"),
        name="proj0_prep",
    )(x3d, x3d, x3d, gain.reshape(1, -1), w_bf16, cos, sin, *wts)


def _mm(a, b):
    return jnp.dot(a, b, preferred_element_type=F32)


def _mm_nt(a, b):
    return lax.dot_general(a, b, NT_DIMS, preferred_element_type=F32)


def _mm_tn(a, b):
    return lax.dot_general(a, b, TN_DIMS, preferred_element_type=F32)


def _stack(x, m0):
    zero = jnp.zeros_like(x)
    return jnp.concatenate([jnp.where(m0, x, zero), jnp.where(m0, zero, x)], axis=0)


def _chunk_local(jobs, masks):
    eye, eye_side, same, m0 = masks["eye"], masks["eye_side"], masks["same"], masks["m0"]
    c = CHUNK
    cat0 = lambda *xs: jnp.concatenate(xs, axis=0)
    cat1 = lambda *xs: jnp.concatenate(xs, axis=1)
    st = lambda x: _stack(x, m0)
    ops = []
    for jb in jobs:
        r32 = jb["r"] * jb["p_inc"]
        ops.append(dict(
            a=(-jb["a"] * jb["p_exc"]).astype(BF16), r32=r32, r=r32.astype(BF16),
            b=(jb["b"] * jb["p_inv"]).astype(BF16), k=(jb["k"] * jb["p_inv"]).astype(BF16),
            v=jb["v"].astype(BF16),
            bh=(jb["b"] * jb["e_hat"]).astype(BF16), kh=(jb["k"] * jb["e_hat"]).astype(BF16)))
    gs = [_mm_nt(cat0(o["a"], o["r"]), cat0(st(o["b"]), st(o["k"]))) for o in ops]
    zero = jnp.zeros((c, LANES), F32)
    n_ab, a_ak, a_rb, a_rk = [], [], [], []
    for jb, g in zip(jobs, gs):
        strict, incl = masks["strict"][jb["dir"]], masks["incl"][jb["dir"]]
        n_ab.append(jnp.where(strict, g[:c, :LANES], zero))
        a_ak.append(jnp.where(strict, g[:c, LANES:], zero).astype(BF16))
        a_rb.append(jnp.where(incl, g[c:, :LANES], zero).astype(BF16))
        a_rk.append(jnp.where(incl, g[c:, LANES:], zero).astype(BF16))
    v_st = [st(o["v"]) for o in ops]
    akv = [_mm(m, v) for m, v in zip(a_ak, v_st)]
    minv = [jnp.where(eye_side, 1.0, 0.0).astype(F32) + n for n in n_ab]
    pw = [n.astype(BF16) for n in n_ab]
    pw = [_mm(p, st(p)).astype(BF16) for p in pw]
    for _ in range(int(math.log2(CHUNK)) - 2):
        res = [_mm(cat0(p, m.astype(BF16)), st(p)) for p, m in zip(pw, minv)]
        pw = [r_[:c].astype(BF16) for r_ in res]
        minv = [m + r_[c:] for m, r_ in zip(minv, res)]
    minv = [m + _mm(m.astype(BF16), st(p)) for m, p in zip(minv, pw)]
    xs = [_mm(m.astype(BF16), cat1(st(o["a"]), st(u.astype(BF16)))) for m, o, u in zip(minv, ops, akv)]
    out = []
    zero_b = jnp.zeros((c, LANES), BF16)
    for jb, o, x, rb_, rk_, vs in zip(jobs, ops, xs, a_rb, a_rk, v_st):
        w1 = x[:, :LANES].astype(BF16)
        u_loc = x[:, LANES:].astype(BF16)
        y_loc = _mm(cat1(rb_, rk_), cat0(st(u_loc), vs))
        rw = o["r32"] + _mm(rb_, st(w1))
        pd = _mm_tn(cat0(o["bh"], o["kh"]), cat0(cat1(w1, u_loc), cat1(zero_b, o["v"])))
        phi = jnp.where(eye, jb["p_tot"], 0.0) + jnp.where(same, pd[:, :LANES], 0.0)
        dm = jnp.where(same, pd[:, LANES:], 0.0)
        out.append((rw.astype(BF16), y_loc, phi.astype(BF16), dm))
    return out


def _rwkv_scan_kernel(rf_ref, vf_ref, af_ref, lwf_ref, kf_ref, bf_ref,
                      rb_ref, vb_ref, ab_ref, lwb_ref, kb_ref, bb_ref,
                      yf_ref, yb_ref, state_ref, *, nsub):
    @pl.when(pl.program_id(1) == 0)
    def _():
        state_ref[...] = jnp.zeros(state_ref.shape, F32)

    n2 = 2 * CHUNK
    ri = lax.broadcasted_iota(jnp.int32, (n2, n2), 0)
    ci = lax.broadcasted_iota(jnp.int32, (n2, n2), 1)
    ti = lax.broadcasted_iota(jnp.int32, (CHUNK, CHUNK), 0)
    si = lax.broadcasted_iota(jnp.int32, (CHUNK, CHUNK), 1)
    t_side = lax.broadcasted_iota(jnp.int32, (CHUNK, LANES), 0)
    s_side = lax.broadcasted_iota(jnp.int32, (CHUNK, LANES), 1) % CHUNK
    masks = dict(
        eye=ri == ci,
        same=(ri // CHUNK) == (ci // CHUNK),
        eye_side=s_side == t_side,
        m0=lax.broadcasted_iota(jnp.int32, (1, LANES), 1) < HEAD_DIM,
        strict=(s_side < t_side, s_side > t_side),
        incl=(s_side <= t_side, s_side >= t_side),
    )
    tris = (jnp.where(si <= ti, 1.0, 0.0).astype(F32), jnp.where(si >= ti, 1.0, 0.0).astype(F32))
    dirs = (
        (rf_ref, vf_ref, af_ref, lwf_ref, kf_ref, bf_ref, yf_ref),
        (rb_ref, vb_ref, ab_ref, lwb_ref, kb_ref, bb_ref, yb_ref),
    )
    jobs = []
    for d, (r_ref, v_ref, a_ref, lw_ref, k_ref, b_ref, _) in enumerate(dirs):
        tot_row = CHUNK - 1 if d == 0 else 0
        for sub in range(nsub):
            rows = slice(sub * CHUNK, (sub + 1) * CHUNK)
            lw = lw_ref[rows, :]
            cum = jnp.dot(tris[d], lw, precision=HIGHEST, preferred_element_type=F32)
            tot = cum[tot_row:tot_row + 1, :]
            rowops = dict(r=r_ref[rows, :], v=v_ref[rows, :], a=a_ref[rows, :], k=k_ref[rows, :],
                          b=b_ref[rows, :], p_inc=jnp.exp(cum), p_inv=jnp.exp(-cum),
                          p_exc=jnp.exp(cum - lw), e_hat=jnp.exp(tot - cum), p_tot=jnp.exp(tot))
            for p in range(N_PAIRS):
                sl = slice(p * LANES, (p + 1) * LANES)
                job = {name: val[:, sl] for name, val in rowops.items()}
                job.update(dir=d, sub=sub, pair=p)
                jobs.append(job)
    local = _chunk_local(jobs, masks)
    by_key = {(jb["dir"], jb["sub"], jb["pair"]): loc for jb, loc in zip(jobs, local)}
    states = {(d, p): state_ref[d, p] for d in range(2) for p in range(N_PAIRS)}
    for step in range(nsub):
        for d in range(2):
            sub = step if d == 0 else nsub - 1 - step
            y_ref = dirs[d][-1]
            for p in range(N_PAIRS):
                rw, y_loc, phi, dm = by_key[(d, sub, p)]
                res = _mm(jnp.concatenate([rw, phi], axis=0), states[(d, p)].astype(BF16))
                states[(d, p)] = res[CHUNK:] + dm
                y_ref[sub * CHUNK:(sub + 1) * CHUNK, p * LANES:(p + 1) * LANES] = res[:CHUNK] + y_loc
    for (d, p), t in states.items():
        state_ref[d, p] = t


def _rwkv_scan(r, v, kk, lwf, kf, bf, lwb, kb, bb, nsub=4):
    b, s, _ = r.shape
    tb = nsub * CHUNK
    nb = s // tb
    fwd = pl.BlockSpec((None, tb, RWKV_WIDTH), lambda bi, c: (bi, c, 0))
    bwd = pl.BlockSpec((None, tb, RWKV_WIDTH), lambda bi, c: (bi, nb - 1 - c, 0))
    return pl.pallas_call(
        functools.partial(_rwkv_scan_kernel, nsub=nsub),
        out_shape=[jax.ShapeDtypeStruct((b, s, RWKV_WIDTH), F32)] * 2,
        grid=(b, nb),
        in_specs=[fwd] * 6 + [bwd] * 6,
        out_specs=[fwd, bwd],
        scratch_shapes=[pltpu.VMEM((2, N_PAIRS, LANES, LANES), F32)],
        compiler_params=_params("parallel", "arbitrary"),
        name="rwkv_scan",
    )(r, v, kk, lwf, kf, bf, r, v, kk, lwb, kb, bb)


FFN_CHUNK = 256


def _ffn_apply(x, pre, post, wg_ref, wu_ref, wd_ref):
    xn = _rms(x, pre, NORM_EPS).astype(BF16)
    acc = jnp.zeros(x.shape, F32)
    for c in range(FFN_HIDDEN // FFN_CHUNK):
        cols = slice(c * FFN_CHUNK, (c + 1) * FFN_CHUNK)
        gate = jnp.dot(xn, wg_ref[:, cols], preferred_element_type=F32)
        up = jnp.dot(xn, wu_ref[:, cols], preferred_element_type=F32)
        h = (gate * _sigmoid(gate) * up).astype(BF16)
        acc = acc + jnp.dot(h, wd_ref[cols, :], preferred_element_type=F32)
    return x + _rms(acc, post, NORM_EPS)


def _ffn_specs(index_map):
    resident = lambda shape: pl.BlockSpec(shape, index_map, pipeline_mode=pl.Buffered(1))
    return [pl.BlockSpec((1, D_MODEL), index_map), pl.BlockSpec((1, D_MODEL), index_map),
            resident((D_MODEL, FFN_HIDDEN)), resident((D_MODEL, FFN_HIDDEN)), resident((FFN_HIDDEN, D_MODEL))]


def _mix0_ffn_kernel(x_ref, oa_ref, yf_ref, yb_ref, bonus_ref, g_ref, lnw_ref, lnb_ref, bd_ref,
                     w_ref, gain_ref, pre_ref, post_ref, wg_ref, wu_ref, wd_ref, o_ref):
    y = yf_ref[...] + yb_ref[...]
    bd = bd_ref[...]
    mean = _head_sum(y, bd) * (1.0 / HEAD_DIM)
    yc = y - mean
    var = _head_sum(yc * yc, bd) * (1.0 / HEAD_DIM)
    yn = yc * lax.rsqrt(var + RWKV_GN_EPS) * lnw_ref[...] + lnb_ref[...]
    ob = (yn + bonus_ref[...]) * g_ref[...]
    m = (jnp.dot(oa_ref[...].astype(BF16), w_ref[0:DIFF_WIDTH, :], preferred_element_type=F32)
         + jnp.dot(ob.astype(BF16), w_ref[DIFF_WIDTH:, :], preferred_element_type=F32))
    x1 = x_ref[...] + _rms(m, gain_ref[...], NORM_EPS)
    o_ref[...] = _ffn_apply(x1, pre_ref[...], post_ref[...], wg_ref, wu_ref, wd_ref)


def _mix0_ffn(x2d, oa, yf, yb, bonus, g, lnw, lnb, bd, w_bf16, gain, ffn, tm=512):
    m = x2d.shape[0]
    pre, post, wg, wu, wd = ffn
    row = lambda w: pl.BlockSpec((tm, w), lambda i: (i, 0))
    const = lambda i: (0, 0)
    small = (lnw.reshape(1, -1), lnb.reshape(1, -1), bd, w_bf16, gain.reshape(1, -1))
    return pl.pallas_call(
        _mix0_ffn_kernel,
        out_shape=jax.ShapeDtypeStruct((m, D_MODEL), F32),
        grid=(m // tm,),
        in_specs=([row(D_MODEL)] + [row(RWKV_WIDTH)] * 5 + [pl.BlockSpec(a.shape, const) for a in small]
                  + _ffn_specs(const)),
        out_specs=row(D_MODEL),
        compiler_params=_params("parallel"),
        name="mix0_ffn",
    )(x2d, oa, yf, yb, bonus, g, *small, pre.reshape(1, -1), post.reshape(1, -1), wg, wu, wd)


def _proj1_kernel(x_ref, g_ref, w_ref, cos_ref, sin_ref, *refs):
    out_refs, scr = refs[:-1], refs[-1]
    tm = x_ref.shape[0]
    xb = _rms(x_ref[...], g_ref[...], NORM_EPS).astype(BF16)
    cos = cos_ref[...]
    sin = sin_ref[...]
    lane = lax.broadcasted_iota(jnp.int32, (1, LANES), 1)
    upper = (lane % HEAD_DIM) >= (HEAD_DIM // 2)
    slot = 0
    for kind in range(3):
        for gi, (_, dil) in enumerate(DIL_PAIRS):
            ref = out_refs[kind * len(DIL_PAIRS) + gi]
            col = kind * DIL_WIDTH + gi * DIL_GROUP_WIDTH
            y2 = jnp.dot(xb, w_ref[:, col:col + DIL_GROUP_WIDTH], preferred_element_type=F32)
            for c in range(DIL_GROUP_WIDTH // LANES):
                y = y2[:, c * LANES:(c + 1) * LANES]
                if kind < 2:
                    y = _rope_tile(y, cos, sin, upper)
                if kind == 0:
                    y = y * (HEAD_DIM ** -0.5 * LOG2E)
                if dil == 1:
                    ref[:, c * LANES:(c + 1) * LANES] = y.astype(BF16)
                    continue
                scr[slot] = y
                for rho in range(dil):
                    rows = scr[slot, pl.ds(rho, tm // dil, stride=dil), :]
                    lo = rho * DIL_GROUP_WIDTH + c * LANES
                    ref[:, lo:lo + LANES] = rows.astype(BF16)
                slot += 1


def _norm_proj1(x3d, gain, w_bf16, cos, sin, tm=512):
    b, s, _ = x3d.shape
    n_fold = sum(1 for _, d in DIL_PAIRS if d > 1) * 3 * (DIL_GROUP_WIDTH // LANES)
    out_shape, out_specs = [], []
    for _ in range(3):
        for _, d in DIL_PAIRS:
            out_shape.append(jax.ShapeDtypeStruct((b, s // d, d * DIL_GROUP_WIDTH), BF16))
            out_specs.append(pl.BlockSpec((None, tm // d, d * DIL_GROUP_WIDTH), lambda bi, i: (bi, i, 0)))
    return pl.pallas_call(
        _proj1_kernel,
        out_shape=out_shape,
        grid=(b, s // tm),
        in_specs=[
            pl.BlockSpec((None, tm, D_MODEL), lambda bi, i: (bi, i, 0)),
            pl.BlockSpec((1, D_MODEL), lambda bi, i: (0, 0)),
            pl.BlockSpec(w_bf16.shape, lambda bi, i: (0, 0)),
            pl.BlockSpec((tm, LANES), lambda bi, i: (i, 0)),
            pl.BlockSpec((tm, LANES), lambda bi, i: (i, 0)),
        ],
        out_specs=out_specs,
        scratch_shapes=[pltpu.VMEM((n_fold, tm, LANES), F32)],
        compiler_params=_params("parallel", "parallel"),
        name="norm_proj1",
    )(x3d, gain.reshape(1, -1), w_bf16, cos, sin)


def _band_attn_kernel(q_ref, kp_ref, kc_ref, kn_ref, vp_ref, vc_ref, vn_ref, o_ref, lse_ref, *,
                      length, nsub, nres):
    qb = LANES
    halo = DIL_RADIUS
    wlen = qb + 2 * halo
    l0 = pl.program_id(2) * (nsub * qb)
    m0 = lax.broadcasted_iota(jnp.int32, (1, LANES), 1) < HEAD_DIM
    ti = lax.broadcasted_iota(jnp.int32, (2 * qb, wlen), 0) % qb
    ji = lax.broadcasted_iota(jnp.int32, (2 * qb, wlen), 1)
    band_bias = jnp.where(jnp.abs(ji - halo - ti) <= halo, 0.0, NEG_INF).astype(F32)
    jcol = lax.broadcasted_iota(jnp.int32, (1, wlen), 1)

    def window(p_ref, c_ref, n_ref, j, cols):
        lo = j * qb - halo
        parts = []
        if lo < 0:
            parts.append(p_ref[:, cols])
            lo = 0
        hi = min((j + 1) * qb + halo, nsub * qb)
        parts.append(c_ref[lo:hi, cols])
        if (j + 1) * qb + halo > nsub * qb:
            parts.append(n_ref[:, cols])
        return jnp.concatenate(parts, axis=0) if len(parts) > 1 else parts[0]

    jobs = [(r, j, p) for r in range(nres) for j in range(nsub) for p in range(DIL_GROUP_WIDTH // LANES)]
    scores = []
    for r, j, p in jobs:
        cols = slice(r * DIL_GROUP_WIDTH + p * LANES, r * DIL_GROUP_WIDTH + (p + 1) * LANES)
        q_st = _stack(q_ref[j * qb:(j + 1) * qb, cols], m0)
        s = _mm_nt(q_st, window(kp_ref, kc_ref, kn_ref, j, cols))
        s = s + band_bias
        if j == 0 or j == nsub - 1:
            kpos0 = l0 + j * qb - halo
            s = s + jnp.where((jcol + kpos0 >= 0) & (jcol + kpos0 < length), 0.0, NEG_INF).astype(F32)
        scores.append(s)
    stats = []
    for s in scores:
        mx = jnp.max(s, axis=-1, keepdims=True)
        pr = jnp.exp2(s - mx)
        stats.append((mx, jnp.sum(pr, axis=-1, keepdims=True), pr.astype(BF16)))
    for (r, j, p), (mx, den, pr) in zip(jobs, stats):
        cols = slice(r * DIL_GROUP_WIDTH + p * LANES, r * DIL_GROUP_WIDTH + (p + 1) * LANES)
        o_st = _mm(pr, window(vp_ref, vc_ref, vn_ref, j, cols)) / den
        lse = (mx + jnp.log2(den)) * (1.0 / LOG2E)
        o_ref[j * qb:(j + 1) * qb, cols] = jnp.where(m0, o_st[:qb], o_st[qb:]).astype(o_ref.dtype)
        lse_ref[j * qb:(j + 1) * qb, cols] = jnp.where(m0, lse[:qb], lse[qb:])


def _band_attention(q, k, v, dilation):
    b, length, width = q.shape
    nsub = min(4, length // LANES)
    nres = min(dilation, 4 // nsub)
    tq = nsub * LANES
    hb = tq // DIL_RADIUS
    nh = length // DIL_RADIUS
    bw = nres * DIL_GROUP_WIDTH
    cur = lambda bi, r, i: (bi, i, r)
    prev = lambda bi, r, i: (bi, jnp.maximum(i * hb - 1, 0), r)
    nxt = lambda bi, r, i: (bi, jnp.minimum((i + 1) * hb, nh - 1), r)
    main = pl.BlockSpec((None, tq, bw), cur)
    hp = pl.BlockSpec((None, DIL_RADIUS, bw), prev)
    hn = pl.BlockSpec((None, DIL_RADIUS, bw), nxt)
    return pl.pallas_call(
        functools.partial(_band_attn_kernel, length=length, nsub=nsub, nres=nres),
        out_shape=[jax.ShapeDtypeStruct((b, length, width), BF16),
                   jax.ShapeDtypeStruct((b, length, width), F32)],
        grid=(b, dilation // nres, length // tq),
        in_specs=[main, hp, main, hn, hp, main, hn],
        out_specs=[main, main],
        compiler_params=_params("parallel", "parallel", "parallel"),
        name="band_attn",
    )(q, k, k, k, v, v, v)


def _mix1_ffn_kernel(x_ref, o0_ref, o1_ref, o2_ref, l0_ref, l1_ref, l2_ref, w_ref, gain_ref,
                     pre_ref, post_ref, wg_ref, wu_ref, wd_ref, out_ref, scr):
    tm = x_ref.shape[0]

    def unfold(ref, dil, slot):
        if dil == 1:
            return ref[...]
        halves = DIL_GROUP_WIDTH // LANES
        for rho in range(dil):
            for c in range(halves):
                lo = rho * DIL_GROUP_WIDTH + c * LANES
                scr[slot * halves + c, pl.ds(rho, tm // dil, stride=dil), :] = ref[:, lo:lo + LANES].astype(F32)
        return jnp.concatenate([scr[slot * halves + c] for c in range(halves)], axis=1)

    dils = [d for _, d in DIL_PAIRS]
    os_, ls, slot = [], [], 0
    for o_ref, l_ref, d in zip((o0_ref, o1_ref, o2_ref), (l0_ref, l1_ref, l2_ref), dils):
        os_.append(unfold(o_ref, d, slot))
        ls.append(unfold(l_ref, d, slot + 1))
        slot += 2 if d > 1 else 0
    mx = jnp.maximum(jnp.maximum(ls[0], ls[1]), ls[2])
    es = [jnp.exp(l - mx) for l in ls]
    den = es[0] + es[1] + es[2]
    m = jnp.zeros((tm, D_MODEL), F32)
    for gi in range(3):
        y = (os_[gi] * (es[gi] / den)).astype(BF16)
        m = m + jnp.dot(y, w_ref[gi * DIL_GROUP_WIDTH:(gi + 1) * DIL_GROUP_WIDTH, :],
                        preferred_element_type=F32)
    x1 = x_ref[...] + _rms(m, gain_ref[...], NORM_EPS)
    out_ref[...] = _ffn_apply(x1, pre_ref[...], post_ref[...], wg_ref, wu_ref, wd_ref)


def _mix1_ffn(x3d, outs, lses, w_bf16, gain, ffn, tm=512):
    b, s, _ = x3d.shape
    pre, post, wg, wu, wd = ffn
    const = lambda bi, i: (0, 0)
    row = pl.BlockSpec((None, tm, D_MODEL), lambda bi, i: (bi, i, 0))
    folded = [pl.BlockSpec((None, tm // d, d * DIL_GROUP_WIDTH), lambda bi, i: (bi, i, 0)) for _, d in DIL_PAIRS]
    n_slots = 2 * sum(1 for _, d in DIL_PAIRS if d > 1)
    return pl.pallas_call(
        _mix1_ffn_kernel,
        out_shape=jax.ShapeDtypeStruct((b, s, D_MODEL), F32),
        grid=(b, s // tm),
        in_specs=[row] + folded + folded
        + [pl.BlockSpec(w_bf16.shape, const), pl.BlockSpec((1, D_MODEL), const)] + _ffn_specs(const),
        out_specs=row,
        scratch_shapes=[pltpu.VMEM((n_slots * DIL_GROUP_WIDTH // LANES, tm, LANES), F32)],
        compiler_params=_params("parallel", "parallel"),
        name="mix1_ffn",
    )(x3d, *outs, *lses, w_bf16, gain.reshape(1, -1), pre.reshape(1, -1), post.reshape(1, -1), wg, wu, wd)


def _block_diag2(top, bottom):
    z_tr = jnp.zeros((top.shape[0], bottom.shape[1]), top.dtype)
    z_bl = jnp.zeros((bottom.shape[0], top.shape[1]), top.dtype)
    return jnp.concatenate([jnp.concatenate([top, z_tr], axis=1),
                            jnp.concatenate([z_bl, bottom], axis=1)], axis=0)


def kernel(x_prompt, x_sample, mix_pre0, mix_post0, w_in0, lam_q1, lam_k1, lam_q2, lam_k2, subln_w,
           mu_r, mu_k, mu_v, mu_w, mu_a, mu_g, w0_f, w1_f, w2_f, w0_b, w1_b, w2_b,
           a0_f, a1_f, a2_f, a0_b, a1_b, a2_b, g1, g2, k_k, k_a, r_k, lnx_w, lnx_b, w_out0,
           ffn_pre0, ffn_post0, ffn_gate0, ffn_up0, ffn_down0,
           mix_pre1, mix_post1, w_in1, w_out1, ffn_pre1, ffn_post1, ffn_gate1, ffn_up1, ffn_down1):
    bf = lambda a: a.astype(BF16)
    row = lambda a: a.reshape(1, -1).astype(F32)
    lam_init = 0.8 - 0.6 * math.exp(-0.3 * 0)
    lamq = jnp.stack([lam_q1, lam_q2]).astype(F32)
    lamk = jnp.stack([lam_k1, lam_k2]).astype(F32)
    gate_pad = 2 * LANES - g1.shape[1]
    ones_bd = _block_diag2(jnp.ones((HEAD_DIM, HEAD_DIM), BF16), jnp.ones((HEAD_DIM, HEAD_DIM), BF16))
    prep_w = (
        jnp.stack([mu_w, mu_a, mu_g]).astype(F32),
        row(jnp.concatenate([mu_r, mu_k, mu_v])),
        bf(jnp.concatenate([w1_f, w1_b], axis=1)),
        bf(_block_diag2(w2_f, w2_b)),
        bf(jnp.concatenate([a1_f, a1_b], axis=1)),
        bf(_block_diag2(a2_f, a2_b)),
        bf(jnp.pad(g1, ((0, 0), (0, gate_pad)))),
        bf(jnp.pad(g2, ((0, gate_pad), (0, 0)))),
        row(jnp.concatenate([w0_f, w0_b])),
        row(jnp.concatenate([a0_f, a0_b])),
        row(k_k), row(k_a), row(r_k.reshape(-1)),
        ones_bd,
    )
    w_in0_b, w_out0_b, w_in1_b, w_out1_b = bf(w_in0), bf(w_out0), bf(w_in1), bf(w_out1)
    ffn0 = (ffn_pre0, ffn_post0, bf(ffn_gate0), bf(ffn_up0), bf(ffn_down0))
    ffn1 = (ffn_pre1, ffn_post1, bf(ffn_gate1), bf(ffn_up1), bf(ffn_down1))

    def run(x):
        b, s, _ = x.shape
        x2d = x.reshape(b * s, D_MODEL)
        cos, sin = _rope_tables(s)
        (q, k, v, r, vv, kk, lwf, lwb, kf, kb, bfw, bbw, bonus, gate) = _proj0_prep(
            x, mix_pre0, w_in0_b, cos, sin, prep_w)
        out_a = _diff_attention(q, k, v, lamq, lamk, subln_w, lam_init)
        yf, yb = _rwkv_scan(r, vv, kk, lwf, kf, bfw, lwb, kb, bbw)
        fl = lambda a: a.reshape(b * s, -1)
        x2 = _mix0_ffn(x2d, fl(out_a), fl(yf), fl(yb), fl(bonus), fl(gate), lnx_w, lnx_b, ones_bd,
                       w_out0_b, mix_post0, ffn0)
        x2 = x2.reshape(b, s, D_MODEL)
        qkv1 = _norm_proj1(x2, mix_pre1, w_in1_b, cos, sin)
        outs, lses = [], []
        for gi, (_, dilation) in enumerate(DIL_PAIRS):
            o, lse = _band_attention(qkv1[gi], qkv1[3 + gi], qkv1[6 + gi], dilation)
            outs.append(o)
            lses.append(lse)
        return _mix1_ffn(x2, outs, lses, w_out1_b, mix_post1, ffn1)

    return (run(x_prompt), run(x_sample))
```

```python
import functools
import math

import jax
import jax.numpy as jnp
from jax import lax
from jax.experimental import pallas as pl
from jax.experimental.pallas import tpu as pltpu

F32 = jnp.float32
BF16 = jnp.bfloat16
HIGHEST = lax.Precision.HIGHEST

D_MODEL = 1024
HEAD_DIM = 64
LANES = 128
MXU_COLS = 256
DIFF_WIDTH = 512
RWKV_WIDTH = 512
N_PAIRS = RWKV_WIDTH // LANES
DIL_PAIRS = ((128, 1), (512, 4), (2048, 16))
DIL_GROUP_WIDTH = 256
DIL_WIDTH = 768
DIL_RADIUS = 64
FFN_HIDDEN = 2816
ROPE_THETA = 10000.0
NORM_EPS = 1e-6
SUBLN_EPS = 1e-5
RWKV_GN_EPS = 64e-5
NEG_INF = -1e30
LOG2E = math.log2(math.e)
CHUNK = 64
VMEM_LIMIT = 56 * 1024 * 1024

NT_DIMS = (((1,), (1,)), ((), ()))
TN_DIMS = (((0,), (0,)), ((), ()))


def _params(*sem):
    return pltpu.CompilerParams(dimension_semantics=sem, vmem_limit_bytes=VMEM_LIMIT)


def _sigmoid(x):
    return 1.0 / (1.0 + jnp.exp(-x))


def _rms(x, gain, eps):
    return x * lax.rsqrt(jnp.mean(x * x, axis=-1, keepdims=True) + eps) * gain


def _rope_tile(x, cos, sin, upper):
    rot = jnp.where(upper, pltpu.roll(x, 32, 1), pltpu.roll(x, 96, 1))
    return x * cos + rot * sin


def _rope_tables(seq):
    half = HEAD_DIM // 2
    inv = ROPE_THETA ** (-jnp.arange(half, dtype=F32) / half)
    ang = jnp.arange(seq, dtype=F32)[:, None] * inv[None, :]
    cos = jnp.cos(ang)
    sin = jnp.sin(ang)
    cos_t = jnp.tile(jnp.concatenate([cos, cos], axis=-1), (1, LANES // HEAD_DIM))
    sin_t = jnp.tile(jnp.concatenate([-sin, sin], axis=-1), (1, LANES // HEAD_DIM))
    return cos_t, sin_t


def _diff_attn_kernel(lamq_ref, lamk_ref, subln_ref, q_ref, k_ref, v_ref, o_ref,
                      m_ref, l_ref, acc_ref, s_ref, *, seq, tk, lam_init):
    q = q_ref[...]
    tq = q.shape[0]
    lane = lax.broadcasted_iota(jnp.int32, (1, LANES), 1)
    zero = jnp.zeros_like(q)
    qs = (jnp.where(lane < HEAD_DIM, q, zero), jnp.where(lane >= HEAD_DIM, q, zero))
    m_ref[...] = jnp.full(m_ref.shape, -jnp.inf, F32)
    l_ref[...] = jnp.zeros(l_ref.shape, F32)
    acc_ref[...] = jnp.zeros(acc_ref.shape, F32)
    nck = tk // LANES
    nblk = seq // tk
    row_parts = tq // 256

    def scores(j, slot):
        off = pl.multiple_of(j * tk, tk)
        kj = k_ref[pl.ds(off, tk), :]
        for c in range(2):
            s_ref[slot, c] = lax.dot_general(qs[c], kj, NT_DIMS, preferred_element_type=F32)

    def consume(j, slot):
        off = pl.multiple_of(j * tk, tk)
        vj = v_ref[pl.ds(off, tk), :]
        for c in range(2):
            for h in range(row_parts):
                rows = slice(h * tq // row_parts, (h + 1) * tq // row_parts)
                cols = [s_ref[slot, c, rows, i * LANES:(i + 1) * LANES] for i in range(nck)]
                mx = cols[0]
                for col in cols[1:]:
                    mx = jnp.maximum(mx, col)
                m_old = m_ref[c, rows, :]
                m_new = jnp.maximum(m_old, jnp.max(mx, axis=-1, keepdims=True))
                alpha = jnp.exp2(m_old - m_new)
                ps = [jnp.exp2(col - m_new) for col in cols]
                lsum = ps[0]
                for pc in ps[1:]:
                    lsum = lsum + pc
                l_ref[c, rows, :] = alpha * l_ref[c, rows, :] + lsum
                p = jnp.concatenate([pc.astype(BF16) for pc in ps], axis=1)
                acc_ref[c, rows, :] = (alpha * acc_ref[c, rows, :]
                                       + jnp.dot(p, vj, preferred_element_type=F32))
                m_ref[c, rows, :] = m_new

    scores(0, 0)

    def body(i, carry):
        scores(2 * i + 1, 1)
        consume(2 * i, 0)
        scores(2 * i + 2, 0)
        consume(2 * i + 1, 1)
        return carry

    lax.fori_loop(0, nblk // 2 - 1, body, 0)
    scores(nblk - 1, 1)
    consume(nblk - 2, 0)
    consume(nblk - 1, 1)

    e = jnp.exp(jnp.sum(lamq_ref[...] * lamk_ref[...], axis=-1, keepdims=True))
    lam = e[0:1] - e[1:2] + lam_init
    l0 = jnp.sum(l_ref[0], axis=-1, keepdims=True)
    l1 = jnp.sum(l_ref[1], axis=-1, keepdims=True)
    o = acc_ref[0] / l0 - lam * (acc_ref[1] / l1)
    o_ref[...] = (_rms(o, subln_ref[...], SUBLN_EPS) * (1.0 - lam_init)).astype(o_ref.dtype)


def _diff_attention(q, k, v, lamq, lamk, subln_w, lam_init, tq=1024, tk=1024):
    b, s, _ = q.shape
    heads = DIFF_WIDTH // LANES
    kern = functools.partial(_diff_attn_kernel, seq=s, tk=tk, lam_init=lam_init)
    return pl.pallas_call(
        kern,
        out_shape=jax.ShapeDtypeStruct((b, s, DIFF_WIDTH), BF16),
        grid=(b, heads, s // tq),
        in_specs=[
            pl.BlockSpec((2, HEAD_DIM), lambda bi, h, i: (0, 0)),
            pl.BlockSpec((2, HEAD_DIM), lambda bi, h, i: (0, 0)),
            pl.BlockSpec((1, LANES), lambda bi, h, i: (0, 0)),
            pl.BlockSpec((None, tq, LANES), lambda bi, h, i: (bi, i, h)),
            pl.BlockSpec((None, s, LANES), lambda bi, h, i: (bi, 0, h)),
            pl.BlockSpec((None, s, LANES), lambda bi, h, i: (bi, 0, h)),
        ],
        out_specs=pl.BlockSpec((None, tq, LANES), lambda bi, h, i: (bi, i, h)),
        scratch_shapes=[
            pltpu.VMEM((2, tq, LANES), F32),
            pltpu.VMEM((2, tq, LANES), F32),
            pltpu.VMEM((2, tq, LANES), F32),
            pltpu.VMEM((2, 2, tq, tk), F32),
        ],
        compiler_params=_params("parallel", "parallel", "parallel"),
        name="diff_attn",
    )(lamq, lamk, subln_w.reshape(1, -1), q, k, v)


def _cshift(x, prev_row, next_row):
    t = x.shape[0]
    row = lax.broadcasted_iota(jnp.int32, (t, 1), 0)
    p = jnp.where(row == 0, prev_row, pltpu.roll(x, 1, 0))
    n = jnp.where(row == t - 1, next_row, pltpu.roll(x, t - 1, 0))
    return 0.5 * (p + n)


def _head_sum(x, ones_bd):
    hi = x.astype(BF16)
    lo = (x - hi.astype(F32)).astype(BF16)
    parts = []
    for p in range(x.shape[1] // LANES):
        sl = slice(p * LANES, (p + 1) * LANES)
        parts.append(jnp.dot(hi[:, sl], ones_bd, preferred_element_type=F32)
                     + jnp.dot(lo[:, sl], ones_bd, preferred_element_type=F32))
    return jnp.concatenate(parts, axis=1)


def _proj0_prep_kernel(x_ref, xp_ref, xq_ref, gain_ref, w_ref, cos_ref, sin_ref,
                       mux_ref, mut_ref, w1_ref, w2_ref, a1_ref, a2_ref, g1_ref, g2_ref,
                       w0_ref, a0_ref, kk_ref, ka_ref, rk_ref, bd_ref,
                       q_out, k_out, v_out,
                       r_out, rv_out, kk_out, lwf_out, lwb_out, kf_out, kb_out, bf_out, bb_out,
                       bonus_out, g_out):
    i = pl.program_id(1)
    first = jnp.where(i > 0, 1.0, 0.0).astype(F32)
    last = jnp.where(i < pl.num_programs(1) - 1, 1.0, 0.0).astype(F32)
    gain = gain_ref[...]
    xn = _rms(x_ref[...], gain, NORM_EPS)
    xn_p = _rms(xp_ref[...], gain, NORM_EPS) * first
    xn_q = _rms(xq_ref[...], gain, NORM_EPS) * last
    xb = xn.astype(BF16)

    cos = cos_ref[...]
    sin = sin_ref[...]
    lane = lax.broadcasted_iota(jnp.int32, (1, LANES), 1)
    upper = (lane % HEAD_DIM) >= (HEAD_DIM // 2)
    for kind, ref in enumerate((q_out, k_out, v_out)):
        for c in range(DIFF_WIDTH // MXU_COLS):
            col = kind * DIFF_WIDTH + c * MXU_COLS
            y2 = jnp.dot(xb, w_ref[:, col:col + MXU_COLS], preferred_element_type=F32)
            for h in range(MXU_COLS // LANES):
                y = y2[:, h * LANES:(h + 1) * LANES]
                if kind < 2:
                    y = _rope_tile(y, cos, sin, upper)
                if kind == 0:
                    y = y * (HEAD_DIM ** -0.5 * LOG2E)
                lo = c * MXU_COLS + h * LANES
                ref[:, lo:lo + LANES] = y.astype(BF16)

    rkv0 = 3 * DIFF_WIDTH
    rows = x_ref.shape[0]
    xb_ext = jnp.concatenate([xb, xn_p.astype(BF16), xn_q.astype(BF16)], axis=0)
    t_ext = jnp.concatenate(
        [jnp.dot(xb_ext, w_ref[:, rkv0 + c * MXU_COLS:rkv0 + (c + 1) * MXU_COLS], preferred_element_type=F32)
         for c in range(3 * RWKV_WIDTH // MXU_COLS)], axis=1)
    t = t_ext[:rows]
    t_p = t_ext[rows:rows + 8]
    t_q = t_ext[rows + 8:]

    xx = _cshift(xn, xn_p[7:8, :], xn_q[0:1, :]) - xn
    mux = mux_ref[...]
    xw = (xn + xx * mux[0:1]).astype(BF16)
    xa = (xn + xx * mux[1:2]).astype(BF16)
    xg = (xn + xx * mux[2:3]).astype(BF16)

    ts = t + (_cshift(t, t_p[7:8, :], t_q[0:1, :]) - t) * mut_ref[...]
    r = ts[:, 0:RWKV_WIDTH]
    k = ts[:, RWKV_WIDTH:2 * RWKV_WIDTH]
    v = ts[:, 2 * RWKV_WIDTH:3 * RWKV_WIDTH]

    hw = jnp.tanh(jnp.dot(xw, w1_ref[...], preferred_element_type=F32))
    dec = jnp.dot(hw.astype(BF16), w2_ref[...], preferred_element_type=F32) + w0_ref[...]
    ha = jnp.dot(xa, a1_ref[...], preferred_element_type=F32)
    rate = _sigmoid(jnp.dot(ha.astype(BF16), a2_ref[...], preferred_element_type=F32) + a0_ref[...])
    hg = _sigmoid(jnp.dot(xg, g1_ref[...], preferred_element_type=F32))
    g_out[...] = jnp.dot(hg.astype(BF16), g2_ref[...], preferred_element_type=F32).astype(g_out.dtype)

    lw = -math.exp(-0.5) * _sigmoid(dec)
    lwf_out[...] = lw[:, 0:RWKV_WIDTH]
    lwb_out[...] = lw[:, RWKV_WIDTH:]

    bd = bd_ref[...]
    kk = k * kk_ref[...]
    kk = kk / jnp.maximum(jnp.sqrt(_head_sum(kk * kk, bd)), 1e-12)
    a_f = rate[:, 0:RWKV_WIDTH]
    a_b = rate[:, RWKV_WIDTH:]
    ka = ka_ref[...]
    k_f = k * (1.0 + (a_f - 1.0) * ka)
    k_b = k * (1.0 + (a_b - 1.0) * ka)
    store = lambda ref, val: ref.__setitem__(Ellipsis, val.astype(ref.dtype))
    store(r_out, r)
    store(rv_out, v)
    store(kk_out, kk)
    store(kf_out, k_f)
    store(kb_out, k_b)
    store(bf_out, kk * a_f)
    store(bb_out, kk * a_b)
    store(bonus_out, _head_sum(r * (0.5 * (k_f + k_b)) * rk_ref[...], bd) * v)


def _proj0_prep(x3d, gain, w_bf16, cos, sin, wts, ts=512):
    b, s, _ = x3d.shape
    nb8 = s // 8
    r8 = ts // 8
    full = lambda a: pl.BlockSpec(a.shape, lambda bi, i: (0,) * a.ndim)
    resident = lambda a: pl.BlockSpec(a.shape, lambda bi, i: (0,) * a.ndim, pipeline_mode=pl.Buffered(1))
    in_specs = [
        pl.BlockSpec((None, ts, D_MODEL), lambda bi, i: (bi, i, 0)),
        pl.BlockSpec((None, 8, D_MODEL), lambda bi, i: (bi, jnp.maximum(i * r8 - 1, 0), 0)),
        pl.BlockSpec((None, 8, D_MODEL), lambda bi, i: (bi, jnp.minimum((i + 1) * r8, nb8 - 1), 0)),
        pl.BlockSpec((1, D_MODEL), lambda bi, i: (0, 0)),
        resident(w_bf16),
        pl.BlockSpec((ts, LANES), lambda bi, i: (i, 0)),
        pl.BlockSpec((ts, LANES), lambda bi, i: (i, 0)),
    ] + [full(a) for a in wts]
    row = pl.BlockSpec((None, ts, RWKV_WIDTH), lambda bi, i: (bi, i, 0))
    dtypes = [BF16] * 3 + [BF16, BF16, BF16, F32, F32, BF16, BF16, BF16, BF16, BF16, BF16]
    return pl.pallas_call(
        _proj0_prep_kernel,
        out_shape=[jax.ShapeDtypeStruct((b, s, RWKV_WIDTH), dt) for dt in dtypes],
        grid=(b, s // ts),
        in_specs=in_specs,
        out_specs=[row] * len(dtypes),
        compiler_params=_params("parallel", "parallel"),
        name="proj0_prep",
    )(x3d, x3d, x3d, gain.reshape(1, -1), w_bf16, cos, sin, *wts)


def _mm(a, b):
    return jnp.dot(a, b, preferred_element_type=F32)


def _mm_nt(a, b):
    return lax.dot_general(a, b, NT_DIMS, preferred_element_type=F32)


def _mm_tn(a, b):
    return lax.dot_general(a, b, TN_DIMS, preferred_element_type=F32)


def _stack(x, m0):
    zero = jnp.zeros_like(x)
    return jnp.concatenate([jnp.where(m0, x, zero), jnp.where(m0, zero, x)], axis=0)


def _cumsum_rows(x, reverse):
    n = x.shape[0]
    row = lax.broadcasted_iota(jnp.int32, (n, 1), 0)
    k = 1
    while k < n:
        if reverse:
            x = x + jnp.where(row < n - k, pltpu.roll(x, n - k, 0), 0.0)
        else:
            x = x + jnp.where(row >= k, pltpu.roll(x, k, 0), 0.0)
        k *= 2
    return x


def _chunk_local(jobs, masks):
    eye, eye_side, same, m0 = masks["eye"], masks["eye_side"], masks["same"], masks["m0"]
    c = CHUNK
    cat0 = lambda *xs: jnp.concatenate(xs, axis=0)
    cat1 = lambda *xs: jnp.concatenate(xs, axis=1)
    st = lambda x: _stack(x, m0)
    ops = []
    for jb in jobs:
        r32 = jb["r"] * jb["p_inc"]
        ops.append(dict(
            a=(-jb["a"] * jb["p_exc"]).astype(BF16), r32=r32, r=r32.astype(BF16),
            b=(jb["b"] * jb["p_inv"]).astype(BF16), k=(jb["k"] * jb["p_inv"]).astype(BF16),
            v=jb["v"].astype(BF16),
            bh=(jb["b"] * jb["e_hat"]).astype(BF16), kh=(jb["k"] * jb["e_hat"]).astype(BF16)))
    gs = [_mm_nt(cat0(o["a"], o["r"]), cat0(st(o["b"]), st(o["k"]))) for o in ops]
    zero = jnp.zeros((c, LANES), F32)
    n_ab, a_ak, a_rb, a_rk = [], [], [], []
    for jb, g in zip(jobs, gs):
        strict, incl = masks["strict"][jb["dir"]], masks["incl"][jb["dir"]]
        n_ab.append(jnp.where(strict, g[:c, :LANES], zero))
        a_ak.append(jnp.where(strict, g[:c, LANES:], zero).astype(BF16))
        a_rb.append(jnp.where(incl, g[c:, :LANES], zero).astype(BF16))
        a_rk.append(jnp.where(incl, g[c:, LANES:], zero).astype(BF16))
    v_st = [st(o["v"]) for o in ops]
    akv = [_mm(m, v) for m, v in zip(a_ak, v_st)]
    minv = [jnp.where(eye_side, 1.0, 0.0).astype(F32) + n for n in n_ab]
    pw = [n.astype(BF16) for n in n_ab]
    pw = [_mm(p, st(p)).astype(BF16) for p in pw]
    for _ in range(int(math.log2(CHUNK)) - 2):
        res = [_mm(cat0(p, m.astype(BF16)), st(p)) for p, m in zip(pw, minv)]
        pw = [r_[:c].astype(BF16) for r_ in res]
        minv = [m + r_[c:] for m, r_ in zip(minv, res)]
    minv = [m + _mm(m.astype(BF16), st(p)) for m, p in zip(minv, pw)]
    xs = [_mm(m.astype(BF16), cat1(st(o["a"]), st(u.astype(BF16)))) for m, o, u in zip(minv, ops, akv)]
    out = []
    zero_b = jnp.zeros((c, LANES), BF16)
    for jb, o, x, rb_, rk_, vs in zip(jobs, ops, xs, a_rb, a_rk, v_st):
        w1 = x[:, :LANES].astype(BF16)
        u_loc = x[:, LANES:].astype(BF16)
        y_loc = _mm(cat1(rb_, rk_), cat0(st(u_loc), vs))
        rw = o["r32"] + _mm(rb_, st(w1))
        pd = _mm_tn(cat0(o["bh"], o["kh"]), cat0(cat1(w1, u_loc), cat1(zero_b, o["v"])))
        phi = jnp.where(eye, jb["p_tot"], 0.0) + jnp.where(same, pd[:, :LANES], 0.0)
        dm = jnp.where(same, pd[:, LANES:], 0.0)
        out.append((rw.astype(BF16), y_loc, phi.astype(BF16), dm))
    return out


def _rwkv_scan_kernel(rf_ref, vf_ref, af_ref, lwf_ref, kf_ref, bf_ref,
                      rb_ref, vb_ref, ab_ref, lwb_ref, kb_ref, bb_ref,
                      yf_ref, yb_ref, state_ref, *, nsub):
    @pl.when(pl.program_id(1) == 0)
    def _():
        state_ref[...] = jnp.zeros(state_ref.shape, F32)

    n2 = 2 * CHUNK
    ri = lax.broadcasted_iota(jnp.int32, (n2, n2), 0)
    ci = lax.broadcasted_iota(jnp.int32, (n2, n2), 1)
    t_side = lax.broadcasted_iota(jnp.int32, (CHUNK, LANES), 0)
    s_side = lax.broadcasted_iota(jnp.int32, (CHUNK, LANES), 1) % CHUNK
    masks = dict(
        eye=ri == ci,
        same=(ri // CHUNK) == (ci // CHUNK),
        eye_side=s_side == t_side,
        m0=lax.broadcasted_iota(jnp.int32, (1, LANES), 1) < HEAD_DIM,
        strict=(s_side < t_side, s_side > t_side),
        incl=(s_side <= t_side, s_side >= t_side),
    )
    dirs = (
        (rf_ref, vf_ref, af_ref, lwf_ref, kf_ref, bf_ref, yf_ref),
        (rb_ref, vb_ref, ab_ref, lwb_ref, kb_ref, bb_ref, yb_ref),
    )
    jobs = []
    for d, (r_ref, v_ref, a_ref, lw_ref, k_ref, b_ref, _) in enumerate(dirs):
        tot_row = CHUNK - 1 if d == 0 else 0
        for sub in range(nsub):
            rows = slice(sub * CHUNK, (sub + 1) * CHUNK)
            lw = lw_ref[rows, :]
            cum = _cumsum_rows(lw, reverse=(d == 1))
            tot = cum[tot_row:tot_row + 1, :]
            rowops = dict(r=r_ref[rows, :], v=v_ref[rows, :], a=a_ref[rows, :], k=k_ref[rows, :],
                          b=b_ref[rows, :], p_inc=jnp.exp(cum), p_inv=jnp.exp(-cum),
                          p_exc=jnp.exp(cum - lw), e_hat=jnp.exp(tot - cum), p_tot=jnp.exp(tot))
            for p in range(N_PAIRS):
                sl = slice(p * LANES, (p + 1) * LANES)
                job = {name: val[:, sl] for name, val in rowops.items()}
                job.update(dir=d, sub=sub, pair=p)
                jobs.append(job)
    local = _chunk_local(jobs, masks)
    by_key = {(jb["dir"], jb["sub"], jb["pair"]): loc for jb, loc in zip(jobs, local)}
    states = {(d, p): state_ref[d, p] for d in range(2) for p in range(N_PAIRS)}
    for step in range(nsub):
        for d in range(2):
            sub = step if d == 0 else nsub - 1 - step
            y_ref = dirs[d][-1]
            for p in range(N_PAIRS):
                rw, y_loc, phi, dm = by_key[(d, sub, p)]
                res = _mm(jnp.concatenate([rw, phi], axis=0), states[(d, p)].astype(BF16))
                states[(d, p)] = res[CHUNK:] + dm
                y_ref[sub * CHUNK:(sub + 1) * CHUNK, p * LANES:(p + 1) * LANES] = res[:CHUNK] + y_loc
    for (d, p), t in states.items():
        state_ref[d, p] = t


def _rwkv_scan(r, v, kk, lwf, kf, bf, lwb, kb, bb, nsub=4):
    b, s, _ = r.shape
    tb = nsub * CHUNK
    nb = s // tb
    fwd = pl.BlockSpec((None, tb, RWKV_WIDTH), lambda bi, c: (bi, c, 0))
    bwd = pl.BlockSpec((None, tb, RWKV_WIDTH), lambda bi, c: (bi, nb - 1 - c, 0))
    return pl.pallas_call(
        functools.partial(_rwkv_scan_kernel, nsub=nsub),
        out_shape=[jax.ShapeDtypeStruct((b, s, RWKV_WIDTH), F32)] * 2,
        grid=(b, nb),
        in_specs=[fwd] * 6 + [bwd] * 6,
        out_specs=[fwd, bwd],
        scratch_shapes=[pltpu.VMEM((2, N_PAIRS, LANES, LANES), F32)],
        compiler_params=_params("parallel", "arbitrary"),
        name="rwkv_scan",
    )(r, v, kk, lwf, kf, bf, r, v, kk, lwb, kb, bb)


FFN_CHUNK = 256


def _ffn_apply(x, pre, post, wg_ref, wu_ref, wd_ref):
    xn = _rms(x, pre, NORM_EPS).astype(BF16)
    acc = jnp.zeros(x.shape, F32)
    for c in range(FFN_HIDDEN // FFN_CHUNK):
        cols = slice(c * FFN_CHUNK, (c + 1) * FFN_CHUNK)
        gate = jnp.dot(xn, wg_ref[:, cols], preferred_element_type=F32)
        up = jnp.dot(xn, wu_ref[:, cols], preferred_element_type=F32)
        h = (gate * _sigmoid(gate) * up).astype(BF16)
        acc = acc + jnp.dot(h, wd_ref[cols, :], preferred_element_type=F32)
    return x + _rms(acc, post, NORM_EPS)


def _ffn_specs(index_map):
    resident = lambda shape: pl.BlockSpec(shape, index_map, pipeline_mode=pl.Buffered(1))
    return [pl.BlockSpec((1, D_MODEL), index_map), pl.BlockSpec((1, D_MODEL), index_map),
            resident((D_MODEL, FFN_HIDDEN)), resident((D_MODEL, FFN_HIDDEN)), resident((FFN_HIDDEN, D_MODEL))]


def _mix0_ffn_kernel(x_ref, oa_ref, yf_ref, yb_ref, bonus_ref, g_ref, lnw_ref, lnb_ref, bd_ref,
                     w_ref, gain_ref, pre_ref, post_ref, wg_ref, wu_ref, wd_ref, o_ref):
    y = yf_ref[...] + yb_ref[...]
    bd = bd_ref[...]
    mean = _head_sum(y, bd) * (1.0 / HEAD_DIM)
    yc = y - mean
    var = _head_sum(yc * yc, bd) * (1.0 / HEAD_DIM)
    yn = yc * lax.rsqrt(var + RWKV_GN_EPS) * lnw_ref[...] + lnb_ref[...]
    ob = (yn + bonus_ref[...]) * g_ref[...]
    m = (jnp.dot(oa_ref[...].astype(BF16), w_ref[0:DIFF_WIDTH, :], preferred_element_type=F32)
         + jnp.dot(ob.astype(BF16), w_ref[DIFF_WIDTH:, :], preferred_element_type=F32))
    x1 = x_ref[...] + _rms(m, gain_ref[...], NORM_EPS)
    o_ref[...] = _ffn_apply(x1, pre_ref[...], post_ref[...], wg_ref, wu_ref, wd_ref)


def _mix0_ffn(x2d, oa, yf, yb, bonus, g, lnw, lnb, bd, w_bf16, gain, ffn, tm=512):
    m = x2d.shape[0]
    pre, post, wg, wu, wd = ffn
    row = lambda w: pl.BlockSpec((tm, w), lambda i: (i, 0))
    const = lambda i: (0, 0)
    small = (lnw.reshape(1, -1), lnb.reshape(1, -1), bd, w_bf16, gain.reshape(1, -1))
    return pl.pallas_call(
        _mix0_ffn_kernel,
        out_shape=jax.ShapeDtypeStruct((m, D_MODEL), F32),
        grid=(m // tm,),
        in_specs=([row(D_MODEL)] + [row(RWKV_WIDTH)] * 5 + [pl.BlockSpec(a.shape, const) for a in small]
                  + _ffn_specs(const)),
        out_specs=row(D_MODEL),
        compiler_params=_params("parallel"),
        name="mix0_ffn",
    )(x2d, oa, yf, yb, bonus, g, *small, pre.reshape(1, -1), post.reshape(1, -1), wg, wu, wd)


def _proj1_kernel(x_ref, g_ref, w_ref, cos_ref, sin_ref, *refs):
    out_refs, scr = refs[:-1], refs[-1]
    tm = x_ref.shape[0]
    xb = _rms(x_ref[...], g_ref[...], NORM_EPS).astype(BF16)
    cos = cos_ref[...]
    sin = sin_ref[...]
    lane = lax.broadcasted_iota(jnp.int32, (1, LANES), 1)
    upper = (lane % HEAD_DIM) >= (HEAD_DIM // 2)
    slot = 0
    for kind in range(3):
        for gi, (_, dil) in enumerate(DIL_PAIRS):
            ref = out_refs[kind * len(DIL_PAIRS) + gi]
            col = kind * DIL_WIDTH + gi * DIL_GROUP_WIDTH
            y2 = jnp.dot(xb, w_ref[:, col:col + DIL_GROUP_WIDTH], preferred_element_type=F32)
            for c in range(DIL_GROUP_WIDTH // LANES):
                y = y2[:, c * LANES:(c + 1) * LANES]
                if kind < 2:
                    y = _rope_tile(y, cos, sin, upper)
                if kind == 0:
                    y = y * (HEAD_DIM ** -0.5 * LOG2E)
                if dil == 1:
                    ref[:, c * LANES:(c + 1) * LANES] = y.astype(BF16)
                    continue
                scr[slot] = y
                for rho in range(dil):
                    rows = scr[slot, pl.ds(rho, tm // dil, stride=dil), :]
                    lo = rho * DIL_GROUP_WIDTH + c * LANES
                    ref[:, lo:lo + LANES] = rows.astype(BF16)
                slot += 1


def _norm_proj1(x3d, gain, w_bf16, cos, sin, tm=512):
    b, s, _ = x3d.shape
    n_fold = sum(1 for _, d in DIL_PAIRS if d > 1) * 3 * (DIL_GROUP_WIDTH // LANES)
    out_shape, out_specs = [], []
    for _ in range(3):
        for _, d in DIL_PAIRS:
            out_shape.append(jax.ShapeDtypeStruct((b, s // d, d * DIL_GROUP_WIDTH), BF16))
            out_specs.append(pl.BlockSpec((None, tm // d, d * DIL_GROUP_WIDTH), lambda bi, i: (bi, i, 0)))
    return pl.pallas_call(
        _proj1_kernel,
        out_shape=out_shape,
        grid=(b, s // tm),
        in_specs=[
            pl.BlockSpec((None, tm, D_MODEL), lambda bi, i: (bi, i, 0)),
            pl.BlockSpec((1, D_MODEL), lambda bi, i: (0, 0)),
            pl.BlockSpec(w_bf16.shape, lambda bi, i: (0, 0)),
            pl.BlockSpec((tm, LANES), lambda bi, i: (i, 0)),
            pl.BlockSpec((tm, LANES), lambda bi, i: (i, 0)),
        ],
        out_specs=out_specs,
        scratch_shapes=[pltpu.VMEM((n_fold, tm, LANES), F32)],
        compiler_params=_params("parallel", "parallel"),
        name="norm_proj1",
    )(x3d, gain.reshape(1, -1), w_bf16, cos, sin)


def _band_attn_kernel(q_ref, kp_ref, kc_ref, kn_ref, vp_ref, vc_ref, vn_ref, o_ref, lse_ref, *,
                      length, nsub, nres):
    qb = LANES
    halo = DIL_RADIUS
    wlen = qb + 2 * halo
    l0 = pl.program_id(2) * (nsub * qb)
    m0 = lax.broadcasted_iota(jnp.int32, (1, LANES), 1) < HEAD_DIM
    ti = lax.broadcasted_iota(jnp.int32, (2 * qb, wlen), 0) % qb
    ji = lax.broadcasted_iota(jnp.int32, (2 * qb, wlen), 1)
    band_bias = jnp.where(jnp.abs(ji - halo - ti) <= halo, 0.0, NEG_INF).astype(F32)
    jcol = lax.broadcasted_iota(jnp.int32, (1, wlen), 1)

    def window(p_ref, c_ref, n_ref, j, cols):
        lo = j * qb - halo
        parts = []
        if lo < 0:
            parts.append(p_ref[:, cols])
            lo = 0
        hi = min((j + 1) * qb + halo, nsub * qb)
        parts.append(c_ref[lo:hi, cols])
        if (j + 1) * qb + halo > nsub * qb:
            parts.append(n_ref[:, cols])
        return jnp.concatenate(parts, axis=0) if len(parts) > 1 else parts[0]

    jobs = [(r, j, p) for r in range(nres) for j in range(nsub) for p in range(DIL_GROUP_WIDTH // LANES)]
    scores = []
    for r, j, p in jobs:
        cols = slice(r * DIL_GROUP_WIDTH + p * LANES, r * DIL_GROUP_WIDTH + (p + 1) * LANES)
        q_st = _stack(q_ref[j * qb:(j + 1) * qb, cols], m0)
        s = _mm_nt(q_st, window(kp_ref, kc_ref, kn_ref, j, cols))
        s = s + band_bias
        if j == 0 or j == nsub - 1:
            kpos0 = l0 + j * qb - halo
            s = s + jnp.where((jcol + kpos0 >= 0) & (jcol + kpos0 < length), 0.0, NEG_INF).astype(F32)
        scores.append(s)
    stats = []
    for s in scores:
        mx = jnp.max(s, axis=-1, keepdims=True)
        pr = jnp.exp2(s - mx)
        stats.append((mx, jnp.sum(pr, axis=-1, keepdims=True), pr.astype(BF16)))
    for (r, j, p), (mx, den, pr) in zip(jobs, stats):
        cols = slice(r * DIL_GROUP_WIDTH + p * LANES, r * DIL_GROUP_WIDTH + (p + 1) * LANES)
        o_st = _mm(pr, window(vp_ref, vc_ref, vn_ref, j, cols)) / den
        lse = (mx + jnp.log2(den)) * (1.0 / LOG2E)
        o_ref[j * qb:(j + 1) * qb, cols] = jnp.where(m0, o_st[:qb], o_st[qb:]).astype(o_ref.dtype)
        lse_ref[j * qb:(j + 1) * qb, cols] = jnp.where(m0, lse[:qb], lse[qb:])


def _band_attention(q, k, v, dilation):
    b, length, width = q.shape
    nsub = min(4, length // LANES)
    nres = min(dilation, 4 // nsub)
    tq = nsub * LANES
    hb = tq // DIL_RADIUS
    nh = length // DIL_RADIUS
    bw = nres * DIL_GROUP_WIDTH
    cur = lambda bi, r, i: (bi, i, r)
    prev = lambda bi, r, i: (bi, jnp.maximum(i * hb - 1, 0), r)
    nxt = lambda bi, r, i: (bi, jnp.minimum((i + 1) * hb, nh - 1), r)
    main = pl.BlockSpec((None, tq, bw), cur)
    hp = pl.BlockSpec((None, DIL_RADIUS, bw), prev)
    hn = pl.BlockSpec((None, DIL_RADIUS, bw), nxt)
    return pl.pallas_call(
        functools.partial(_band_attn_kernel, length=length, nsub=nsub, nres=nres),
        out_shape=[jax.ShapeDtypeStruct((b, length, width), BF16),
                   jax.ShapeDtypeStruct((b, length, width), F32)],
        grid=(b, dilation // nres, length // tq),
        in_specs=[main, hp, main, hn, hp, main, hn],
        out_specs=[main, main],
        compiler_params=_params("parallel", "parallel", "parallel"),
        name="band_attn",
    )(q, k, k, k, v, v, v)


def _mix1_ffn_kernel(x_ref, o0_ref, o1_ref, o2_ref, l0_ref, l1_ref, l2_ref, w_ref, gain_ref,
                     pre_ref, post_ref, wg_ref, wu_ref, wd_ref, out_ref, scr):
    tm = x_ref.shape[0]

    def unfold(ref, dil, slot):
        if dil == 1:
            return ref[...]
        halves = DIL_GROUP_WIDTH // LANES
        for rho in range(dil):
            for c in range(halves):
                lo = rho * DIL_GROUP_WIDTH + c * LANES
                scr[slot * halves + c, pl.ds(rho, tm // dil, stride=dil), :] = ref[:, lo:lo + LANES].astype(F32)
        return jnp.concatenate([scr[slot * halves + c] for c in range(halves)], axis=1)

    dils = [d for _, d in DIL_PAIRS]
    os_, ls, slot = [], [], 0
    for o_ref, l_ref, d in zip((o0_ref, o1_ref, o2_ref), (l0_ref, l1_ref, l2_ref), dils):
        os_.append(unfold(o_ref, d, slot))
        ls.append(unfold(l_ref, d, slot + 1))
        slot += 2 if d > 1 else 0
    mx = jnp.maximum(jnp.maximum(ls[0], ls[1]), ls[2])
    es = [jnp.exp(l - mx) for l in ls]
    den = es[0] + es[1] + es[2]
    m = jnp.zeros((tm, D_MODEL), F32)
    for gi in range(3):
        y = (os_[gi] * (es[gi] / den)).astype(BF16)
        m = m + jnp.dot(y, w_ref[gi * DIL_GROUP_WIDTH:(gi + 1) * DIL_GROUP_WIDTH, :],
                        preferred_element_type=F32)
    x1 = x_ref[...] + _rms(m, gain_ref[...], NORM_EPS)
    out_ref[...] = _ffn_apply(x1, pre_ref[...], post_ref[...], wg_ref, wu_ref, wd_ref)


def _mix1_ffn(x3d, outs, lses, w_bf16, gain, ffn, tm=512):
    b, s, _ = x3d.shape
    pre, post, wg, wu, wd = ffn
    const = lambda bi, i: (0, 0)
    row = pl.BlockSpec((None, tm, D_MODEL), lambda bi, i: (bi, i, 0))
    folded = [pl.BlockSpec((None, tm // d, d * DIL_GROUP_WIDTH), lambda bi, i: (bi, i, 0)) for _, d in DIL_PAIRS]
    n_slots = 2 * sum(1 for _, d in DIL_PAIRS if d > 1)
    return pl.pallas_call(
        _mix1_ffn_kernel,
        out_shape=jax.ShapeDtypeStruct((b, s, D_MODEL), F32),
        grid=(b, s // tm),
        in_specs=[row] + folded + folded
        + [pl.BlockSpec(w_bf16.shape, const), pl.BlockSpec((1, D_MODEL), const)] + _ffn_specs(const),
        out_specs=row,
        scratch_shapes=[pltpu.VMEM((n_slots * DIL_GROUP_WIDTH // LANES, tm, LANES), F32)],
        compiler_params=_params("parallel", "parallel"),
        name="mix1_ffn",
    )(x3d, *outs, *lses, w_bf16, gain.reshape(1, -1), pre.reshape(1, -1), post.reshape(1, -1), wg, wu, wd)


def _block_diag2(top, bottom):
    z_tr = jnp.zeros((top.shape[0], bottom.shape[1]), top.dtype)
    z_bl = jnp.zeros((bottom.shape[0], top.shape[1]), top.dtype)
    return jnp.concatenate([jnp.concatenate([top, z_tr], axis=1),
                            jnp.concatenate([z_bl, bottom], axis=1)], axis=0)


def kernel(x_prompt, x_sample, mix_pre0, mix_post0, w_in0, lam_q1, lam_k1, lam_q2, lam_k2, subln_w,
           mu_r, mu_k, mu_v, mu_w, mu_a, mu_g, w0_f, w1_f, w2_f, w0_b, w1_b, w2_b,
           a0_f, a1_f, a2_f, a0_b, a1_b, a2_b, g1, g2, k_k, k_a, r_k, lnx_w, lnx_b, w_out0,
           ffn_pre0, ffn_post0, ffn_gate0, ffn_up0, ffn_down0,
           mix_pre1, mix_post1, w_in1, w_out1, ffn_pre1, ffn_post1, ffn_gate1, ffn_up1, ffn_down1):
    bf = lambda a: a.astype(BF16)
    row = lambda a: a.reshape(1, -1).astype(F32)
    lam_init = 0.8 - 0.6 * math.exp(-0.3 * 0)
    lamq = jnp.stack([lam_q1, lam_q2]).astype(F32)
    lamk = jnp.stack([lam_k1, lam_k2]).astype(F32)
    gate_pad = 2 * LANES - g1.shape[1]
    ones_bd = _block_diag2(jnp.ones((HEAD_DIM, HEAD_DIM), BF16), jnp.ones((HEAD_DIM, HEAD_DIM), BF16))
    prep_w = (
        jnp.stack([mu_w, mu_a, mu_g]).astype(F32),
        row(jnp.concatenate([mu_r, mu_k, mu_v])),
        bf(jnp.concatenate([w1_f, w1_b], axis=1)),
        bf(_block_diag2(w2_f, w2_b)),
        bf(jnp.concatenate([a1_f, a1_b], axis=1)),
        bf(_block_diag2(a2_f, a2_b)),
        bf(jnp.pad(g1, ((0, 0), (0, gate_pad)))),
        bf(jnp.pad(g2, ((0, gate_pad), (0, 0)))),
        row(jnp.concatenate([w0_f, w0_b])),
        row(jnp.concatenate([a0_f, a0_b])),
        row(k_k), row(k_a), row(r_k.reshape(-1)),
        ones_bd,
    )
    w_in0_b, w_out0_b, w_in1_b, w_out1_b = bf(w_in0), bf(w_out0), bf(w_in1), bf(w_out1)
    ffn0 = (ffn_pre0, ffn_post0, bf(ffn_gate0), bf(ffn_up0), bf(ffn_down0))
    ffn1 = (ffn_pre1, ffn_post1, bf(ffn_gate1), bf(ffn_up1), bf(ffn_down1))

    def run(x):
        b, s, _ = x.shape
        x2d = x.reshape(b * s, D_MODEL)
        cos, sin = _rope_tables(s)
        (q, k, v, r, vv, kk, lwf, lwb, kf, kb, bfw, bbw, bonus, gate) = _proj0_prep(
            x, mix_pre0, w_in0_b, cos, sin, prep_w)
        out_a = _diff_attention(q, k, v, lamq, lamk, subln_w, lam_init)
        yf, yb = _rwkv_scan(r, vv, kk, lwf, kf, bfw, lwb, kb, bbw)
        fl = lambda a: a.reshape(b * s, -1)
        x2 = _mix0_ffn(x2d, fl(out_a), fl(yf), fl(yb), fl(bonus), fl(gate), lnx_w, lnx_b, ones_bd,
                       w_out0_b, mix_post0, ffn0)
        x2 = x2.reshape(b, s, D_MODEL)
        qkv1 = _norm_proj1(x2, mix_pre1, w_in1_b, cos, sin)
        outs, lses = [], []
        for gi, (_, dilation) in enumerate(DIL_PAIRS):
            o, lse = _band_attention(qkv1[gi], qkv1[3 + gi], qkv1[6 + gi], dilation)
            outs.append(o)
            lses.append(lse)
        return _mix1_ffn(x2, outs, lses, w_out1_b, mix_post1, ffn1)

    return (run(x_prompt), run(x_sample))
```

```python
import functools
import math

import jax
import jax.numpy as jnp
from jax import lax
from jax.experimental import pallas as pl
from jax.experimental.pallas import tpu as pltpu

F32 = jnp.float32
BF16 = jnp.bfloat16

D_MODEL = 1024
HEAD_DIM = 64
LANES = 128
MXU_COLS = 256
DIFF_WIDTH = 512
RWKV_WIDTH = 512
N_PAIRS = RWKV_WIDTH // LANES
DIL_PAIRS = ((128, 1), (512, 4), (2048, 16))
DIL_GROUP_WIDTH = 256
DIL_WIDTH = 768
DIL_RADIUS = 64
FFN_HIDDEN = 2816
ROPE_THETA = 10000.0
NORM_EPS = 1e-6
SUBLN_EPS = 1e-5
RWKV_GN_EPS = 64e-5
NEG_INF = -1e30
LOG2E = math.log2(math.e)
CHUNK = 64
VMEM_LIMIT = 56 * 1024 * 1024

NT_DIMS = (((1,), (1,)), ((), ()))
TN_DIMS = (((0,), (0,)), ((), ()))


def _params(*sem):
    return pltpu.CompilerParams(dimension_semantics=sem, vmem_limit_bytes=VMEM_LIMIT)


def _sigmoid(x):
    return 1.0 / (1.0 + jnp.exp(-x))


def _rms(x, gain, eps):
    return x * lax.rsqrt(jnp.mean(x * x, axis=-1, keepdims=True) + eps) * gain


def _rope_tile(x, cos, sin, upper):
    rot = jnp.where(upper, pltpu.roll(x, 32, 1), pltpu.roll(x, 96, 1))
    return x * cos + rot * sin


def _rope_tables(seq):
    half = HEAD_DIM // 2
    inv = ROPE_THETA ** (-jnp.arange(half, dtype=F32) / half)
    ang = jnp.arange(seq, dtype=F32)[:, None] * inv[None, :]
    cos = jnp.cos(ang)
    sin = jnp.sin(ang)
    cos_t = jnp.tile(jnp.concatenate([cos, cos], axis=-1), (1, LANES // HEAD_DIM))
    sin_t = jnp.tile(jnp.concatenate([-sin, sin], axis=-1), (1, LANES // HEAD_DIM))
    return cos_t, sin_t


def _diff_attn_kernel(lamq_ref, lamk_ref, subln_ref, q_ref, k_ref, v_ref, o_ref,
                      m_ref, l_ref, acc_ref, s_ref, *, seq, tk, lam_init):
    q = q_ref[...]
    tq = q.shape[0]
    lane = lax.broadcasted_iota(jnp.int32, (1, LANES), 1)
    zero = jnp.zeros_like(q)
    qs = (jnp.where(lane < HEAD_DIM, q, zero), jnp.where(lane >= HEAD_DIM, q, zero))
    m_ref[...] = jnp.full(m_ref.shape, -jnp.inf, F32)
    l_ref[...] = jnp.zeros(l_ref.shape, F32)
    acc_ref[...] = jnp.zeros(acc_ref.shape, F32)
    nck = tk // LANES
    nblk = seq // tk
    row_parts = tq // 256

    def scores(j, slot):
        off = pl.multiple_of(j * tk, tk)
        kj = k_ref[pl.ds(off, tk), :]
        for c in range(2):
            s_ref[slot, c] = lax.dot_general(qs[c], kj, NT_DIMS, preferred_element_type=F32)

    def consume(j, slot):
        off = pl.multiple_of(j * tk, tk)
        vj = v_ref[pl.ds(off, tk), :]
        for c in range(2):
            for h in range(row_parts):
                rows = slice(h * tq // row_parts, (h + 1) * tq // row_parts)
                cols = [s_ref[slot, c, rows, i * LANES:(i + 1) * LANES] for i in range(nck)]
                mx = cols[0]
                for col in cols[1:]:
                    mx = jnp.maximum(mx, col)
                m_old = m_ref[c, rows, :]
                m_new = jnp.maximum(m_old, jnp.max(mx, axis=-1, keepdims=True))
                alpha = jnp.exp2(m_old - m_new)
                ps = [jnp.exp2(col - m_new) for col in cols]
                lsum = ps[0]
                for pc in ps[1:]:
                    lsum = lsum + pc
                l_ref[c, rows, :] = alpha * l_ref[c, rows, :] + lsum
                p = jnp.concatenate([pc.astype(BF16) for pc in ps], axis=1)
                acc_ref[c, rows, :] = (alpha * acc_ref[c, rows, :]
                                       + jnp.dot(p, vj, preferred_element_type=F32))
                m_ref[c, rows, :] = m_new

    scores(0, 0)

    def body(i, carry):
        scores(2 * i + 1, 1)
        consume(2 * i, 0)
        scores(2 * i + 2, 0)
        consume(2 * i + 1, 1)
        return carry

    lax.fori_loop(0, nblk // 2 - 1, body, 0)
    scores(nblk - 1, 1)
    consume(nblk - 2, 0)
    consume(nblk - 1, 1)

    e = jnp.exp(jnp.sum(lamq_ref[...] * lamk_ref[...], axis=-1, keepdims=True))
    lam = e[0:1] - e[1:2] + lam_init
    l0 = jnp.sum(l_ref[0], axis=-1, keepdims=True)
    l1 = jnp.sum(l_ref[1], axis=-1, keepdims=True)
    o = acc_ref[0] / l0 - lam * (acc_ref[1] / l1)
    o_ref[...] = (_rms(o, subln_ref[...], SUBLN_EPS) * (1.0 - lam_init)).astype(o_ref.dtype)


def _diff_attention(q, k, v, lamq, lamk, subln_w, lam_init, tq=1024, tk=1024):
    b, s, _ = q.shape
    assert s % tq == 0 and s % (2 * tk) == 0, (s, tq, tk)
    heads = DIFF_WIDTH // LANES
    kern = functools.partial(_diff_attn_kernel, seq=s, tk=tk, lam_init=lam_init)
    return pl.pallas_call(
        kern,
        out_shape=jax.ShapeDtypeStruct((b, s, DIFF_WIDTH), BF16),
        grid=(b, heads, s // tq),
        in_specs=[
            pl.BlockSpec((2, HEAD_DIM), lambda bi, h, i: (0, 0)),
            pl.BlockSpec((2, HEAD_DIM), lambda bi, h, i: (0, 0)),
            pl.BlockSpec((1, LANES), lambda bi, h, i: (0, 0)),
            pl.BlockSpec((None, tq, LANES), lambda bi, h, i: (bi, i, h)),
            pl.BlockSpec((None, s, LANES), lambda bi, h, i: (bi, 0, h)),
            pl.BlockSpec((None, s, LANES), lambda bi, h, i: (bi, 0, h)),
        ],
        out_specs=pl.BlockSpec((None, tq, LANES), lambda bi, h, i: (bi, i, h)),
        scratch_shapes=[
            pltpu.VMEM((2, tq, LANES), F32),
            pltpu.VMEM((2, tq, LANES), F32),
            pltpu.VMEM((2, tq, LANES), F32),
            pltpu.VMEM((2, 2, tq, tk), F32),
        ],
        compiler_params=_params("parallel", "parallel", "parallel"),
        name="diff_attn",
    )(lamq, lamk, subln_w.reshape(1, -1), q, k, v)


def _cshift(x, prev_row, next_row):
    t = x.shape[0]
    row = lax.broadcasted_iota(jnp.int32, (t, 1), 0)
    p = jnp.where(row == 0, prev_row, pltpu.roll(x, 1, 0))
    n = jnp.where(row == t - 1, next_row, pltpu.roll(x, t - 1, 0))
    return 0.5 * (p + n)


def _head_sum(x, ones_bd):
    hi = x.astype(BF16)
    lo = (x - hi.astype(F32)).astype(BF16)
    parts = []
    for p in range(x.shape[1] // LANES):
        sl = slice(p * LANES, (p + 1) * LANES)
        parts.append(jnp.dot(hi[:, sl], ones_bd, preferred_element_type=F32)
                     + jnp.dot(lo[:, sl], ones_bd, preferred_element_type=F32))
    return jnp.concatenate(parts, axis=1)


def _proj0_prep_kernel(x_ref, xp_ref, xq_ref, gain_ref, w_ref, cos_ref, sin_ref,
                       mux_ref, mut_ref, w1_ref, w2_ref, a1_ref, a2_ref, g1_ref, g2_ref,
                       w0_ref, a0_ref, kk_ref, ka_ref, rk_ref, bd_ref,
                       q_out, k_out, v_out,
                       r_out, rv_out, kk_out, lwf_out, lwb_out, kf_out, kb_out, bf_out, bb_out,
                       bonus_out, g_out):
    i = pl.program_id(1)
    first = jnp.where(i > 0, 1.0, 0.0).astype(F32)
    last = jnp.where(i < pl.num_programs(1) - 1, 1.0, 0.0).astype(F32)
    gain = gain_ref[...]
    xn = _rms(x_ref[...], gain, NORM_EPS)
    xn_p = _rms(xp_ref[...], gain, NORM_EPS) * first
    xn_q = _rms(xq_ref[...], gain, NORM_EPS) * last
    xb = xn.astype(BF16)

    cos = cos_ref[...]
    sin = sin_ref[...]
    lane = lax.broadcasted_iota(jnp.int32, (1, LANES), 1)
    upper = (lane % HEAD_DIM) >= (HEAD_DIM // 2)
    for kind, ref in enumerate((q_out, k_out, v_out)):
        for c in range(DIFF_WIDTH // MXU_COLS):
            col = kind * DIFF_WIDTH + c * MXU_COLS
            y2 = jnp.dot(xb, w_ref[:, col:col + MXU_COLS], preferred_element_type=F32)
            for h in range(MXU_COLS // LANES):
                y = y2[:, h * LANES:(h + 1) * LANES]
                if kind < 2:
                    y = _rope_tile(y, cos, sin, upper)
                if kind == 0:
                    y = y * (HEAD_DIM ** -0.5 * LOG2E)
                lo = c * MXU_COLS + h * LANES
                ref[:, lo:lo + LANES] = y.astype(BF16)

    rkv0 = 3 * DIFF_WIDTH
    rows = x_ref.shape[0]
    xb_ext = jnp.concatenate([xb, xn_p.astype(BF16), xn_q.astype(BF16)], axis=0)
    t_ext = jnp.concatenate(
        [jnp.dot(xb_ext, w_ref[:, rkv0 + c * MXU_COLS:rkv0 + (c + 1) * MXU_COLS], preferred_element_type=F32)
         for c in range(3 * RWKV_WIDTH // MXU_COLS)], axis=1)
    t = t_ext[:rows]
    t_p = t_ext[rows:rows + 8]
    t_q = t_ext[rows + 8:]

    xx = _cshift(xn, xn_p[7:8, :], xn_q[0:1, :]) - xn
    mux = mux_ref[...]
    xw = (xn + xx * mux[0:1]).astype(BF16)
    xa = (xn + xx * mux[1:2]).astype(BF16)
    xg = (xn + xx * mux[2:3]).astype(BF16)

    ts = t + (_cshift(t, t_p[7:8, :], t_q[0:1, :]) - t) * mut_ref[...]
    r = ts[:, 0:RWKV_WIDTH]
    k = ts[:, RWKV_WIDTH:2 * RWKV_WIDTH]
    v = ts[:, 2 * RWKV_WIDTH:3 * RWKV_WIDTH]

    hw = jnp.tanh(jnp.dot(xw, w1_ref[...], preferred_element_type=F32))
    dec = jnp.dot(hw.astype(BF16), w2_ref[...], preferred_element_type=F32) + w0_ref[...]
    ha = jnp.dot(xa, a1_ref[...], preferred_element_type=F32)
    rate = _sigmoid(jnp.dot(ha.astype(BF16), a2_ref[...], preferred_element_type=F32) + a0_ref[...])
    hg = _sigmoid(jnp.dot(xg, g1_ref[...], preferred_element_type=F32))
    g_out[...] = jnp.dot(hg.astype(BF16), g2_ref[...], preferred_element_type=F32).astype(g_out.dtype)

    lw = -math.exp(-0.5) * _sigmoid(dec)
    lwf_out[...] = lw[:, 0:RWKV_WIDTH]
    lwb_out[...] = lw[:, RWKV_WIDTH:]

    bd = bd_ref[...]
    kk = k * kk_ref[...]
    kk = kk / jnp.maximum(jnp.sqrt(_head_sum(kk * kk, bd)), 1e-12)
    a_f = rate[:, 0:RWKV_WIDTH]
    a_b = rate[:, RWKV_WIDTH:]
    ka = ka_ref[...]
    k_f = k * (1.0 + (a_f - 1.0) * ka)
    k_b = k * (1.0 + (a_b - 1.0) * ka)
    store = lambda ref, val: ref.__setitem__(Ellipsis, val.astype(ref.dtype))
    store(r_out, r)
    store(rv_out, v)
    store(kk_out, kk)
    store(kf_out, k_f)
    store(kb_out, k_b)
    store(bf_out, kk * a_f)
    store(bb_out, kk * a_b)
    store(bonus_out, _head_sum(r * (0.5 * (k_f + k_b)) * rk_ref[...], bd) * v)


def _proj0_prep(x3d, gain, w_bf16, cos, sin, wts, ts=512):
    b, s, _ = x3d.shape
    assert s % ts == 0 and ts % 8 == 0, (s, ts)
    nb8 = s // 8
    r8 = ts // 8
    full = lambda a: pl.BlockSpec(a.shape, lambda bi, i: (0,) * a.ndim)
    resident = lambda a: pl.BlockSpec(a.shape, lambda bi, i: (0,) * a.ndim, pipeline_mode=pl.Buffered(1))
    in_specs = [
        pl.BlockSpec((None, ts, D_MODEL), lambda bi, i: (bi, i, 0)),
        pl.BlockSpec((None, 8, D_MODEL), lambda bi, i: (bi, jnp.maximum(i * r8 - 1, 0), 0)),
        pl.BlockSpec((None, 8, D_MODEL), lambda bi, i: (bi, jnp.minimum((i + 1) * r8, nb8 - 1), 0)),
        pl.BlockSpec((1, D_MODEL), lambda bi, i: (0, 0)),
        resident(w_bf16),
        pl.BlockSpec((ts, LANES), lambda bi, i: (i, 0)),
        pl.BlockSpec((ts, LANES), lambda bi, i: (i, 0)),
    ] + [full(a) for a in wts]
    row = pl.BlockSpec((None, ts, RWKV_WIDTH), lambda bi, i: (bi, i, 0))
    dtypes = [BF16] * 3 + [BF16, BF16, BF16, F32, F32, BF16, BF16, BF16, BF16, BF16, BF16]
    return pl.pallas_call(
        _proj0_prep_kernel,
        out_shape=[jax.ShapeDtypeStruct((b, s, RWKV_WIDTH), dt) for dt in dtypes],
        grid=(b, s // ts),
        in_specs=in_specs,
        out_specs=[row] * len(dtypes),
        compiler_params=_params("parallel", "parallel"),
        name="proj0_prep",
    )(x3d, x3d, x3d, gain.reshape(1, -1), w_bf16, cos, sin, *wts)


def _mm(a, b):
    return jnp.dot(a, b, preferred_element_type=F32)


def _mm_nt(a, b):
    return lax.dot_general(a, b, NT_DIMS, preferred_element_type=F32)


def _mm_tn(a, b):
    return lax.dot_general(a, b, TN_DIMS, preferred_element_type=F32)


def _stack(x, m0):
    zero = jnp.zeros_like(x)
    return jnp.concatenate([jnp.where(m0, x, zero), jnp.where(m0, zero, x)], axis=0)


def _cumsum_rows(x, reverse):
    n = x.shape[0]
    row = lax.broadcasted_iota(jnp.int32, (n, 1), 0)
    k = 1
    while k < n:
        if reverse:
            x = x + jnp.where(row < n - k, pltpu.roll(x, n - k, 0), 0.0)
        else:
            x = x + jnp.where(row >= k, pltpu.roll(x, k, 0), 0.0)
        k *= 2
    return x


def _chunk_local(jobs, masks):
    eye, eye_side, same, m0 = masks["eye"], masks["eye_side"], masks["same"], masks["m0"]
    c = CHUNK
    cat0 = lambda *xs: jnp.concatenate(xs, axis=0)
    cat1 = lambda *xs: jnp.concatenate(xs, axis=1)
    st = lambda x: _stack(x, m0)
    ops = []
    for jb in jobs:
        r32 = jb["r"] * jb["p_inc"]
        ops.append(dict(
            a=(-jb["a"] * jb["p_exc"]).astype(BF16), r32=r32, r=r32.astype(BF16),
            b=(jb["b"] * jb["p_inv"]).astype(BF16), k=(jb["k"] * jb["p_inv"]).astype(BF16),
            v=jb["v"].astype(BF16),
            bh=(jb["b"] * jb["e_hat"]).astype(BF16), kh=(jb["k"] * jb["e_hat"]).astype(BF16)))
    gs = [_mm_nt(cat0(o["a"], o["r"]), cat0(st(o["b"]), st(o["k"]))) for o in ops]
    zero = jnp.zeros((c, LANES), F32)
    n_ab, a_ak, a_rb, a_rk = [], [], [], []
    for jb, g in zip(jobs, gs):
        strict, incl = masks["strict"][jb["dir"]], masks["incl"][jb["dir"]]
        n_ab.append(jnp.where(strict, g[:c, :LANES], zero))
        a_ak.append(jnp.where(strict, g[:c, LANES:], zero).astype(BF16))
        a_rb.append(jnp.where(incl, g[c:, :LANES], zero).astype(BF16))
        a_rk.append(jnp.where(incl, g[c:, LANES:], zero).astype(BF16))
    v_st = [st(o["v"]) for o in ops]
    akv = [_mm(m, v) for m, v in zip(a_ak, v_st)]
    minv = [jnp.where(eye_side, 1.0, 0.0).astype(F32) + n for n in n_ab]
    pw = [n.astype(BF16) for n in n_ab]
    pw = [_mm(p, st(p)).astype(BF16) for p in pw]
    for _ in range(int(math.log2(CHUNK)) - 2):
        res = [_mm(cat0(p, m.astype(BF16)), st(p)) for p, m in zip(pw, minv)]
        pw = [r_[:c].astype(BF16) for r_ in res]
        minv = [m + r_[c:] for m, r_ in zip(minv, res)]
    minv = [m + _mm(m.astype(BF16), st(p)) for m, p in zip(minv, pw)]
    xs = [_mm(m.astype(BF16), cat1(st(o["a"]), st(u.astype(BF16)))) for m, o, u in zip(minv, ops, akv)]
    out = []
    zero_b = jnp.zeros((c, LANES), BF16)
    for jb, o, x, rb_, rk_, vs in zip(jobs, ops, xs, a_rb, a_rk, v_st):
        w1 = x[:, :LANES].astype(BF16)
        u_loc = x[:, LANES:].astype(BF16)
        y_loc = _mm(cat1(rb_, rk_), cat0(st(u_loc), vs))
        rw = o["r32"] + _mm(rb_, st(w1))
        pd = _mm_tn(cat0(o["bh"], o["kh"]), cat0(cat1(w1, u_loc), cat1(zero_b, o["v"])))
        phi = jnp.where(eye, jb["p_tot"], 0.0) + jnp.where(same, pd[:, :LANES], 0.0)
        dm = jnp.where(same, pd[:, LANES:], 0.0)
        out.append((rw.astype(BF16), y_loc, phi.astype(BF16), dm))
    return out


def _rwkv_scan_kernel(rf_ref, vf_ref, af_ref, lwf_ref, kf_ref, bf_ref,
                      rb_ref, vb_ref, ab_ref, lwb_ref, kb_ref, bb_ref,
                      yf_ref, yb_ref, state_ref, *, nsub):
    @pl.when(pl.program_id(1) == 0)
    def _():
        state_ref[...] = jnp.zeros(state_ref.shape, F32)

    n2 = 2 * CHUNK
    ri = lax.broadcasted_iota(jnp.int32, (n2, n2), 0)
    ci = lax.broadcasted_iota(jnp.int32, (n2, n2), 1)
    t_side = lax.broadcasted_iota(jnp.int32, (CHUNK, LANES), 0)
    s_side = lax.broadcasted_iota(jnp.int32, (CHUNK, LANES), 1) % CHUNK
    masks = dict(
        eye=ri == ci,
        same=(ri // CHUNK) == (ci // CHUNK),
        eye_side=s_side == t_side,
        m0=lax.broadcasted_iota(jnp.int32, (1, LANES), 1) < HEAD_DIM,
        strict=(s_side < t_side, s_side > t_side),
        incl=(s_side <= t_side, s_side >= t_side),
    )
    dirs = (
        (rf_ref, vf_ref, af_ref, lwf_ref, kf_ref, bf_ref, yf_ref),
        (rb_ref, vb_ref, ab_ref, lwb_ref, kb_ref, bb_ref, yb_ref),
    )
    jobs = []
    for d, (r_ref, v_ref, a_ref, lw_ref, k_ref, b_ref, _) in enumerate(dirs):
        tot_row = CHUNK - 1 if d == 0 else 0
        for sub in range(nsub):
            rows = slice(sub * CHUNK, (sub + 1) * CHUNK)
            lw = lw_ref[rows, :]
            cum = _cumsum_rows(lw, reverse=(d == 1))
            tot = cum[tot_row:tot_row + 1, :]
            rowops = dict(r=r_ref[rows, :], v=v_ref[rows, :], a=a_ref[rows, :], k=k_ref[rows, :],
                          b=b_ref[rows, :], p_inc=jnp.exp(cum), p_inv=jnp.exp(-cum),
                          p_exc=jnp.exp(cum - lw), e_hat=jnp.exp(tot - cum), p_tot=jnp.exp(tot))
            for p in range(N_PAIRS):
                sl = slice(p * LANES, (p + 1) * LANES)
                job = {name: val[:, sl] for name, val in rowops.items()}
                job.update(dir=d, sub=sub, pair=p)
                jobs.append(job)
    local = _chunk_local(jobs, masks)
    by_key = {(jb["dir"], jb["sub"], jb["pair"]): loc for jb, loc in zip(jobs, local)}
    states = {(d, p): state_ref[d, p] for d in range(2) for p in range(N_PAIRS)}
    for step in range(nsub):
        for d in range(2):
            sub = step if d == 0 else nsub - 1 - step
            y_ref = dirs[d][-1]
            for p in range(N_PAIRS):
                rw, y_loc, phi, dm = by_key[(d, sub, p)]
                res = _mm(jnp.concatenate([rw, phi], axis=0), states[(d, p)].astype(BF16))
                states[(d, p)] = res[CHUNK:] + dm
                y_ref[sub * CHUNK:(sub + 1) * CHUNK, p * LANES:(p + 1) * LANES] = res[:CHUNK] + y_loc
    for (d, p), t in states.items():
        state_ref[d, p] = t


def _rwkv_scan(r, v, kk, lwf, kf, bf, lwb, kb, bb, nsub=4):
    b, s, _ = r.shape
    tb = nsub * CHUNK
    assert s % tb == 0, (s, tb)
    nb = s // tb
    fwd = pl.BlockSpec((None, tb, RWKV_WIDTH), lambda bi, c: (bi, c, 0))
    bwd = pl.BlockSpec((None, tb, RWKV_WIDTH), lambda bi, c: (bi, nb - 1 - c, 0))
    return pl.pallas_call(
        functools.partial(_rwkv_scan_kernel, nsub=nsub),
        out_shape=[jax.ShapeDtypeStruct((b, s, RWKV_WIDTH), F32)] * 2,
        grid=(b, nb),
        in_specs=[fwd] * 6 + [bwd] * 6,
        out_specs=[fwd, bwd],
        scratch_shapes=[pltpu.VMEM((2, N_PAIRS, LANES, LANES), F32)],
        compiler_params=_params("parallel", "arbitrary"),
        name="rwkv_scan",
    )(r, v, kk, lwf, kf, bf, r, v, kk, lwb, kb, bb)


FFN_CHUNK = 256


def _ffn_apply(x, pre, post, wg_ref, wu_ref, wd_ref):
    xn = _rms(x, pre, NORM_EPS).astype(BF16)
    acc = jnp.zeros(x.shape, F32)
    for c in range(FFN_HIDDEN // FFN_CHUNK):
        cols = slice(c * FFN_CHUNK, (c + 1) * FFN_CHUNK)
        gate = jnp.dot(xn, wg_ref[:, cols], preferred_element_type=F32)
        up = jnp.dot(xn, wu_ref[:, cols], preferred_element_type=F32)
        h = (gate * _sigmoid(gate) * up).astype(BF16)
        acc = acc + jnp.dot(h, wd_ref[cols, :], preferred_element_type=F32)
    return x + _rms(acc, post, NORM_EPS)


def _ffn_specs(index_map):
    resident = lambda shape: pl.BlockSpec(shape, index_map, pipeline_mode=pl.Buffered(1))
    return [pl.BlockSpec((1, D_MODEL), index_map), pl.BlockSpec((1, D_MODEL), index_map),
            resident((D_MODEL, FFN_HIDDEN)), resident((D_MODEL, FFN_HIDDEN)), resident((FFN_HIDDEN, D_MODEL))]


def _mix0_ffn_kernel(x_ref, oa_ref, yf_ref, yb_ref, bonus_ref, g_ref, lnw_ref, lnb_ref, bd_ref,
                     w_ref, gain_ref, pre_ref, post_ref, wg_ref, wu_ref, wd_ref, o_ref):
    y = yf_ref[...] + yb_ref[...]
    bd = bd_ref[...]
    mean = _head_sum(y, bd) * (1.0 / HEAD_DIM)
    yc = y - mean
    var = _head_sum(yc * yc, bd) * (1.0 / HEAD_DIM)
    yn = yc * lax.rsqrt(var + RWKV_GN_EPS) * lnw_ref[...] + lnb_ref[...]
    ob = (yn + bonus_ref[...]) * g_ref[...]
    m = (jnp.dot(oa_ref[...].astype(BF16), w_ref[0:DIFF_WIDTH, :], preferred_element_type=F32)
         + jnp.dot(ob.astype(BF16), w_ref[DIFF_WIDTH:, :], preferred_element_type=F32))
    x1 = x_ref[...] + _rms(m, gain_ref[...], NORM_EPS)
    o_ref[...] = _ffn_apply(x1, pre_ref[...], post_ref[...], wg_ref, wu_ref, wd_ref)


def _mix0_ffn(x2d, oa, yf, yb, bonus, g, lnw, lnb, bd, w_bf16, gain, ffn, tm=512):
    m = x2d.shape[0]
    assert m % tm == 0, (m, tm)
    pre, post, wg, wu, wd = ffn
    row = lambda w: pl.BlockSpec((tm, w), lambda i: (i, 0))
    const = lambda i: (0, 0)
    small = (lnw.reshape(1, -1), lnb.reshape(1, -1), bd, w_bf16, gain.reshape(1, -1))
    return pl.pallas_call(
        _mix0_ffn_kernel,
        out_shape=jax.ShapeDtypeStruct((m, D_MODEL), F32),
        grid=(m // tm,),
        in_specs=([row(D_MODEL)] + [row(RWKV_WIDTH)] * 5 + [pl.BlockSpec(a.shape, const) for a in small]
                  + _ffn_specs(const)),
        out_specs=row(D_MODEL),
        compiler_params=_params("parallel"),
        name="mix0_ffn",
    )(x2d, oa, yf, yb, bonus, g, *small, pre.reshape(1, -1), post.reshape(1, -1), wg, wu, wd)


def _proj1_kernel(x_ref, g_ref, w_ref, cos_ref, sin_ref, *refs):
    out_refs, scr = refs[:-1], refs[-1]
    tm = x_ref.shape[0]
    xb = _rms(x_ref[...], g_ref[...], NORM_EPS).astype(BF16)
    cos = cos_ref[...]
    sin = sin_ref[...]
    lane = lax.broadcasted_iota(jnp.int32, (1, LANES), 1)
    upper = (lane % HEAD_DIM) >= (HEAD_DIM // 2)
    slot = 0
    for kind in range(3):
        for gi, (_, dil) in enumerate(DIL_PAIRS):
            ref = out_refs[kind * len(DIL_PAIRS) + gi]
            col = kind * DIL_WIDTH + gi * DIL_GROUP_WIDTH
            y2 = jnp.dot(xb, w_ref[:, col:col + DIL_GROUP_WIDTH], preferred_element_type=F32)
            for c in range(DIL_GROUP_WIDTH // LANES):
                y = y2[:, c * LANES:(c + 1) * LANES]
                if kind < 2:
                    y = _rope_tile(y, cos, sin, upper)
                if kind == 0:
                    y = y * (HEAD_DIM ** -0.5 * LOG2E)
                if dil == 1:
                    ref[:, c * LANES:(c + 1) * LANES] = y.astype(BF16)
                    continue
                scr[slot] = y
                for rho in range(dil):
                    rows = scr[slot, pl.ds(rho, tm // dil, stride=dil), :]
                    lo = rho * DIL_GROUP_WIDTH + c * LANES
                    ref[:, lo:lo + LANES] = rows.astype(BF16)
                slot += 1


def _norm_proj1(x3d, gain, w_bf16, cos, sin, tm=512):
    b, s, _ = x3d.shape
    assert s % tm == 0 and all((tm // d) % 16 == 0 for _, d in DIL_PAIRS), (s, tm)
    n_fold = sum(1 for _, d in DIL_PAIRS if d > 1) * 3 * (DIL_GROUP_WIDTH // LANES)
    out_shape, out_specs = [], []
    for _ in range(3):
        for _, d in DIL_PAIRS:
            out_shape.append(jax.ShapeDtypeStruct((b, s // d, d * DIL_GROUP_WIDTH), BF16))
            out_specs.append(pl.BlockSpec((None, tm // d, d * DIL_GROUP_WIDTH), lambda bi, i: (bi, i, 0)))
    return pl.pallas_call(
        _proj1_kernel,
        out_shape=out_shape,
        grid=(b, s // tm),
        in_specs=[
            pl.BlockSpec((None, tm, D_MODEL), lambda bi, i: (bi, i, 0)),
            pl.BlockSpec((1, D_MODEL), lambda bi, i: (0, 0)),
            pl.BlockSpec(w_bf16.shape, lambda bi, i: (0, 0)),
            pl.BlockSpec((tm, LANES), lambda bi, i: (i, 0)),
            pl.BlockSpec((tm, LANES), lambda bi, i: (i, 0)),
        ],
        out_specs=out_specs,
        scratch_shapes=[pltpu.VMEM((n_fold, tm, LANES), F32)],
        compiler_params=_params("parallel", "parallel"),
        name="norm_proj1",
    )(x3d, gain.reshape(1, -1), w_bf16, cos, sin)


def _band_attn_kernel(q_ref, kp_ref, kc_ref, kn_ref, vp_ref, vc_ref, vn_ref, o_ref, lse_ref, *,
                      length, nsub, nres):
    qb = LANES
    halo = DIL_RADIUS
    wlen = qb + 2 * halo
    l0 = pl.program_id(2) * (nsub * qb)
    m0 = lax.broadcasted_iota(jnp.int32, (1, LANES), 1) < HEAD_DIM
    ti = lax.broadcasted_iota(jnp.int32, (2 * qb, wlen), 0) % qb
    ji = lax.broadcasted_iota(jnp.int32, (2 * qb, wlen), 1)
    band_bias = jnp.where(jnp.abs(ji - halo - ti) <= halo, 0.0, NEG_INF).astype(F32)
    jcol = lax.broadcasted_iota(jnp.int32, (1, wlen), 1)

    def window(p_ref, c_ref, n_ref, j, cols):
        lo = j * qb - halo
        parts = []
        if lo < 0:
            parts.append(p_ref[:, cols])
            lo = 0
        hi = min((j + 1) * qb + halo, nsub * qb)
        parts.append(c_ref[lo:hi, cols])
        if (j + 1) * qb + halo > nsub * qb:
            parts.append(n_ref[:, cols])
        return jnp.concatenate(parts, axis=0) if len(parts) > 1 else parts[0]

    jobs = [(r, j, p) for r in range(nres) for j in range(nsub) for p in range(DIL_GROUP_WIDTH // LANES)]
    scores = []
    for r, j, p in jobs:
        cols = slice(r * DIL_GROUP_WIDTH + p * LANES, r * DIL_GROUP_WIDTH + (p + 1) * LANES)
        q_st = _stack(q_ref[j * qb:(j + 1) * qb, cols], m0)
        s = _mm_nt(q_st, window(kp_ref, kc_ref, kn_ref, j, cols))
        s = s + band_bias
        if j == 0 or j == nsub - 1:
            kpos0 = l0 + j * qb - halo
            s = s + jnp.where((jcol + kpos0 >= 0) & (jcol + kpos0 < length), 0.0, NEG_INF).astype(F32)
        scores.append(s)
    stats = []
    for s in scores:
        mx = jnp.max(s, axis=-1, keepdims=True)
        pr = jnp.exp2(s - mx)
        stats.append((mx, jnp.sum(pr, axis=-1, keepdims=True), pr.astype(BF16)))
    for (r, j, p), (mx, den, pr) in zip(jobs, stats):
        cols = slice(r * DIL_GROUP_WIDTH + p * LANES, r * DIL_GROUP_WIDTH + (p + 1) * LANES)
        o_st = _mm(pr, window(vp_ref, vc_ref, vn_ref, j, cols)) / den
        lse = (mx + jnp.log2(den)) * (1.0 / LOG2E)
        o_ref[j * qb:(j + 1) * qb, cols] = jnp.where(m0, o_st[:qb], o_st[qb:]).astype(o_ref.dtype)
        lse_ref[j * qb:(j + 1) * qb, cols] = jnp.where(m0, lse[:qb], lse[qb:])


def _band_attention(q, k, v, dilation):
    b, length, width = q.shape
    assert length % LANES == 0 and width == dilation * DIL_GROUP_WIDTH, (length, width, dilation)
    nsub = min(4, length // LANES)
    nres = min(dilation, 4 // nsub)
    tq = nsub * LANES
    hb = tq // DIL_RADIUS
    nh = length // DIL_RADIUS
    bw = nres * DIL_GROUP_WIDTH
    cur = lambda bi, r, i: (bi, i, r)
    prev = lambda bi, r, i: (bi, jnp.maximum(i * hb - 1, 0), r)
    nxt = lambda bi, r, i: (bi, jnp.minimum((i + 1) * hb, nh - 1), r)
    main = pl.BlockSpec((None, tq, bw), cur)
    hp = pl.BlockSpec((None, DIL_RADIUS, bw), prev)
    hn = pl.BlockSpec((None, DIL_RADIUS, bw), nxt)
    return pl.pallas_call(
        functools.partial(_band_attn_kernel, length=length, nsub=nsub, nres=nres),
        out_shape=[jax.ShapeDtypeStruct((b, length, width), BF16),
                   jax.ShapeDtypeStruct((b, length, width), F32)],
        grid=(b, dilation // nres, length // tq),
        in_specs=[main, hp, main, hn, hp, main, hn],
        out_specs=[main, main],
        compiler_params=_params("parallel", "parallel", "parallel"),
        name="band_attn",
    )(q, k, k, k, v, v, v)


def _mix1_ffn_kernel(x_ref, o0_ref, o1_ref, o2_ref, l0_ref, l1_ref, l2_ref, w_ref, gain_ref,
                     pre_ref, post_ref, wg_ref, wu_ref, wd_ref, out_ref, scr):
    tm = x_ref.shape[0]

    def unfold(ref, dil, slot):
        if dil == 1:
            return ref[...]
        halves = DIL_GROUP_WIDTH // LANES
        for rho in range(dil):
            for c in range(halves):
                lo = rho * DIL_GROUP_WIDTH + c * LANES
                scr[slot * halves + c, pl.ds(rho, tm // dil, stride=dil), :] = ref[:, lo:lo + LANES].astype(F32)
        return jnp.concatenate([scr[slot * halves + c] for c in range(halves)], axis=1)

    dils = [d for _, d in DIL_PAIRS]
    os_, ls, slot = [], [], 0
    for o_ref, l_ref, d in zip((o0_ref, o1_ref, o2_ref), (l0_ref, l1_ref, l2_ref), dils):
        os_.append(unfold(o_ref, d, slot))
        ls.append(unfold(l_ref, d, slot + 1))
        slot += 2 if d > 1 else 0
    mx = jnp.maximum(jnp.maximum(ls[0], ls[1]), ls[2])
    es = [jnp.exp(l - mx) for l in ls]
    den = es[0] + es[1] + es[2]
    m = jnp.zeros((tm, D_MODEL), F32)
    for gi in range(3):
        y = (os_[gi] * (es[gi] / den)).astype(BF16)
        m = m + jnp.dot(y, w_ref[gi * DIL_GROUP_WIDTH:(gi + 1) * DIL_GROUP_WIDTH, :],
                        preferred_element_type=F32)
    x1 = x_ref[...] + _rms(m, gain_ref[...], NORM_EPS)
    out_ref[...] = _ffn_apply(x1, pre_ref[...], post_ref[...], wg_ref, wu_ref, wd_ref)


def _mix1_ffn(x3d, outs, lses, w_bf16, gain, ffn, tm=512):
    b, s, _ = x3d.shape
    assert s % tm == 0 and all((tm // d) % 16 == 0 for _, d in DIL_PAIRS), (s, tm)
    pre, post, wg, wu, wd = ffn
    const = lambda bi, i: (0, 0)
    row = pl.BlockSpec((None, tm, D_MODEL), lambda bi, i: (bi, i, 0))
    folded = [pl.BlockSpec((None, tm // d, d * DIL_GROUP_WIDTH), lambda bi, i: (bi, i, 0)) for _, d in DIL_PAIRS]
    n_slots = 2 * sum(1 for _, d in DIL_PAIRS if d > 1)
    return pl.pallas_call(
        _mix1_ffn_kernel,
        out_shape=jax.ShapeDtypeStruct((b, s, D_MODEL), F32),
        grid=(b, s // tm),
        in_specs=[row] + folded + folded
        + [pl.BlockSpec(w_bf16.shape, const), pl.BlockSpec((1, D_MODEL), const)] + _ffn_specs(const),
        out_specs=row,
        scratch_shapes=[pltpu.VMEM((n_slots * DIL_GROUP_WIDTH // LANES, tm, LANES), F32)],
        compiler_params=_params("parallel", "parallel"),
        name="mix1_ffn",
    )(x3d, *outs, *lses, w_bf16, gain.reshape(1, -1), pre.reshape(1, -1), post.reshape(1, -1), wg, wu, wd)


def _block_diag2(top, bottom):
    z_tr = jnp.zeros((top.shape[0], bottom.shape[1]), top.dtype)
    z_bl = jnp.zeros((bottom.shape[0], top.shape[1]), top.dtype)
    return jnp.concatenate([jnp.concatenate([top, z_tr], axis=1),
                            jnp.concatenate([z_bl, bottom], axis=1)], axis=0)


def kernel(x_prompt, x_sample, mix_pre0, mix_post0, w_in0, lam_q1, lam_k1, lam_q2, lam_k2, subln_w,
           mu_r, mu_k, mu_v, mu_w, mu_a, mu_g, w0_f, w1_f, w2_f, w0_b, w1_b, w2_b,
           a0_f, a1_f, a2_f, a0_b, a1_b, a2_b, g1, g2, k_k, k_a, r_k, lnx_w, lnx_b, w_out0,
           ffn_pre0, ffn_post0, ffn_gate0, ffn_up0, ffn_down0,
           mix_pre1, mix_post1, w_in1, w_out1, ffn_pre1, ffn_post1, ffn_gate1, ffn_up1, ffn_down1):
    bf = lambda a: a.astype(BF16)
    row = lambda a: a.reshape(1, -1).astype(F32)
    lam_init = 0.8 - 0.6 * math.exp(-0.3 * 0)
    lamq = jnp.stack([lam_q1, lam_q2]).astype(F32)
    lamk = jnp.stack([lam_k1, lam_k2]).astype(F32)
    gate_pad = 2 * LANES - g1.shape[1]
    ones_bd = _block_diag2(jnp.ones((HEAD_DIM, HEAD_DIM), BF16), jnp.ones((HEAD_DIM, HEAD_DIM), BF16))
    prep_w = (
        jnp.stack([mu_w, mu_a, mu_g]).astype(F32),
        row(jnp.concatenate([mu_r, mu_k, mu_v])),
        bf(jnp.concatenate([w1_f, w1_b], axis=1)),
        bf(_block_diag2(w2_f, w2_b)),
        bf(jnp.concatenate([a1_f, a1_b], axis=1)),
        bf(_block_diag2(a2_f, a2_b)),
        bf(jnp.pad(g1, ((0, 0), (0, gate_pad)))),
        bf(jnp.pad(g2, ((0, gate_pad), (0, 0)))),
        row(jnp.concatenate([w0_f, w0_b])),
        row(jnp.concatenate([a0_f, a0_b])),
        row(k_k), row(k_a), row(r_k.reshape(-1)),
        ones_bd,
    )
    w_in0_b, w_out0_b, w_in1_b, w_out1_b = bf(w_in0), bf(w_out0), bf(w_in1), bf(w_out1)
    ffn0 = (ffn_pre0, ffn_post0, bf(ffn_gate0), bf(ffn_up0), bf(ffn_down0))
    ffn1 = (ffn_pre1, ffn_post1, bf(ffn_gate1), bf(ffn_up1), bf(ffn_down1))

    def run(x):
        b, s, _ = x.shape
        x2d = x.reshape(b * s, D_MODEL)
        cos, sin = _rope_tables(s)
        (q, k, v, r, vv, kk, lwf, lwb, kf, kb, bfw, bbw, bonus, gate) = _proj0_prep(
            x, mix_pre0, w_in0_b, cos, sin, prep_w)
        out_a = _diff_attention(q, k, v, lamq, lamk, subln_w, lam_init)
        yf, yb = _rwkv_scan(r, vv, kk, lwf, kf, bfw, lwb, kb, bbw)
        fl = lambda a: a.reshape(b * s, -1)
        x2 = _mix0_ffn(x2d, fl(out_a), fl(yf), fl(yb), fl(bonus), fl(gate), lnx_w, lnx_b, ones_bd,
                       w_out0_b, mix_post0, ffn0)
        x2 = x2.reshape(b, s, D_MODEL)
        qkv1 = _norm_proj1(x2, mix_pre1, w_in1_b, cos, sin)
        outs, lses = [], []
        for gi, (_, dilation) in enumerate(DIL_PAIRS):
            o, lse = _band_attention(qkv1[gi], qkv1[3 + gi], qkv1[6 + gi], dilation)
            outs.append(o)
            lses.append(lse)
        return _mix1_ffn(x2, outs, lses, w_out1_b, mix_post1, ffn1)

    return (run(x_prompt), run(x_sample))
```

```python
import functools
import math

import jax
import jax.numpy as jnp
from jax import lax
from jax.experimental import pallas as pl
from jax.experimental.pallas import tpu as pltpu

F32 = jnp.float32
BF16 = jnp.bfloat16

D_MODEL = 1024
HEAD_DIM = 64
LANES = 128
MXU_COLS = 256
DIFF_WIDTH = 512
RWKV_WIDTH = 512
N_PAIRS = RWKV_WIDTH // LANES
DIL_PAIRS = ((128, 1), (512, 4), (2048, 16))
DIL_GROUP_WIDTH = 256
DIL_WIDTH = 768
DIL_RADIUS = 64
FFN_HIDDEN = 2816
ROPE_THETA = 10000.0
NORM_EPS = 1e-6
SUBLN_EPS = 1e-5
RWKV_GN_EPS = 64e-5
NEG_INF = -1e30
LOG2E = math.log2(math.e)
CHUNK = 64
VMEM_LIMIT = 56 * 1024 * 1024

NT_DIMS = (((1,), (1,)), ((), ()))
TN_DIMS = (((0,), (0,)), ((), ()))


def _params(*sem):
    return pltpu.CompilerParams(dimension_semantics=sem, vmem_limit_bytes=VMEM_LIMIT)


def _sigmoid(x):
    return 1.0 / (1.0 + jnp.exp(-x))


def _rms(x, gain, eps):
    return x * lax.rsqrt(jnp.mean(x * x, axis=-1, keepdims=True) + eps) * gain


def _rope_tile(x, cos, sin, upper):
    rot = jnp.where(upper, pltpu.roll(x, 32, 1), pltpu.roll(x, 96, 1))
    return x * cos + rot * sin


def _rope_tables(seq):
    half = HEAD_DIM // 2
    inv = ROPE_THETA ** (-jnp.arange(half, dtype=F32) / half)
    ang = jnp.arange(seq, dtype=F32)[:, None] * inv[None, :]
    cos = jnp.cos(ang)
    sin = jnp.sin(ang)
    cos_t = jnp.tile(jnp.concatenate([cos, cos], axis=-1), (1, LANES // HEAD_DIM))
    sin_t = jnp.tile(jnp.concatenate([-sin, sin], axis=-1), (1, LANES // HEAD_DIM))
    return cos_t, sin_t


def _diff_attn_kernel(lamq_ref, lamk_ref, subln_ref, q_ref, k_ref, v_ref, o_ref,
                      m_ref, l_ref, acc_ref, s_ref, *, seq, tk, lam_init):
    q = q_ref[...]
    tq = q.shape[0]
    lane = lax.broadcasted_iota(jnp.int32, (1, LANES), 1)
    zero = jnp.zeros_like(q)
    qs = (jnp.where(lane < HEAD_DIM, q, zero), jnp.where(lane >= HEAD_DIM, q, zero))
    m_ref[...] = jnp.full(m_ref.shape, -jnp.inf, F32)
    l_ref[...] = jnp.zeros(l_ref.shape, F32)
    acc_ref[...] = jnp.zeros(acc_ref.shape, F32)
    nck = tk // LANES
    nblk = seq // tk
    row_parts = tq // 256

    def scores(j, slot):
        off = pl.multiple_of(j * tk, tk)
        kj = k_ref[pl.ds(off, tk), :]
        for c in range(2):
            s_ref[slot, c] = lax.dot_general(qs[c], kj, NT_DIMS, preferred_element_type=F32)

    def consume(j, slot):
        off = pl.multiple_of(j * tk, tk)
        vj = v_ref[pl.ds(off, tk), :]
        for c in range(2):
            for h in range(row_parts):
                rows = slice(h * tq // row_parts, (h + 1) * tq // row_parts)
                cols = [s_ref[slot, c, rows, i * LANES:(i + 1) * LANES] for i in range(nck)]
                mx = cols[0]
                for col in cols[1:]:
                    mx = jnp.maximum(mx, col)
                m_old = m_ref[c, rows, :]
                m_new = jnp.maximum(m_old, jnp.max(mx, axis=-1, keepdims=True))
                alpha = jnp.exp2(m_old - m_new)
                ps = [jnp.exp2(col - m_new) for col in cols]
                lsum = ps[0]
                for pc in ps[1:]:
                    lsum = lsum + pc
                l_ref[c, rows, :] = alpha * l_ref[c, rows, :] + lsum
                p = jnp.concatenate([pc.astype(BF16) for pc in ps], axis=1)
                acc_ref[c, rows, :] = (alpha * acc_ref[c, rows, :]
                                       + jnp.dot(p, vj, preferred_element_type=F32))
                m_ref[c, rows, :] = m_new

    scores(0, 0)

    def body(i, carry):
        scores(2 * i + 1, 1)
        consume(2 * i, 0)
        scores(2 * i + 2, 0)
        consume(2 * i + 1, 1)
        return carry

    lax.fori_loop(0, nblk // 2 - 1, body, 0)
    scores(nblk - 1, 1)
    consume(nblk - 2, 0)
    consume(nblk - 1, 1)

    e = jnp.exp(jnp.sum(lamq_ref[...] * lamk_ref[...], axis=-1, keepdims=True))
    lam = e[0:1] - e[1:2] + lam_init
    l0 = jnp.sum(l_ref[0], axis=-1, keepdims=True)
    l1 = jnp.sum(l_ref[1], axis=-1, keepdims=True)
    o = acc_ref[0] / l0 - lam * (acc_ref[1] / l1)
    o_ref[...] = (_rms(o, subln_ref[...], SUBLN_EPS) * (1.0 - lam_init)).astype(o_ref.dtype)


def _diff_attention(q, k, v, lamq, lamk, subln_w, lam_init, tq=1024, tk=1024):
    b, s, _ = q.shape
    assert s % tq == 0 and s % (2 * tk) == 0, (s, tq, tk)
    heads = DIFF_WIDTH // LANES
    kern = functools.partial(_diff_attn_kernel, seq=s, tk=tk, lam_init=lam_init)
    return pl.pallas_call(
        kern,
        out_shape=jax.ShapeDtypeStruct((b, s, DIFF_WIDTH), BF16),
        grid=(b, heads, s // tq),
        in_specs=[
            pl.BlockSpec((2, HEAD_DIM), lambda bi, h, i: (0, 0)),
            pl.BlockSpec((2, HEAD_DIM), lambda bi, h, i: (0, 0)),
            pl.BlockSpec((1, LANES), lambda bi, h, i: (0, 0)),
            pl.BlockSpec((None, tq, LANES), lambda bi, h, i: (bi, i, h)),
            pl.BlockSpec((None, s, LANES), lambda bi, h, i: (bi, 0, h)),
            pl.BlockSpec((None, s, LANES), lambda bi, h, i: (bi, 0, h)),
        ],
        out_specs=pl.BlockSpec((None, tq, LANES), lambda bi, h, i: (bi, i, h)),
        scratch_shapes=[
            pltpu.VMEM((2, tq, LANES), F32),
            pltpu.VMEM((2, tq, LANES), F32),
            pltpu.VMEM((2, tq, LANES), F32),
            pltpu.VMEM((2, 2, tq, tk), F32),
        ],
        compiler_params=_params("parallel", "parallel", "parallel"),
        name="diff_attn",
    )(lamq, lamk, subln_w.reshape(1, -1), q, k, v)


def _cshift(x, prev_row, next_row):
    t = x.shape[0]
    row = lax.broadcasted_iota(jnp.int32, (t, 1), 0)
    p = jnp.where(row == 0, prev_row, pltpu.roll(x, 1, 0))
    n = jnp.where(row == t - 1, next_row, pltpu.roll(x, t - 1, 0))
    return 0.5 * (p + n)


def _head_sum(x, ones_bd):
    hi = x.astype(BF16)
    lo = (x - hi.astype(F32)).astype(BF16)
    parts = []
    for p in range(x.shape[1] // LANES):
        sl = slice(p * LANES, (p + 1) * LANES)
        parts.append(jnp.dot(hi[:, sl], ones_bd, preferred_element_type=F32)
                     + jnp.dot(lo[:, sl], ones_bd, preferred_element_type=F32))
    return jnp.concatenate(parts, axis=1)


def _proj0_prep_kernel(x_ref, xp_ref, xq_ref, gain_ref, w_ref, cos_ref, sin_ref,
                       mux_ref, mut_ref, w1_ref, w2_ref, a1_ref, a2_ref, g1_ref, g2_ref,
                       w0_ref, a0_ref, kk_ref, ka_ref, rk_ref, bd_ref,
                       q_out, k_out, v_out,
                       r_out, rv_out, kk_out, lwf_out, lwb_out, kf_out, kb_out, bf_out, bb_out,
                       bonus_out, g_out):
    i = pl.program_id(1)
    first = jnp.where(i > 0, 1.0, 0.0).astype(F32)
    last = jnp.where(i < pl.num_programs(1) - 1, 1.0, 0.0).astype(F32)
    gain = gain_ref[...]
    xn = _rms(x_ref[...], gain, NORM_EPS)
    xn_p = _rms(xp_ref[...], gain, NORM_EPS) * first
    xn_q = _rms(xq_ref[...], gain, NORM_EPS) * last
    xb = xn.astype(BF16)

    cos = cos_ref[...]
    sin = sin_ref[...]
    lane = lax.broadcasted_iota(jnp.int32, (1, LANES), 1)
    upper = (lane % HEAD_DIM) >= (HEAD_DIM // 2)
    for kind, ref in enumerate((q_out, k_out, v_out)):
        for c in range(DIFF_WIDTH // MXU_COLS):
            col = kind * DIFF_WIDTH + c * MXU_COLS
            y2 = jnp.dot(xb, w_ref[:, col:col + MXU_COLS], preferred_element_type=F32)
            for h in range(MXU_COLS // LANES):
                y = y2[:, h * LANES:(h + 1) * LANES]
                if kind < 2:
                    y = _rope_tile(y, cos, sin, upper)
                if kind == 0:
                    y = y * (HEAD_DIM ** -0.5 * LOG2E)
                lo = c * MXU_COLS + h * LANES
                ref[:, lo:lo + LANES] = y.astype(BF16)

    rkv0 = 3 * DIFF_WIDTH
    rows = x_ref.shape[0]
    xb_ext = jnp.concatenate([xb, xn_p.astype(BF16), xn_q.astype(BF16)], axis=0)
    t_ext = jnp.concatenate(
        [jnp.dot(xb_ext, w_ref[:, rkv0 + c * MXU_COLS:rkv0 + (c + 1) * MXU_COLS], preferred_element_type=F32)
         for c in range(3 * RWKV_WIDTH // MXU_COLS)], axis=1)
    t = t_ext[:rows]
    t_p = t_ext[rows:rows + 8]
    t_q = t_ext[rows + 8:]

    xx = _cshift(xn, xn_p[7:8, :], xn_q[0:1, :]) - xn
    mux = mux_ref[...]
    xw = (xn + xx * mux[0:1]).astype(BF16)
    xa = (xn + xx * mux[1:2]).astype(BF16)
    xg = (xn + xx * mux[2:3]).astype(BF16)

    ts = t + (_cshift(t, t_p[7:8, :], t_q[0:1, :]) - t) * mut_ref[...]
    r = ts[:, 0:RWKV_WIDTH]
    k = ts[:, RWKV_WIDTH:2 * RWKV_WIDTH]
    v = ts[:, 2 * RWKV_WIDTH:3 * RWKV_WIDTH]

    hw = jnp.tanh(jnp.dot(xw, w1_ref[...], preferred_element_type=F32))
    dec = jnp.dot(hw.astype(BF16), w2_ref[...], preferred_element_type=F32) + w0_ref[...]
    ha = jnp.dot(xa, a1_ref[...], preferred_element_type=F32)
    rate = _sigmoid(jnp.dot(ha.astype(BF16), a2_ref[...], preferred_element_type=F32) + a0_ref[...])
    hg = _sigmoid(jnp.dot(xg, g1_ref[...], preferred_element_type=F32))
    g_out[...] = jnp.dot(hg.astype(BF16), g2_ref[...], preferred_element_type=F32).astype(g_out.dtype)

    lw = -math.exp(-0.5) * _sigmoid(dec)
    lwf_out[...] = lw[:, 0:RWKV_WIDTH]
    lwb_out[...] = lw[:, RWKV_WIDTH:]

    bd = bd_ref[...]
    kk = k * kk_ref[...]
    kk = kk / jnp.maximum(jnp.sqrt(_head_sum(kk * kk, bd)), 1e-12)
    a_f = rate[:, 0:RWKV_WIDTH]
    a_b = rate[:, RWKV_WIDTH:]
    ka = ka_ref[...]
    k_f = k * (1.0 + (a_f - 1.0) * ka)
    k_b = k * (1.0 + (a_b - 1.0) * ka)
    store = lambda ref, val: ref.__setitem__(Ellipsis, val.astype(ref.dtype))
    store(r_out, r)
    store(rv_out, v)
    store(kk_out, kk)
    store(kf_out, k_f)
    store(kb_out, k_b)
    store(bf_out, kk * a_f)
    store(bb_out, kk * a_b)
    store(bonus_out, _head_sum(r * (0.5 * (k_f + k_b)) * rk_ref[...], bd) * v)


def _proj0_prep(x3d, gain, w_bf16, cos, sin, wts, ts=512):
    b, s, _ = x3d.shape
    assert s % ts == 0 and ts % 8 == 0, (s, ts)
    nb8 = s // 8
    r8 = ts // 8
    full = lambda a: pl.BlockSpec(a.shape, lambda bi, i: (0,) * a.ndim)
    resident = lambda a: pl.BlockSpec(a.shape, lambda bi, i: (0,) * a.ndim, pipeline_mode=pl.Buffered(1))
    in_specs = [
        pl.BlockSpec((None, ts, D_MODEL), lambda bi, i: (bi, i, 0)),
        pl.BlockSpec((None, 8, D_MODEL), lambda bi, i: (bi, jnp.maximum(i * r8 - 1, 0), 0)),
        pl.BlockSpec((None, 8, D_MODEL), lambda bi, i: (bi, jnp.minimum((i + 1) * r8, nb8 - 1), 0)),
        pl.BlockSpec((1, D_MODEL), lambda bi, i: (0, 0)),
        resident(w_bf16),
        pl.BlockSpec((ts, LANES), lambda bi, i: (i, 0)),
        pl.BlockSpec((ts, LANES), lambda bi, i: (i, 0)),
    ] + [full(a) for a in wts]
    row = pl.BlockSpec((None, ts, RWKV_WIDTH), lambda bi, i: (bi, i, 0))
    dtypes = [BF16] * 3 + [BF16, BF16, BF16, F32, F32, BF16, BF16, BF16, BF16, BF16, BF16]
    return pl.pallas_call(
        _proj0_prep_kernel,
        out_shape=[jax.ShapeDtypeStruct((b, s, RWKV_WIDTH), dt) for dt in dtypes],
        grid=(b, s // ts),
        in_specs=in_specs,
        out_specs=[row] * len(dtypes),
        compiler_params=_params("parallel", "parallel"),
        name="proj0_prep",
    )(x3d, x3d, x3d, gain.reshape(1, -1), w_bf16, cos, sin, *wts)


def _mm(a, b):
    return jnp.dot(a, b, preferred_element_type=F32)


def _mm_nt(a, b):
    return lax.dot_general(a, b, NT_DIMS, preferred_element_type=F32)


def _mm_tn(a, b):
    return lax.dot_general(a, b, TN_DIMS, preferred_element_type=F32)


def _stack(x, m0):
    zero = jnp.zeros_like(x)
    return jnp.concatenate([jnp.where(m0, x, zero), jnp.where(m0, zero, x)], axis=0)


def _cumsum_rows(x, reverse):
    n = x.shape[0]
    row = lax.broadcasted_iota(jnp.int32, (n, 1), 0)
    k = 1
    while k < n:
        if reverse:
            x = x + jnp.where(row < n - k, pltpu.roll(x, n - k, 0), 0.0)
        else:
            x = x + jnp.where(row >= k, pltpu.roll(x, k, 0), 0.0)
        k *= 2
    return x


def _chunk_local(jobs, masks):
    eye, eye_side, same, m0 = masks["eye"], masks["eye_side"], masks["same"], masks["m0"]
    c = CHUNK
    cat0 = lambda *xs: jnp.concatenate(xs, axis=0)
    cat1 = lambda *xs: jnp.concatenate(xs, axis=1)
    st = lambda x: _stack(x, m0)
    ops = []
    for jb in jobs:
        r32 = jb["r"] * jb["p_inc"]
        ops.append(dict(
            a=(-jb["a"] * jb["p_exc"]).astype(BF16), r32=r32, r=r32.astype(BF16),
            b=(jb["b"] * jb["p_inv"]).astype(BF16), k=(jb["k"] * jb["p_inv"]).astype(BF16),
            v=jb["v"].astype(BF16),
            bh=(jb["b"] * jb["e_hat"]).astype(BF16), kh=(jb["k"] * jb["e_hat"]).astype(BF16)))
    gs = [_mm_nt(cat0(o["a"], o["r"]), cat0(st(o["b"]), st(o["k"]))) for o in ops]
    zero = jnp.zeros((c, LANES), F32)
    n_ab, a_ak, a_rb, a_rk = [], [], [], []
    for jb, g in zip(jobs, gs):
        strict, incl = masks["strict"][jb["dir"]], masks["incl"][jb["dir"]]
        n_ab.append(jnp.where(strict, g[:c, :LANES], zero))
        a_ak.append(jnp.where(strict, g[:c, LANES:], zero).astype(BF16))
        a_rb.append(jnp.where(incl, g[c:, :LANES], zero).astype(BF16))
        a_rk.append(jnp.where(incl, g[c:, LANES:], zero).astype(BF16))
    v_st = [st(o["v"]) for o in ops]
    akv = [_mm(m, v) for m, v in zip(a_ak, v_st)]
    minv = [jnp.where(eye_side, 1.0, 0.0).astype(F32) + n for n in n_ab]
    pw = [n.astype(BF16) for n in n_ab]
    pw = [_mm(p, st(p)).astype(BF16) for p in pw]
    for _ in range(int(math.log2(CHUNK)) - 2):
        res = [_mm(cat0(p, m.astype(BF16)), st(p)) for p, m in zip(pw, minv)]
        pw = [r_[:c].astype(BF16) for r_ in res]
        minv = [m + r_[c:] for m, r_ in zip(minv, res)]
    minv = [m + _mm(m.astype(BF16), st(p)) for m, p in zip(minv, pw)]
    xs = [_mm(m.astype(BF16), cat1(st(o["a"]), st(u.astype(BF16)))) for m, o, u in zip(minv, ops, akv)]
    out = []
    zero_b = jnp.zeros((c, LANES), BF16)
    for jb, o, x, rb_, rk_, vs in zip(jobs, ops, xs, a_rb, a_rk, v_st):
        w1 = x[:, :LANES].astype(BF16)
        u_loc = x[:, LANES:].astype(BF16)
        y_loc = _mm(cat1(rb_, rk_), cat0(st(u_loc), vs))
        rw = o["r32"] + _mm(rb_, st(w1))
        pd = _mm_tn(cat0(o["bh"], o["kh"]), cat0(cat1(w1, u_loc), cat1(zero_b, o["v"])))
        phi = jnp.where(eye, jb["p_tot"], 0.0) + jnp.where(same, pd[:, :LANES], 0.0)
        dm = jnp.where(same, pd[:, LANES:], 0.0)
        out.append((rw.astype(BF16), y_loc, phi.astype(BF16), dm))
    return out


def _rwkv_scan_kernel(rf_ref, vf_ref, af_ref, lwf_ref, kf_ref, bf_ref,
                      rb_ref, vb_ref, ab_ref, lwb_ref, kb_ref, bb_ref,
                      yf_ref, yb_ref, state_ref, *, nsub):
    @pl.when(pl.program_id(1) == 0)
    def _():
        state_ref[...] = jnp.zeros(state_ref.shape, F32)

    n2 = 2 * CHUNK
    ri = lax.broadcasted_iota(jnp.int32, (n2, n2), 0)
    ci = lax.broadcasted_iota(jnp.int32, (n2, n2), 1)
    t_side = lax.broadcasted_iota(jnp.int32, (CHUNK, LANES), 0)
    s_side = lax.broadcasted_iota(jnp.int32, (CHUNK, LANES), 1) % CHUNK
    masks = dict(
        eye=ri == ci,
        same=(ri // CHUNK) == (ci // CHUNK),
        eye_side=s_side == t_side,
        m0=lax.broadcasted_iota(jnp.int32, (1, LANES), 1) < HEAD_DIM,
        strict=(s_side < t_side, s_side > t_side),
        incl=(s_side <= t_side, s_side >= t_side),
    )
    dirs = (
        (rf_ref, vf_ref, af_ref, lwf_ref, kf_ref, bf_ref, yf_ref),
        (rb_ref, vb_ref, ab_ref, lwb_ref, kb_ref, bb_ref, yb_ref),
    )
    jobs = []
    for d, (r_ref, v_ref, a_ref, lw_ref, k_ref, b_ref, _) in enumerate(dirs):
        tot_row = CHUNK - 1 if d == 0 else 0
        for sub in range(nsub):
            rows = slice(sub * CHUNK, (sub + 1) * CHUNK)
            lw = lw_ref[rows, :]
            cum = _cumsum_rows(lw, reverse=(d == 1))
            tot = cum[tot_row:tot_row + 1, :]
            rowops = dict(r=r_ref[rows, :], v=v_ref[rows, :], a=a_ref[rows, :], k=k_ref[rows, :],
                          b=b_ref[rows, :], p_inc=jnp.exp(cum), p_inv=jnp.exp(-cum),
                          p_exc=jnp.exp(cum - lw), e_hat=jnp.exp(tot - cum), p_tot=jnp.exp(tot))
            for p in range(N_PAIRS):
                sl = slice(p * LANES, (p + 1) * LANES)
                job = {name: val[:, sl] for name, val in rowops.items()}
                job.update(dir=d, sub=sub, pair=p)
                jobs.append(job)
    local = _chunk_local(jobs, masks)
    by_key = {(jb["dir"], jb["sub"], jb["pair"]): loc for jb, loc in zip(jobs, local)}
    states = {(d, p): state_ref[d, p] for d in range(2) for p in range(N_PAIRS)}
    for step in range(nsub):
        for d in range(2):
            sub = step if d == 0 else nsub - 1 - step
            y_ref = dirs[d][-1]
            for p in range(N_PAIRS):
                rw, y_loc, phi, dm = by_key[(d, sub, p)]
                res = _mm(jnp.concatenate([rw, phi], axis=0), states[(d, p)].astype(BF16))
                states[(d, p)] = res[CHUNK:] + dm
                y_ref[sub * CHUNK:(sub + 1) * CHUNK, p * LANES:(p + 1) * LANES] = res[:CHUNK] + y_loc
    for (d, p), t in states.items():
        state_ref[d, p] = t


def _rwkv_scan(r, v, kk, lwf, kf, bf, lwb, kb, bb, nsub=4):
    b, s, _ = r.shape
    tb = nsub * CHUNK
    assert s % tb == 0, (s, tb)
    nb = s // tb
    fwd = pl.BlockSpec((None, tb, RWKV_WIDTH), lambda bi, c: (bi, c, 0))
    bwd = pl.BlockSpec((None, tb, RWKV_WIDTH), lambda bi, c: (bi, nb - 1 - c, 0))
    return pl.pallas_call(
        functools.partial(_rwkv_scan_kernel, nsub=nsub),
        out_shape=[jax.ShapeDtypeStruct((b, s, RWKV_WIDTH), F32)] * 2,
        grid=(b, nb),
        in_specs=[fwd] * 6 + [bwd] * 6,
        out_specs=[fwd, bwd],
        scratch_shapes=[pltpu.VMEM((2, N_PAIRS, LANES, LANES), F32)],
        compiler_params=_params("parallel", "arbitrary"),
        name="rwkv_scan",
    )(r, v, kk, lwf, kf, bf, r, v, kk, lwb, kb, bb)


FFN_CHUNK = 256


def _ffn_apply(x, pre, post, wg_ref, wu_ref, wd_ref):
    xn = _rms(x, pre, NORM_EPS).astype(BF16)
    hs = []
    for c in range(FFN_HIDDEN // FFN_CHUNK):
        cols = slice(c * FFN_CHUNK, (c + 1) * FFN_CHUNK)
        gate = jnp.dot(xn, wg_ref[:, cols], preferred_element_type=F32)
        up = jnp.dot(xn, wu_ref[:, cols], preferred_element_type=F32)
        hs.append((gate * _sigmoid(gate) * up).astype(BF16))
    acc = jnp.dot(jnp.concatenate(hs, axis=1), wd_ref[...], preferred_element_type=F32)
    return x + _rms(acc, post, NORM_EPS)


def _ffn_specs(index_map):
    resident = lambda shape: pl.BlockSpec(shape, index_map, pipeline_mode=pl.Buffered(1))
    return [pl.BlockSpec((1, D_MODEL), index_map), pl.BlockSpec((1, D_MODEL), index_map),
            resident((D_MODEL, FFN_HIDDEN)), resident((D_MODEL, FFN_HIDDEN)), resident((FFN_HIDDEN, D_MODEL))]


def _mix0_ffn_kernel(x_ref, oa_ref, yf_ref, yb_ref, bonus_ref, g_ref, lnw_ref, lnb_ref, bd_ref,
                     w_ref, gain_ref, pre_ref, post_ref, wg_ref, wu_ref, wd_ref, o_ref):
    y = yf_ref[...] + yb_ref[...]
    bd = bd_ref[...]
    mean = _head_sum(y, bd) * (1.0 / HEAD_DIM)
    yc = y - mean
    var = _head_sum(yc * yc, bd) * (1.0 / HEAD_DIM)
    yn = yc * lax.rsqrt(var + RWKV_GN_EPS) * lnw_ref[...] + lnb_ref[...]
    ob = (yn + bonus_ref[...]) * g_ref[...]
    m = (jnp.dot(oa_ref[...].astype(BF16), w_ref[0:DIFF_WIDTH, :], preferred_element_type=F32)
         + jnp.dot(ob.astype(BF16), w_ref[DIFF_WIDTH:, :], preferred_element_type=F32))
    x1 = x_ref[...] + _rms(m, gain_ref[...], NORM_EPS)
    o_ref[...] = _ffn_apply(x1, pre_ref[...], post_ref[...], wg_ref, wu_ref, wd_ref)


def _mix0_ffn(x2d, oa, yf, yb, bonus, g, lnw, lnb, bd, w_bf16, gain, ffn, tm=512):
    m = x2d.shape[0]
    assert m % tm == 0, (m, tm)
    pre, post, wg, wu, wd = ffn
    row = lambda w: pl.BlockSpec((tm, w), lambda i: (i, 0))
    const = lambda i: (0, 0)
    small = (lnw.reshape(1, -1), lnb.reshape(1, -1), bd, w_bf16, gain.reshape(1, -1))
    return pl.pallas_call(
        _mix0_ffn_kernel,
        out_shape=jax.ShapeDtypeStruct((m, D_MODEL), F32),
        grid=(m // tm,),
        in_specs=([row(D_MODEL)] + [row(RWKV_WIDTH)] * 5 + [pl.BlockSpec(a.shape, const) for a in small]
                  + _ffn_specs(const)),
        out_specs=row(D_MODEL),
        compiler_params=_params("parallel"),
        name="mix0_ffn",
    )(x2d, oa, yf, yb, bonus, g, *small, pre.reshape(1, -1), post.reshape(1, -1), wg, wu, wd)


def _proj1_kernel(x_ref, g_ref, w_ref, cos_ref, sin_ref, *refs):
    out_refs, scr = refs[:-1], refs[-1]
    tm = x_ref.shape[0]
    xb = _rms(x_ref[...], g_ref[...], NORM_EPS).astype(BF16)
    cos = cos_ref[...]
    sin = sin_ref[...]
    lane = lax.broadcasted_iota(jnp.int32, (1, LANES), 1)
    upper = (lane % HEAD_DIM) >= (HEAD_DIM // 2)
    slot = 0
    for kind in range(3):
        for gi, (_, dil) in enumerate(DIL_PAIRS):
            ref = out_refs[kind * len(DIL_PAIRS) + gi]
            col = kind * DIL_WIDTH + gi * DIL_GROUP_WIDTH
            y2 = jnp.dot(xb, w_ref[:, col:col + DIL_GROUP_WIDTH], preferred_element_type=F32)
            for c in range(DIL_GROUP_WIDTH // LANES):
                y = y2[:, c * LANES:(c + 1) * LANES]
                if kind < 2:
                    y = _rope_tile(y, cos, sin, upper)
                if kind == 0:
                    y = y * (HEAD_DIM ** -0.5 * LOG2E)
                if dil == 1:
                    ref[:, c * LANES:(c + 1) * LANES] = y.astype(BF16)
                    continue
                scr[slot] = y
                for rho in range(dil):
                    rows = scr[slot, pl.ds(rho, tm // dil, stride=dil), :]
                    lo = rho * DIL_GROUP_WIDTH + c * LANES
                    ref[:, lo:lo + LANES] = rows.astype(BF16)
                slot += 1


def _norm_proj1(x3d, gain, w_bf16, cos, sin, tm=512):
    b, s, _ = x3d.shape
    assert s % tm == 0 and all((tm // d) % 16 == 0 for _, d in DIL_PAIRS), (s, tm)
    n_fold = sum(1 for _, d in DIL_PAIRS if d > 1) * 3 * (DIL_GROUP_WIDTH // LANES)
    out_shape, out_specs = [], []
    for _ in range(3):
        for _, d in DIL_PAIRS:
            out_shape.append(jax.ShapeDtypeStruct((b, s // d, d * DIL_GROUP_WIDTH), BF16))
            out_specs.append(pl.BlockSpec((None, tm // d, d * DIL_GROUP_WIDTH), lambda bi, i: (bi, i, 0)))
    return pl.pallas_call(
        _proj1_kernel,
        out_shape=out_shape,
        grid=(b, s // tm),
        in_specs=[
            pl.BlockSpec((None, tm, D_MODEL), lambda bi, i: (bi, i, 0)),
            pl.BlockSpec((1, D_MODEL), lambda bi, i: (0, 0)),
            pl.BlockSpec(w_bf16.shape, lambda bi, i: (0, 0)),
            pl.BlockSpec((tm, LANES), lambda bi, i: (i, 0)),
            pl.BlockSpec((tm, LANES), lambda bi, i: (i, 0)),
        ],
        out_specs=out_specs,
        scratch_shapes=[pltpu.VMEM((n_fold, tm, LANES), F32)],
        compiler_params=_params("parallel", "parallel"),
        name="norm_proj1",
    )(x3d, gain.reshape(1, -1), w_bf16, cos, sin)


def _band_attn_kernel(q_ref, kp_ref, kc_ref, kn_ref, vp_ref, vc_ref, vn_ref, o_ref, lse_ref, *,
                      length, nsub, nres):
    qb = LANES
    halo = DIL_RADIUS
    wlen = qb + 2 * halo
    l0 = pl.program_id(2) * (nsub * qb)
    m0 = lax.broadcasted_iota(jnp.int32, (1, LANES), 1) < HEAD_DIM
    ti = lax.broadcasted_iota(jnp.int32, (2 * qb, wlen), 0) % qb
    ji = lax.broadcasted_iota(jnp.int32, (2 * qb, wlen), 1)
    band_bias = jnp.where(jnp.abs(ji - halo - ti) <= halo, 0.0, NEG_INF).astype(F32)
    jcol = lax.broadcasted_iota(jnp.int32, (1, wlen), 1)

    def window(p_ref, c_ref, n_ref, j, cols):
        lo = j * qb - halo
        parts = []
        if lo < 0:
            parts.append(p_ref[:, cols])
            lo = 0
        hi = min((j + 1) * qb + halo, nsub * qb)
        parts.append(c_ref[lo:hi, cols])
        if (j + 1) * qb + halo > nsub * qb:
            parts.append(n_ref[:, cols])
        return jnp.concatenate(parts, axis=0) if len(parts) > 1 else parts[0]

    jobs = [(r, j, p) for r in range(nres) for j in range(nsub) for p in range(DIL_GROUP_WIDTH // LANES)]
    scores = []
    for r, j, p in jobs:
        cols = slice(r * DIL_GROUP_WIDTH + p * LANES, r * DIL_GROUP_WIDTH + (p + 1) * LANES)
        q_st = _stack(q_ref[j * qb:(j + 1) * qb, cols], m0)
        s = _mm_nt(q_st, window(kp_ref, kc_ref, kn_ref, j, cols))
        s = s + band_bias
        if j == 0 or j == nsub - 1:
            kpos0 = l0 + j * qb - halo
            s = s + jnp.where((jcol + kpos0 >= 0) & (jcol + kpos0 < length), 0.0, NEG_INF).astype(F32)
        scores.append(s)
    stats = []
    for s in scores:
        mx = jnp.max(s, axis=-1, keepdims=True)
        pr = jnp.exp2(s - mx)
        stats.append((mx, jnp.sum(pr, axis=-1, keepdims=True), pr.astype(BF16)))
    for (r, j, p), (mx, den, pr) in zip(jobs, stats):
        cols = slice(r * DIL_GROUP_WIDTH + p * LANES, r * DIL_GROUP_WIDTH + (p + 1) * LANES)
        o_st = _mm(pr, window(vp_ref, vc_ref, vn_ref, j, cols)) / den
        lse = (mx + jnp.log2(den)) * (1.0 / LOG2E)
        o_ref[j * qb:(j + 1) * qb, cols] = jnp.where(m0, o_st[:qb], o_st[qb:]).astype(o_ref.dtype)
        lse_ref[j * qb:(j + 1) * qb, cols] = jnp.where(m0, lse[:qb], lse[qb:])


def _band_attention(q, k, v, dilation):
    b, length, width = q.shape
    assert length % LANES == 0 and width == dilation * DIL_GROUP_WIDTH, (length, width, dilation)
    nsub = min(4, length // LANES)
    nres = min(dilation, 4 // nsub)
    tq = nsub * LANES
    hb = tq // DIL_RADIUS
    nh = length // DIL_RADIUS
    bw = nres * DIL_GROUP_WIDTH
    cur = lambda bi, r, i: (bi, i, r)
    prev = lambda bi, r, i: (bi, jnp.maximum(i * hb - 1, 0), r)
    nxt = lambda bi, r, i: (bi, jnp.minimum((i + 1) * hb, nh - 1), r)
    main = pl.BlockSpec((None, tq, bw), cur)
    hp = pl.BlockSpec((None, DIL_RADIUS, bw), prev)
    hn = pl.BlockSpec((None, DIL_RADIUS, bw), nxt)
    return pl.pallas_call(
        functools.partial(_band_attn_kernel, length=length, nsub=nsub, nres=nres),
        out_shape=[jax.ShapeDtypeStruct((b, length, width), BF16),
                   jax.ShapeDtypeStruct((b, length, width), F32)],
        grid=(b, dilation // nres, length // tq),
        in_specs=[main, hp, main, hn, hp, main, hn],
        out_specs=[main, main],
        compiler_params=_params("parallel", "parallel", "parallel"),
        name="band_attn",
    )(q, k, k, k, v, v, v)


def _mix1_ffn_kernel(x_ref, o0_ref, o1_ref, o2_ref, l0_ref, l1_ref, l2_ref, w_ref, gain_ref,
                     pre_ref, post_ref, wg_ref, wu_ref, wd_ref, out_ref, scr):
    tm = x_ref.shape[0]

    def unfold(ref, dil, slot):
        if dil == 1:
            return ref[...]
        halves = DIL_GROUP_WIDTH // LANES
        for rho in range(dil):
            for c in range(halves):
                lo = rho * DIL_GROUP_WIDTH + c * LANES
                scr[slot * halves + c, pl.ds(rho, tm // dil, stride=dil), :] = ref[:, lo:lo + LANES].astype(F32)
        return jnp.concatenate([scr[slot * halves + c] for c in range(halves)], axis=1)

    dils = [d for _, d in DIL_PAIRS]
    os_, ls, slot = [], [], 0
    for o_ref, l_ref, d in zip((o0_ref, o1_ref, o2_ref), (l0_ref, l1_ref, l2_ref), dils):
        os_.append(unfold(o_ref, d, slot))
        ls.append(unfold(l_ref, d, slot + 1))
        slot += 2 if d > 1 else 0
    mx = jnp.maximum(jnp.maximum(ls[0], ls[1]), ls[2])
    es = [jnp.exp(l - mx) for l in ls]
    den = es[0] + es[1] + es[2]
    m = jnp.zeros((tm, D_MODEL), F32)
    for gi in range(3):
        y = (os_[gi] * (es[gi] / den)).astype(BF16)
        m = m + jnp.dot(y, w_ref[gi * DIL_GROUP_WIDTH:(gi + 1) * DIL_GROUP_WIDTH, :],
                        preferred_element_type=F32)
    x1 = x_ref[...] + _rms(m, gain_ref[...], NORM_EPS)
    out_ref[...] = _ffn_apply(x1, pre_ref[...], post_ref[...], wg_ref, wu_ref, wd_ref)


def _mix1_ffn(x3d, outs, lses, w_bf16, gain, ffn, tm=512):
    b, s, _ = x3d.shape
    assert s % tm == 0 and all((tm // d) % 16 == 0 for _, d in DIL_PAIRS), (s, tm)
    pre, post, wg, wu, wd = ffn
    const = lambda bi, i: (0, 0)
    row = pl.BlockSpec((None, tm, D_MODEL), lambda bi, i: (bi, i, 0))
    folded = [pl.BlockSpec((None, tm // d, d * DIL_GROUP_WIDTH), lambda bi, i: (bi, i, 0)) for _, d in DIL_PAIRS]
    n_slots = 2 * sum(1 for _, d in DIL_PAIRS if d > 1)
    return pl.pallas_call(
        _mix1_ffn_kernel,
        out_shape=jax.ShapeDtypeStruct((b, s, D_MODEL), F32),
        grid=(b, s // tm),
        in_specs=[row] + folded + folded
        + [pl.BlockSpec(w_bf16.shape, const), pl.BlockSpec((1, D_MODEL), const)] + _ffn_specs(const),
        out_specs=row,
        scratch_shapes=[pltpu.VMEM((n_slots * DIL_GROUP_WIDTH // LANES, tm, LANES), F32)],
        compiler_params=_params("parallel", "parallel"),
        name="mix1_ffn",
    )(x3d, *outs, *lses, w_bf16, gain.reshape(1, -1), pre.reshape(1, -1), post.reshape(1, -1), wg, wu, wd)


def _block_diag2(top, bottom):
    z_tr = jnp.zeros((top.shape[0], bottom.shape[1]), top.dtype)
    z_bl = jnp.zeros((bottom.shape[0], top.shape[1]), top.dtype)
    return jnp.concatenate([jnp.concatenate([top, z_tr], axis=1),
                            jnp.concatenate([z_bl, bottom], axis=1)], axis=0)


def kernel(x_prompt, x_sample, mix_pre0, mix_post0, w_in0, lam_q1, lam_k1, lam_q2, lam_k2, subln_w,
           mu_r, mu_k, mu_v, mu_w, mu_a, mu_g, w0_f, w1_f, w2_f, w0_b, w1_b, w2_b,
           a0_f, a1_f, a2_f, a0_b, a1_b, a2_b, g1, g2, k_k, k_a, r_k, lnx_w, lnx_b, w_out0,
           ffn_pre0, ffn_post0, ffn_gate0, ffn_up0, ffn_down0,
           mix_pre1, mix_post1, w_in1, w_out1, ffn_pre1, ffn_post1, ffn_gate1, ffn_up1, ffn_down1):
    bf = lambda a: a.astype(BF16)
    row = lambda a: a.reshape(1, -1).astype(F32)
    lam_init = 0.8 - 0.6 * math.exp(-0.3 * 0)
    lamq = jnp.stack([lam_q1, lam_q2]).astype(F32)
    lamk = jnp.stack([lam_k1, lam_k2]).astype(F32)
    gate_pad = 2 * LANES - g1.shape[1]
    ones_bd = _block_diag2(jnp.ones((HEAD_DIM, HEAD_DIM), BF16), jnp.ones((HEAD_DIM, HEAD_DIM), BF16))
    prep_w = (
        jnp.stack([mu_w, mu_a, mu_g]).astype(F32),
        row(jnp.concatenate([mu_r, mu_k, mu_v])),
        bf(jnp.concatenate([w1_f, w1_b], axis=1)),
        bf(_block_diag2(w2_f, w2_b)),
        bf(jnp.concatenate([a1_f, a1_b], axis=1)),
        bf(_block_diag2(a2_f, a2_b)),
        bf(jnp.pad(g1, ((0, 0), (0, gate_pad)))),
        bf(jnp.pad(g2, ((0, gate_pad), (0, 0)))),
        row(jnp.concatenate([w0_f, w0_b])),
        row(jnp.concatenate([a0_f, a0_b])),
        row(k_k), row(k_a), row(r_k.reshape(-1)),
        ones_bd,
    )
    w_in0_b, w_out0_b, w_in1_b, w_out1_b = bf(w_in0), bf(w_out0), bf(w_in1), bf(w_out1)
    ffn0 = (ffn_pre0, ffn_post0, bf(ffn_gate0), bf(ffn_up0), bf(ffn_down0))
    ffn1 = (ffn_pre1, ffn_post1, bf(ffn_gate1), bf(ffn_up1), bf(ffn_down1))

    def run(x):
        b, s, _ = x.shape
        x2d = x.reshape(b * s, D_MODEL)
        cos, sin = _rope_tables(s)
        (q, k, v, r, vv, kk, lwf, lwb, kf, kb, bfw, bbw, bonus, gate) = _proj0_prep(
            x, mix_pre0, w_in0_b, cos, sin, prep_w)
        out_a = _diff_attention(q, k, v, lamq, lamk, subln_w, lam_init)
        yf, yb = _rwkv_scan(r, vv, kk, lwf, kf, bfw, lwb, kb, bbw)
        fl = lambda a: a.reshape(b * s, -1)
        x2 = _mix0_ffn(x2d, fl(out_a), fl(yf), fl(yb), fl(bonus), fl(gate), lnx_w, lnx_b, ones_bd,
                       w_out0_b, mix_post0, ffn0)
        x2 = x2.reshape(b, s, D_MODEL)
        qkv1 = _norm_proj1(x2, mix_pre1, w_in1_b, cos, sin)
        outs, lses = [], []
        for gi, (_, dilation) in enumerate(DIL_PAIRS):
            o, lse = _band_attention(qkv1[gi], qkv1[3 + gi], qkv1[6 + gi], dilation)
            outs.append(o)
            lses.append(lse)
        return _mix1_ffn(x2, outs, lses, w_out1_b, mix_post1, ffn1)

    return (run(x_prompt), run(x_sample))
```
